```python
import math
import jax, jax.numpy as jnp
from jax import lax
import numpy as np

D_MODEL = 1024
BATCH = 8
SEQ = 4096
DEPTH = 1

RMS_EPS = 1e-6
MIX_WIDTH = D_MODEL
POOL_WIDTH = MIX_WIDTH // 2
SSM_WIDTH = MIX_WIDTH - POOL_WIDTH
POOL_WINDOWS = (2, 4, 8, 16)
POOL_GROUPS = len(POOL_WINDOWS)
POOL_GROUP_DIM = POOL_WIDTH // POOL_GROUPS
SSM_GROUP_DIM = 16
SSM_GROUPS = SSM_WIDTH // SSM_GROUP_DIM
SSM_STATE = 64
DT_MIN, DT_MAX = 1e-3, 1e-1
N_EXPERTS = 32
TOP_K = 4
D_FF = D_MODEL
SWIGLU_ALPHA = 1.702
SWIGLU_LIMIT = 7.0
MOE_BLOCK = 256
IN_PROJ_WIDTH = MIX_WIDTH + 2 * D_MODEL

kernel_name = "hybrid_pool_s5_gated_moe_block"


def _rmsnorm(x, g):
    xf = x.astype(jnp.float32)
    y = xf * lax.rsqrt(jnp.mean(xf * xf, axis=-1, keepdims=True) + RMS_EPS)
    return (y * g.astype(jnp.float32)).astype(x.dtype)


def _pool_mixer(u, pool_w, pool_scale):
    bsz, t_len, _ = u.shape
    uf = u.astype(jnp.float32)
    pos = jnp.arange(1, t_len + 1, dtype=jnp.float32)[None, :, None]
    outs = []
    for g, w in enumerate(POOL_WINDOWS):
        ug = uf[..., g * POOL_GROUP_DIM:(g + 1) * POOL_GROUP_DIM]
        cs = jnp.cumsum(ug, axis=1)
        lag = jnp.pad(cs[:, :t_len - w], ((0, 0), (w, 0), (0, 0)))
        mean = (cs - lag) / jnp.minimum(pos, float(w))
        outs.append(mean - ug)
    d = jnp.stack(outs, axis=2).astype(u.dtype)
    y = jnp.einsum('btgc,gcd->btgd', d, pool_w).reshape(bsz, t_len, POOL_WIDTH)
    return y * pool_scale


def _complex_affine_combine(e1, e2):
    ar1, ai1, br1, bi1 = e1
    ar2, ai2, br2, bi2 = e2
    ar = ar2 * ar1 - ai2 * ai1
    ai = ar2 * ai1 + ai2 * ar1
    br = ar2 * br1 - ai2 * bi1 + br2
    bi = ar2 * bi1 + ai2 * br1 + bi2
    return (ar, ai, br, bi)


def _ssm_mixer(u, lam_re, lam_im, log_dt, b_re, b_im, c_re, c_im, d_skip, glu_w, glu_b):
    f32 = jnp.float32
    bsz, t_len, _ = u.shape
    uf = u.astype(f32).reshape(bsz, t_len, SSM_GROUPS, SSM_GROUP_DIM)
    lr, li = lam_re.astype(f32), lam_im.astype(f32)
    dt = jnp.exp(log_dt.astype(f32))[:, None]
    mag = jnp.exp(lr * dt)
    lb_re, lb_im = mag * jnp.cos(li * dt), mag * jnp.sin(li * dt)
    den = lr * lr + li * li
    xr, xi = lb_re - 1.0, lb_im
    f_re = (xr * lr + xi * li) / den
    f_im = (xi * lr - xr * li) / den
    br, bi = b_re.astype(f32), b_im.astype(f32)
    bb_re = f_re[..., None] * br - f_im[..., None] * bi
    bb_im = f_re[..., None] * bi + f_im[..., None] * br
    bu_re = jnp.einsum('btgp,gnp->btgn', uf, bb_re)
    bu_im = jnp.einsum('btgp,gnp->btgn', uf, bb_im)
    a_re = jnp.broadcast_to(lb_re, bu_re.shape)
    a_im = jnp.broadcast_to(lb_im, bu_im.shape)
    _, _, s_re, s_im = lax.associative_scan(
        _complex_affine_combine, (a_re, a_im, bu_re, bu_im), axis=1)
    y = (jnp.einsum('btgn,gpn->btgp', s_re, c_re.astype(f32))
         - jnp.einsum('btgn,gpn->btgp', s_im, c_im.astype(f32))
         + d_skip.astype(f32) * uf)
    y = jax.nn.gelu(y.reshape(bsz, t_len, SSM_WIDTH))
    y = y * jax.nn.sigmoid(y @ glu_w.astype(f32) + glu_b.astype(f32))
    return y.astype(u.dtype)


def _clamped_swiglu(h):
    x_glu = jnp.minimum(h[..., ::2], SWIGLU_LIMIT)
    x_lin = jnp.clip(h[..., 1::2], -SWIGLU_LIMIT, SWIGLU_LIMIT)
    return x_glu * jax.nn.sigmoid(SWIGLU_ALPHA * x_glu) * (x_lin + 1.0)


def _moe(h, router_w, router_b, w1, b1, w2, b2):
    bsz, t_len, d = h.shape
    xt = h.reshape(-1, d)
    m = xt.shape[0]
    logits = (xt @ router_w + router_b).astype(jnp.float32)
    top_val, top_idx = lax.top_k(logits, TOP_K)
    gate = jax.nn.softmax(top_val, axis=-1).astype(h.dtype)
    n_assign = m * TOP_K
    n_slots = ((n_assign + N_EXPERTS * (MOE_BLOCK - 1)) + MOE_BLOCK - 1) // MOE_BLOCK * MOE_BLOCK
    n_blocks = n_slots // MOE_BLOCK
    flat_e = top_idx.reshape(-1)
    flat_tok = jnp.arange(n_assign, dtype=jnp.int32) // TOP_K
    flat_w = gate.reshape(-1)
    order = jnp.argsort(flat_e)
    sorted_e, sorted_tok, sorted_w = flat_e[order], flat_tok[order], flat_w[order]
    counts = jnp.bincount(flat_e, length=N_EXPERTS)
    padded = (counts + MOE_BLOCK - 1) // MOE_BLOCK * MOE_BLOCK
    pad_end = jnp.cumsum(padded)
    pad_start = pad_end - padded
    start = jnp.cumsum(counts) - counts
    dest = pad_start[sorted_e] + (jnp.arange(n_assign, dtype=jnp.int32) - start[sorted_e])
    slot_tok = jnp.zeros((n_slots,), jnp.int32).at[dest].set(sorted_tok)
    slot_w = jnp.zeros((n_slots,), h.dtype).at[dest].set(sorted_w)
    blk_e = jnp.minimum(
        jnp.searchsorted(pad_end, jnp.arange(n_blocks, dtype=jnp.int32) * MOE_BLOCK, side='right'),
        N_EXPERTS - 1)
    xs = xt[slot_tok].reshape(n_blocks, MOE_BLOCK, d)

    def expert_block(args):
        xb, e = args
        act = _clamped_swiglu(xb @ w1[e] + b1[e])
        return act @ w2[e] + b2[e]

    ys = lax.map(expert_block, (xs, blk_e)).reshape(n_slots, d) * slot_w[:, None]
    out = jax.ops.segment_sum(ys, slot_tok, num_segments=m)
    return out.reshape(bsz, t_len, d)


def setup_inputs(seed: int = 0) -> dict:
    key = jax.random.key(seed)
    ks = jax.random.split(key, 32)
    f32 = jnp.float32
    L = DEPTH

    def nrm(k, shape, scale):
        return jax.random.normal(k, shape, f32) * scale

    x = jax.random.normal(ks[0], (BATCH, SEQ, D_MODEL), f32)
    norm_mix_g = 1.0 + nrm(ks[1], (L, D_MODEL), 0.02)
    w_in = nrm(ks[2], (L, D_MODEL, IN_PROJ_WIDTH), D_MODEL ** -0.5)
    b_gate = nrm(ks[3], (L, 2 * D_MODEL), 0.02)
    pool_w = nrm(ks[4], (L, POOL_GROUPS, POOL_GROUP_DIM, POOL_GROUP_DIM), POOL_GROUP_DIM ** -0.5)
    pool_scale = 1.0 + nrm(ks[5], (L, POOL_WIDTH), 0.1)
    n_idx = jnp.arange(SSM_STATE, dtype=f32)
    ssm_lambda_re = -0.5 + nrm(ks[6], (L, SSM_GROUPS, SSM_STATE), 0.01)
    ssm_lambda_im = math.pi * n_idx + nrm(ks[7], (L, SSM_GROUPS, SSM_STATE), 0.01)
    ssm_log_dt = jax.random.uniform(ks[8], (L, SSM_GROUPS), f32, math.log(DT_MIN), math.log(DT_MAX))
    b_scale = (2.0 * SSM_GROUP_DIM) ** -0.5
    ssm_b_re = nrm(ks[9], (L, SSM_GROUPS, SSM_STATE, SSM_GROUP_DIM), b_scale)
    ssm_b_im = nrm(ks[10], (L, SSM_GROUPS, SSM_STATE, SSM_GROUP_DIM), b_scale)
    c_scale = (2.0 * SSM_STATE) ** -0.5
    ssm_c_re = nrm(ks[11], (L, SSM_GROUPS, SSM_GROUP_DIM, SSM_STATE), c_scale)
    ssm_c_im = nrm(ks[12], (L, SSM_GROUPS, SSM_GROUP_DIM, SSM_STATE), c_scale)
    ssm_d = nrm(ks[13], (L, SSM_GROUPS, SSM_GROUP_DIM), 1.0)
    ssm_glu_w = nrm(ks[14], (L, SSM_WIDTH, SSM_WIDTH), SSM_WIDTH ** -0.5)
    ssm_glu_b = nrm(ks[15], (L, SSM_WIDTH), 0.02)
    w_pool_up = nrm(ks[16], (L, POOL_WIDTH, D_MODEL), POOL_WIDTH ** -0.5)
    w_ssm_up = nrm(ks[17], (L, SSM_WIDTH, D_MODEL), SSM_WIDTH ** -0.5)
    w_out = nrm(ks[18], (L, D_MODEL, D_MODEL), D_MODEL ** -0.5)
    norm_ffn_g = 1.0 + nrm(ks[19], (L, D_MODEL), 0.02)
    router_w = nrm(ks[20], (L, D_MODEL, N_EXPERTS), D_MODEL ** -0.5)
    router_b = nrm(ks[21], (L, N_EXPERTS), 0.01)
    moe_w1 = nrm(ks[22], (L, N_EXPERTS, D_MODEL, 2 * D_FF), D_MODEL ** -0.5)
    moe_b1 = nrm(ks[23], (L, N_EXPERTS, 2 * D_FF), 0.02)
    moe_w2 = nrm(ks[24], (L, N_EXPERTS, D_FF, D_MODEL), D_FF ** -0.5)
    moe_b2 = nrm(ks[25], (L, N_EXPERTS, D_MODEL), 0.02)
    norm_final_g = 1.0 + nrm(ks[26], (D_MODEL,), 0.02)
    return {
        "x": x, "norm_mix_g": norm_mix_g, "w_in": w_in, "b_gate": b_gate,
        "pool_w": pool_w, "pool_scale": pool_scale,
        "ssm_lambda_re": ssm_lambda_re, "ssm_lambda_im": ssm_lambda_im, "ssm_log_dt": ssm_log_dt,
        "ssm_b_re": ssm_b_re, "ssm_b_im": ssm_b_im, "ssm_c_re": ssm_c_re, "ssm_c_im": ssm_c_im,
        "ssm_d": ssm_d, "ssm_glu_w": ssm_glu_w, "ssm_glu_b": ssm_glu_b,
        "w_pool_up": w_pool_up, "w_ssm_up": w_ssm_up, "w_out": w_out,
        "norm_ffn_g": norm_ffn_g, "router_w": router_w, "router_b": router_b,
        "moe_w1": moe_w1, "moe_b1": moe_b1, "moe_w2": moe_w2, "moe_b2": moe_b2,
        "norm_final_g": norm_final_g,
    }


def reference(x, norm_mix_g, w_in, b_gate, pool_w, pool_scale,
              ssm_lambda_re, ssm_lambda_im, ssm_log_dt, ssm_b_re, ssm_b_im, ssm_c_re, ssm_c_im,
              ssm_d, ssm_glu_w, ssm_glu_b, w_pool_up, w_ssm_up, w_out,
              norm_ffn_g, router_w, router_b, moe_w1, moe_b1, moe_w2, moe_b2,
              norm_final_g):
    for l in range(DEPTH):
        h = _rmsnorm(x, norm_mix_g[l])
        proj = h @ w_in[l]
        u_pool = proj[..., :POOL_WIDTH]
        u_ssm = proj[..., POOL_WIDTH:MIX_WIDTH]
        gates = jax.nn.sigmoid(proj[..., MIX_WIDTH:] + b_gate[l])
        g_pool, g_ssm = gates[..., :D_MODEL], gates[..., D_MODEL:]
        y_pool = _pool_mixer(u_pool, pool_w[l], pool_scale[l]) @ w_pool_up[l]
        y_ssm = _ssm_mixer(u_ssm, ssm_lambda_re[l], ssm_lambda_im[l], ssm_log_dt[l],
                           ssm_b_re[l], ssm_b_im[l], ssm_c_re[l], ssm_c_im[l],
                           ssm_d[l], ssm_glu_w[l], ssm_glu_b[l]) @ w_ssm_up[l]
        x = x + (g_pool * y_pool + g_ssm * y_ssm) @ w_out[l]
        x = x + _moe(_rmsnorm(x, norm_ffn_g[l]), router_w[l], router_b[l],
                     moe_w1[l], moe_b1[l], moe_w2[l], moe_b2[l])
    return _rmsnorm(x, norm_final_g)
```

```python
import functools
import math

import jax
import jax.numpy as jnp
from jax import lax
from jax.experimental import pallas as pl
from jax.experimental.pallas import tpu as pltpu

F32 = jnp.float32
BF16 = jnp.bfloat16

RMS_EPS = 1e-6
POOL_WINDOWS = (2, 4, 8, 16)
POOL_HALO = 16
SSM_GROUP_DIM = 16
SSM_STATE = 64
N_EXPERTS = 32
TOP_K = 4
SWIGLU_ALPHA = 1.702
SWIGLU_LIMIT = 7.0

LANES = 128
SUBLANES = 8
SSM_CHUNK = 32
SSM_PAIR = 2
ROW_TILE = 256
MOE_ROWS = 256
VMEM_LIMIT = 56 * 1024 * 1024


def _cparams(n_axes):
    return pltpu.CompilerParams(
        dimension_semantics=("arbitrary",) * n_axes, vmem_limit_bytes=VMEM_LIMIT)


def _inproj_kernel(x_ref, g_ref, wp_ref, ws_ref, wg_ref, bg_ref, up_ref, us_ref, gate_ref):
    x = x_ref[...]
    ms = jnp.mean(x * x, axis=-1, keepdims=True)
    h = (x * lax.rsqrt(ms + RMS_EPS) * g_ref[...]).astype(BF16)
    up_ref[...] = jnp.dot(h, wp_ref[...], preferred_element_type=F32)
    us_ref[...] = jnp.dot(h, ws_ref[...], preferred_element_type=F32)
    gl = jnp.dot(h, wg_ref[...], preferred_element_type=F32) + bg_ref[...]
    gate_ref[...] = jax.nn.sigmoid(gl).astype(BF16)


def _inproj(x2, g, wp, ws, wg, bg):
    m, d = x2.shape
    pw, sw, gw = wp.shape[1], ws.shape[1], wg.shape[1]
    tm = ROW_TILE
    const = lambda i: (0, 0)
    row = lambda i: (i, 0)
    return pl.pallas_call(
        _inproj_kernel,
        grid=(m // tm,),
        in_specs=[
            pl.BlockSpec((tm, d), row),
            pl.BlockSpec((1, d), const),
            pl.BlockSpec((d, pw), const),
            pl.BlockSpec((d, sw), const),
            pl.BlockSpec((d, gw), const),
            pl.BlockSpec((1, gw), const),
        ],
        out_specs=[
            pl.BlockSpec((tm, pw), row),
            pl.BlockSpec((tm, sw), row),
            pl.BlockSpec((tm, gw), row),
        ],
        out_shape=[
            jax.ShapeDtypeStruct((m, pw), F32),
            jax.ShapeDtypeStruct((m, sw), F32),
            jax.ShapeDtypeStruct((m, gw), BF16),
        ],
        compiler_params=_cparams(1),
        name="inproj",
    )(x2, g, wp, ws, wg, bg)


def _ssm_matrices(lam_re, lam_im, log_dt, b_re, b_im, c_re, c_im, d_skip):
    hi = lax.Precision.HIGHEST
    L, P, N = SSM_CHUNK, SSM_GROUP_DIM, SSM_STATE
    G = lam_re.shape[0]
    lr, li = lam_re.astype(F32), lam_im.astype(F32)
    dt = jnp.exp(log_dt.astype(F32))[:, None]
    mag = jnp.exp(lr * dt)
    lb_re, lb_im = mag * jnp.cos(li * dt), mag * jnp.sin(li * dt)
    den = lr * lr + li * li
    xr, xi = lb_re - 1.0, lb_im
    f_re = (xr * lr + xi * li) / den
    f_im = (xi * lr - xr * li) / den
    br, bi = b_re.astype(F32), b_im.astype(F32)
    bb_re = f_re[..., None] * br - f_im[..., None] * bi
    bb_im = f_re[..., None] * bi + f_im[..., None] * br
    tau = jnp.arange(L + 1, dtype=F32)[None, :, None]
    pmag = jnp.exp(lr[:, None, :] * dt[:, None, :] * tau)
    pang = li[:, None, :] * dt[:, None, :] * tau
    pw_re, pw_im = pmag * jnp.cos(pang), pmag * jnp.sin(pang)
    ab_re = pw_re[:, :L, :, None] * bb_re[:, None] - pw_im[:, :L, :, None] * bb_im[:, None]
    ab_im = pw_re[:, :L, :, None] * bb_im[:, None] + pw_im[:, :L, :, None] * bb_re[:, None]
    cr, ci = c_re.astype(F32), c_im.astype(F32)
    kern = (jnp.einsum('gqn,gtnp->gtqp', cr, ab_re, precision=hi)
            - jnp.einsum('gqn,gtnp->gtqp', ci, ab_im, precision=hi))
    lag = jnp.arange(L)[None, :] - jnp.arange(L)[:, None]
    toep = kern[:, jnp.clip(lag, 0, L - 1)]
    toep = jnp.where((lag >= 0)[None, :, :, None, None], toep, 0.0)
    toep = toep.transpose(0, 1, 4, 2, 3).reshape(G, L * P, L * P)
    w_re = ab_re[:, ::-1].transpose(0, 1, 3, 2).reshape(G, L * P, N)
    w_im = ab_im[:, ::-1].transpose(0, 1, 3, 2).reshape(G, L * P, N)
    p1_re, p1_im = pw_re[:, 1:], pw_im[:, 1:]
    v_re = (cr[:, None] * p1_re[:, :, None, :] - ci[:, None] * p1_im[:, :, None, :])
    v_im = -(cr[:, None] * p1_im[:, :, None, :] + ci[:, None] * p1_re[:, :, None, :])
    v_re = v_re.transpose(0, 3, 1, 2).reshape(G, N, L * P)
    v_im = v_im.transpose(0, 3, 1, 2).reshape(G, N, L * P)
    half = (jnp.arange(G) % SSM_PAIR)[:, None, None]
    lane_half = (jnp.arange(SSM_PAIR * N) // N)[None, None, :]
    w_re = jnp.where(lane_half == half, jnp.tile(w_re, (1, 1, SSM_PAIR)), 0.0)
    w_im = jnp.where(lane_half == half, jnp.tile(w_im, (1, 1, SSM_PAIR)), 0.0)
    row_half = lane_half.transpose(0, 2, 1)
    v_re = jnp.where(row_half == half, jnp.tile(v_re, (1, SSM_PAIR, 1)), 0.0)
    v_im = jnp.where(row_half == half, jnp.tile(v_im, (1, SSM_PAIR, 1)), 0.0)
    al = jnp.stack([pw_re[:, L].reshape(G // SSM_PAIR, SSM_PAIR * N),
                    pw_im[:, L].reshape(G // SSM_PAIR, SSM_PAIR * N)], axis=1)
    dvec = jnp.tile(d_skip.astype(F32)[:, None, :], (1, L, 1)).reshape(G, 1, L * P)
    return (toep.astype(BF16), w_re.astype(BF16), w_im.astype(BF16),
            v_re.astype(BF16), v_im.astype(BF16), al, dvec)


def _ssm_kernel(n_chunks, u_ref, toep_ref, wre_ref, wim_ref, vre_ref, vim_ref, al_ref, d_ref,
                y_ref, ere_s, eim_s, sre_s, sim_s):
    ub = [u_ref[gi].astype(BF16) for gi in range(SSM_PAIR)]
    ere_s[...] = sum(jnp.dot(ub[gi], wre_ref[gi], preferred_element_type=F32) for gi in range(SSM_PAIR))
    eim_s[...] = sum(jnp.dot(ub[gi], wim_ref[gi], preferred_element_type=F32) for gi in range(SSM_PAIR))
    ar = jnp.broadcast_to(al_ref[0, 0:1, :], (SUBLANES, SSM_PAIR * SSM_STATE))
    ai = jnp.broadcast_to(al_ref[0, 1:2, :], (SUBLANES, SSM_PAIR * SSM_STATE))

    def step(c, carry):
        sr, si = carry
        off = pl.multiple_of(c * SUBLANES, SUBLANES)
        sre_s[pl.ds(off, SUBLANES), :] = sr
        sim_s[pl.ds(off, SUBLANES), :] = si
        er = ere_s[pl.ds(off, SUBLANES), :]
        ei = eim_s[pl.ds(off, SUBLANES), :]
        return ar * sr - ai * si + er, ar * si + ai * sr + ei

    zero = jnp.zeros((SUBLANES, SSM_PAIR * SSM_STATE), F32)
    lax.fori_loop(0, n_chunks, step, (zero, zero))
    srb = sre_s[...].astype(BF16)
    sib = sim_s[...].astype(BF16)
    for gi in range(SSM_PAIR):
        y = (jnp.dot(ub[gi], toep_ref[gi], preferred_element_type=F32)
             + jnp.dot(srb, vre_ref[gi], preferred_element_type=F32)
             + jnp.dot(sib, vim_ref[gi], preferred_element_type=F32)
             + d_ref[gi] * u_ref[gi])
        y_ref[gi] = jax.nn.gelu(y)


def _ssm(u_g, mats, n_chunks):
    toep, w_re, w_im, v_re, v_im, al, dvec = mats
    g, r, lp = u_g.shape
    ns = SSM_PAIR * SSM_STATE
    pair = lambda i: (i, 0, 0)
    return pl.pallas_call(
        functools.partial(_ssm_kernel, n_chunks),
        grid=(g // SSM_PAIR,),
        in_specs=[
            pl.BlockSpec((SSM_PAIR, r, lp), pair),
            pl.BlockSpec((SSM_PAIR, lp, lp), pair),
            pl.BlockSpec((SSM_PAIR, lp, ns), pair),
            pl.BlockSpec((SSM_PAIR, lp, ns), pair),
            pl.BlockSpec((SSM_PAIR, ns, lp), pair),
            pl.BlockSpec((SSM_PAIR, ns, lp), pair),
            pl.BlockSpec((1, 2, ns), pair),
            pl.BlockSpec((SSM_PAIR, 1, lp), pair),
        ],
        out_specs=pl.BlockSpec((SSM_PAIR, r, lp), pair),
        out_shape=jax.ShapeDtypeStruct((g, r, lp), F32),
        scratch_shapes=[pltpu.VMEM((r, ns), F32) for _ in range(4)],
        compiler_params=_cparams(1),
        name="ssm",
    )(u_g, toep, w_re, w_im, v_re, v_im, al, dvec)


def _mix_kernel(tiles_per_seq, x_ref, up_ref, halo_ref, ys_ref, gate_ref, poolw_ref, pscale_ref,
                gluw_ref, glub_ref, wpu_ref, wsu_ref, wout_ref, gffn_ref, rwt_ref, rb_ref, tri_ref,
                x1_ref, h3_ref, idx_ref, gw_ref, rank_ref, cnt_ref, ext_s, carry_s):
    i = pl.program_id(0)
    j = i % tiles_per_seq
    tm = x_ref.shape[0]
    d_model = x_ref.shape[1]
    gdim = poolw_ref.shape[1]

    @pl.when(i == 0)
    def _():
        carry_s[...] = jnp.zeros_like(carry_s)

    ext_s[0:POOL_HALO, :] = jnp.where(j == 0, 0.0, halo_ref[...])
    ext_s[POOL_HALO:, :] = up_ref[...]
    pos = (j * tm + 1 + lax.broadcasted_iota(jnp.int32, (tm, 1), 0)).astype(F32)
    parts = []
    for g, w in enumerate(POOL_WINDOWS):
        cols = slice(g * gdim, (g + 1) * gdim)
        cur = ext_s[POOL_HALO:POOL_HALO + tm, cols]
        s = cur
        for k in range(1, w):
            s = s + ext_s[POOL_HALO - k:POOL_HALO - k + tm, cols]
        dlt = (s / jnp.minimum(pos, float(w)) - cur).astype(BF16)
        parts.append(jnp.dot(dlt, poolw_ref[g], preferred_element_type=F32))
    yp = jnp.concatenate(parts, axis=1) * pscale_ref[...]
    y_pool = jnp.dot(yp.astype(BF16), wpu_ref[...], preferred_element_type=F32)

    yg = ys_ref[...]
    glu = yg * jax.nn.sigmoid(
        jnp.dot(yg.astype(BF16), gluw_ref[...], preferred_element_type=F32) + glub_ref[...])
    y_ssm = jnp.dot(glu.astype(BF16), wsu_ref[...], preferred_element_type=F32)

    z = (gate_ref[:, :d_model].astype(F32) * y_pool + gate_ref[:, d_model:].astype(F32) * y_ssm)
    x1 = x_ref[...] + jnp.dot(z.astype(BF16), wout_ref[...], preferred_element_type=F32)
    x1_ref[...] = x1

    ms = jnp.mean(x1 * x1, axis=-1, keepdims=True)
    h2 = x1 * lax.rsqrt(ms + RMS_EPS) * gffn_ref[...]
    for s in range(d_model // LANES):
        h3_ref[:, s, :] = h2[:, s * LANES:(s + 1) * LANES]

    logits = lax.dot_general(rwt_ref[...], h2, (((1,), (1,)), ((), ())),
                             precision=lax.Precision.HIGHEST,
                             preferred_element_type=F32) + rb_ref[...]
    n_e = logits.shape[0]
    iota_e = lax.broadcasted_iota(jnp.int32, (n_e, tm), 0)
    l = logits
    tops, hots = [], []
    for k in range(TOP_K):
        m = jnp.max(l, axis=0, keepdims=True)
        idx = jnp.min(jnp.where(l == m, iota_e, n_e), axis=0, keepdims=True)
        hot = iota_e == idx
        l = jnp.where(hot, -jnp.inf, l)
        tops.append(m)
        hots.append(hot)
        idx_ref[k:k + 1, :] = idx
    exps = [jnp.exp(m - tops[0]) for m in tops]
    den = exps[0] + exps[1] + exps[2] + exps[3]
    for k in range(TOP_K):
        gw_ref[k:k + 1, :] = exps[k] / den

    multi = sum(h.astype(F32) for h in hots)
    cum = jnp.dot(multi.astype(BF16), tri_ref[...], preferred_element_type=F32) + carry_s[:, 0:1]
    for k in range(TOP_K):
        rk = jnp.sum(jnp.where(hots[k], cum, 0.0), axis=0, keepdims=True)
        rank_ref[k:k + 1, :] = rk.astype(jnp.int32)
    carry_s[...] = carry_s[...] + jnp.sum(multi, axis=1, keepdims=True)
    cnt_ref[...] = carry_s[...]


def _mix(x2, u_pool, y_ssm, gates, pool_w, pool_scale, glu_w, glu_b, w_pool_up, w_ssm_up, w_out,
         g_ffn, rw_t, rb, seq_len):
    m, d = x2.shape
    pw = u_pool.shape[1]
    sw = y_ssm.shape[1]
    tm = ROW_TILE
    n_e = rw_t.shape[0]
    tiles_per_seq = seq_len // tm
    tri = (jnp.arange(tm)[:, None] < jnp.arange(tm)[None, :]).astype(BF16)
    row = lambda i: (i, 0)
    const2 = lambda i: (0, 0)
    const3 = lambda i: (0, 0, 0)
    col = lambda i: (0, i)
    halo = lambda i: (jnp.maximum(i * (tm // POOL_HALO) - 1, 0), 0)
    return pl.pallas_call(
        functools.partial(_mix_kernel, tiles_per_seq),
        grid=(m // tm,),
        in_specs=[
            pl.BlockSpec((tm, d), row),
            pl.BlockSpec((tm, pw), row),
            pl.BlockSpec((POOL_HALO, pw), halo),
            pl.BlockSpec((tm, sw), row),
            pl.BlockSpec((tm, 2 * d), row),
            pl.BlockSpec(pool_w.shape, const3),
            pl.BlockSpec((1, pw), const2),
            pl.BlockSpec((sw, sw), const2),
            pl.BlockSpec((1, sw), const2),
            pl.BlockSpec((pw, d), const2),
            pl.BlockSpec((sw, d), const2),
            pl.BlockSpec((d, d), const2),
            pl.BlockSpec((1, d), const2),
            pl.BlockSpec((n_e, d), const2),
            pl.BlockSpec((n_e, 1), const2),
            pl.BlockSpec((tm, tm), const2),
        ],
        out_specs=[
            pl.BlockSpec((tm, d), row),
            pl.BlockSpec((tm, d // LANES, LANES), lambda i: (i, 0, 0)),
            pl.BlockSpec((TOP_K, tm), col),
            pl.BlockSpec((TOP_K, tm), col),
            pl.BlockSpec((TOP_K, tm), col),
            pl.BlockSpec((n_e, LANES), const2),
        ],
        out_shape=[
            jax.ShapeDtypeStruct((m, d), F32),
            jax.ShapeDtypeStruct((m, d // LANES, LANES), F32),
            jax.ShapeDtypeStruct((TOP_K, m), jnp.int32),
            jax.ShapeDtypeStruct((TOP_K, m), F32),
            jax.ShapeDtypeStruct((TOP_K, m), jnp.int32),
            jax.ShapeDtypeStruct((n_e, LANES), F32),
        ],
        scratch_shapes=[
            pltpu.VMEM((POOL_HALO + tm, pw), F32),
            pltpu.VMEM((n_e, LANES), F32),
        ],
        compiler_params=_cparams(1),
        name="mix_route",
    )(x2, u_pool, u_pool, y_ssm, gates, pool_w, pool_scale, glu_w, glu_b, w_pool_up, w_ssm_up,
      w_out, g_ffn, rw_t, rb, tri)


def _row_copy(src, dst, sem):
    return pltpu.make_async_copy(src, dst, sem)


def _dispatch_kernel(dest_ref, zpos_ref, h3_ref, xs_ref, zero_s, sem):
    i = pl.program_id(0)
    tm = h3_ref.shape[0]
    zrows = zero_s.shape[0]

    @pl.when(i == 0)
    def _():
        zero_s[...] = jnp.zeros_like(zero_s)

        def fill(start):
            cp = _row_copy(zero_s, xs_ref.at[pl.ds(start, zrows)], sem)
            cp.start()
            cp.wait()

        for e in range(N_EXPERTS):
            fill(zpos_ref[e])

        def tail(t, carry):
            fill(pl.multiple_of(t * zrows, zrows))
            return carry

        lax.fori_loop(zpos_ref[N_EXPERTS], xs_ref.shape[0] // zrows, tail, 0)

    def issue(r, carry):
        for k in range(TOP_K):
            _row_copy(h3_ref.at[r], xs_ref.at[dest_ref[k * tm + r]], sem).start()
        return carry

    lax.fori_loop(0, tm, issue, 0)
    for k in range(TOP_K):
        _row_copy(h3_ref, xs_ref.at[pl.ds(0, tm)], sem).wait()


def _dispatch(dest_tiles, zpos, h3, n_slots):
    m, s, lanes = h3.shape
    tm = ROW_TILE
    return pl.pallas_call(
        _dispatch_kernel,
        grid=(m // tm,),
        in_specs=[
            pl.BlockSpec((TOP_K * tm,), lambda i: (i,), memory_space=pltpu.SMEM),
            pl.BlockSpec(memory_space=pltpu.SMEM),
            pl.BlockSpec((tm, s, lanes), lambda i: (i, 0, 0)),
        ],
        out_specs=pl.BlockSpec(memory_space=pl.ANY),
        out_shape=jax.ShapeDtypeStruct((n_slots + MOE_ROWS, s, lanes), F32),
        scratch_shapes=[pltpu.VMEM((MOE_ROWS, s, lanes), F32), pltpu.SemaphoreType.DMA(())],
        compiler_params=_cparams(1),
        name="dispatch",
    )(dest_tiles, zpos, h3)


def _expert_kernel(blk_e_ref, nused_ref, xs_ref, w1g_ref, w1l_ref, b1g_ref, b1l_ref, w2_ref, b2_ref,
                   ys_ref):
    i = pl.program_id(0)
    n_s = xs_ref.shape[1]

    @pl.when(i < nused_ref[0])
    def _():
        x = jnp.concatenate([xs_ref[:, s, :] for s in range(n_s)], axis=1).astype(BF16)
        hg = jnp.dot(x, w1g_ref[0], preferred_element_type=F32) + b1g_ref[0]
        hl = jnp.dot(x, w1l_ref[0], preferred_element_type=F32) + b1l_ref[0]
        xg = jnp.minimum(hg, SWIGLU_LIMIT)
        xl = jnp.clip(hl, -SWIGLU_LIMIT, SWIGLU_LIMIT)
        act = xg * jax.nn.sigmoid(SWIGLU_ALPHA * xg) * (xl + 1.0)
        y = jnp.dot(act.astype(BF16), w2_ref[0], preferred_element_type=F32) + b2_ref[0]
        for s in range(n_s):
            ys_ref[:, s, :] = y[:, s * LANES:(s + 1) * LANES]

    @pl.when(i >= nused_ref[0])
    def _():
        ys_ref[...] = jnp.zeros_like(ys_ref)


def _experts(blk_e, n_used, xs, w1g, w1l, b1g, b1l, w2, b2, n_blocks):
    _, s, lanes = xs.shape
    d = s * lanes
    f = w1g.shape[2]
    xmap = lambda i, be, nu: (jnp.minimum(i, nu[0] - 1), 0, 0)
    emap = lambda i, be, nu: (be[i], 0, 0)
    grid_spec = pltpu.PrefetchScalarGridSpec(
        num_scalar_prefetch=2,
        grid=(n_blocks,),
        in_specs=[
            pl.BlockSpec((MOE_ROWS, s, lanes), xmap),
            pl.BlockSpec((1, d, f), emap),
            pl.BlockSpec((1, d, f), emap),
            pl.BlockSpec((1, 1, f), emap),
            pl.BlockSpec((1, 1, f), emap),
            pl.BlockSpec((1, f, d), emap),
            pl.BlockSpec((1, 1, d), emap),
        ],
        out_specs=pl.BlockSpec((MOE_ROWS, s, lanes), lambda i, be, nu: (i, 0, 0)),
    )
    return pl.pallas_call(
        _expert_kernel,
        grid_spec=grid_spec,
        out_shape=jax.ShapeDtypeStruct((n_blocks * MOE_ROWS, s, lanes), F32),
        compiler_params=_cparams(1),
        name="experts",
    )(blk_e, n_used, xs, w1g, w1l, b1g, b1l, w2, b2)


def _combine_kernel(dest_ref, x1_ref, gw_ref, gfin_ref, ys_ref, out_ref, buf_s, sem):
    tm = x1_ref.shape[0]
    n_s = buf_s.shape[1]

    def issue(r, carry):
        _row_copy(ys_ref.at[dest_ref[r]], buf_s.at[r], sem).start()
        return carry

    lax.fori_loop(0, TOP_K * tm, issue, 0)
    for k in range(TOP_K):
        _row_copy(ys_ref.at[pl.ds(0, tm)], buf_s.at[pl.ds(0, tm)], sem).wait()

    acc = x1_ref[...]
    for k in range(TOP_K):
        yk = jnp.concatenate([buf_s[k * tm:(k + 1) * tm, s, :] for s in range(n_s)], axis=1)
        acc = acc + gw_ref[:, k:k + 1] * yk
    ms = jnp.mean(acc * acc, axis=-1, keepdims=True)
    out_ref[...] = acc * lax.rsqrt(ms + RMS_EPS) * gfin_ref[...]


def _combine(dest_tiles, x1, gw_rows, g_final, ys):
    m, d = x1.shape
    _, s, lanes = ys.shape
    tm = ROW_TILE
    return pl.pallas_call(
        _combine_kernel,
        grid=(m // tm,),
        in_specs=[
            pl.BlockSpec((TOP_K * tm,), lambda i: (i,), memory_space=pltpu.SMEM),
            pl.BlockSpec((tm, d), lambda i: (i, 0)),
            pl.BlockSpec((tm, TOP_K), lambda i: (i, 0)),
            pl.BlockSpec((1, d), lambda i: (0, 0)),
            pl.BlockSpec(memory_space=pl.ANY),
        ],
        out_specs=pl.BlockSpec((tm, d), lambda i: (i, 0)),
        out_shape=jax.ShapeDtypeStruct((m, d), F32),
        scratch_shapes=[pltpu.VMEM((TOP_K * tm, s, lanes), F32), pltpu.SemaphoreType.DMA(())],
        compiler_params=_cparams(1),
        name="combine",
    )(dest_tiles, x1, gw_rows, g_final, ys)


def kernel(x, norm_mix_g, w_in, b_gate, pool_w, pool_scale, ssm_lambda_re, ssm_lambda_im, ssm_log_dt, ssm_b_re, ssm_b_im, ssm_c_re, ssm_c_im, ssm_d, ssm_glu_w, ssm_glu_b, w_pool_up, w_ssm_up, w_out, norm_ffn_g, router_w, router_b, moe_w1, moe_b1, moe_w2, moe_b2, norm_final_g):
    bsz, seq, d = x.shape
    depth = w_in.shape[0]
    pw = pool_w.shape[1] * pool_w.shape[2]
    n_groups = ssm_lambda_re.shape[1]
    sw = n_groups * SSM_GROUP_DIM
    m = bsz * seq
    L = SSM_CHUNK
    n_chunks = seq // L
    assert depth == 1
    assert bsz == SUBLANES and seq % ROW_TILE == 0 and seq % L == 0 and d % LANES == 0
    assert pool_w.shape[1] == len(POOL_WINDOWS) and n_groups % SSM_PAIR == 0

    x2 = x.reshape(m, d)
    for l in range(depth):
        wi = w_in[l].astype(BF16)
        u_pool, u_ssm, gates = _inproj(
            x2, norm_mix_g[l][None], wi[:, :pw], wi[:, pw:pw + sw], wi[:, pw + sw:], b_gate[l][None])
        mats = _ssm_matrices(ssm_lambda_re[l], ssm_lambda_im[l], ssm_log_dt[l], ssm_b_re[l],
                             ssm_b_im[l], ssm_c_re[l], ssm_c_im[l], ssm_d[l])
        u_g = (u_ssm.reshape(bsz, n_chunks, L, n_groups, SSM_GROUP_DIM)
               .transpose(3, 1, 0, 2, 4).reshape(n_groups, n_chunks * bsz, L * SSM_GROUP_DIM))
        y_g = _ssm(u_g, mats, n_chunks)
        y_ssm = (y_g.reshape(n_groups, n_chunks, bsz, L, SSM_GROUP_DIM)
                 .transpose(2, 1, 3, 0, 4).reshape(m, sw))
        x1, h3, idx_t, gw_t, rank_t, cnt = _mix(
            x2, u_pool, y_ssm, gates, pool_w[l].astype(BF16), pool_scale[l][None],
            ssm_glu_w[l].astype(BF16), ssm_glu_b[l][None], w_pool_up[l].astype(BF16),
            w_ssm_up[l].astype(BF16), w_out[l].astype(BF16), norm_ffn_g[l][None],
            router_w[l].T, router_b[l][:, None], seq)

        n_assign = m * TOP_K
        n_blocks = (n_assign + N_EXPERTS * (MOE_ROWS - 1) + MOE_ROWS - 1) // MOE_ROWS
        n_slots = n_blocks * MOE_ROWS
        counts = cnt[:, 0].astype(jnp.int32)
        padded = (counts + MOE_ROWS - 1) // MOE_ROWS * MOE_ROWS
        pad_end = jnp.cumsum(padded)
        pad_start = pad_end - padded
        n_used = (pad_end[-1] // MOE_ROWS).astype(jnp.int32)[None]
        blk_e = jnp.minimum(
            jnp.searchsorted(pad_end, jnp.arange(n_blocks, dtype=jnp.int32) * MOE_ROWS, side='right'),
            N_EXPERTS - 1).astype(jnp.int32)
        dest = pad_start[idx_t] + rank_t
        dest_tiles = (dest.reshape(TOP_K, m // ROW_TILE, ROW_TILE)
                      .transpose(1, 0, 2).reshape(-1).astype(jnp.int32))
        zpos = jnp.concatenate([pad_start + counts, n_used]).astype(jnp.int32)

        xs = _dispatch(dest_tiles, zpos, h3, n_slots)
        w1 = moe_w1[l]
        ys = _experts(blk_e, n_used, xs,
                      w1[:, :, 0::2].astype(BF16), w1[:, :, 1::2].astype(BF16),
                      moe_b1[l][:, None, 0::2], moe_b1[l][:, None, 1::2],
                      moe_w2[l].astype(BF16), moe_b2[l][:, None, :], n_blocks)
        x2 = _combine(dest_tiles, x1, gw_t.T, norm_final_g[None], ys)
    return x2.reshape(bsz, seq, d)
```

```python
import functools
import math

import jax
import jax.numpy as jnp
from jax import lax
from jax.experimental import pallas as pl
from jax.experimental.pallas import tpu as pltpu

F32 = jnp.float32
BF16 = jnp.bfloat16

RMS_EPS = 1e-6
POOL_WINDOWS = (2, 4, 8, 16)
POOL_HALO = 16
SSM_GROUP_DIM = 16
SSM_STATE = 64
N_EXPERTS = 32
TOP_K = 4
SWIGLU_ALPHA = 1.702
SWIGLU_LIMIT = 7.0

LANES = 128
SUBLANES = 8
SSM_CHUNK = SUBLANES
SSM_PACK = LANES // SSM_GROUP_DIM
SSM_TILE = 256
DEINT = 2 * LANES
ROW_TILE = 256
MOE_ROWS = 256
VMEM_LIMIT = 56 * 1024 * 1024


def _cparams(n_axes):
    return pltpu.CompilerParams(
        dimension_semantics=("arbitrary",) * n_axes, vmem_limit_bytes=VMEM_LIMIT)


def _inproj_kernel(x_ref, g_ref, wp_ref, ws_ref, wg_ref, bg_ref, up_ref, us_ref, gate_ref):
    x = x_ref[...]
    ms = jnp.mean(x * x, axis=-1, keepdims=True)
    h = (x * lax.rsqrt(ms + RMS_EPS) * g_ref[...]).astype(BF16)
    up_ref[...] = jnp.dot(h, wp_ref[...], preferred_element_type=F32)
    us_ref[...] = jnp.dot(h, ws_ref[...], preferred_element_type=F32)
    gl = jnp.dot(h, wg_ref[...], preferred_element_type=F32) + bg_ref[...]
    gate_ref[...] = jax.nn.sigmoid(gl).astype(BF16)


def _inproj(x2, g, wp, ws, wg, bg):
    m, d = x2.shape
    pw, sw, gw = wp.shape[1], ws.shape[1], wg.shape[1]
    tm = ROW_TILE
    const = lambda i: (0, 0)
    row = lambda i: (i, 0)
    return pl.pallas_call(
        _inproj_kernel,
        grid=(m // tm,),
        in_specs=[
            pl.BlockSpec((tm, d), row),
            pl.BlockSpec((1, d), const),
            pl.BlockSpec((d, pw), const),
            pl.BlockSpec((d, sw), const),
            pl.BlockSpec((d, gw), const),
            pl.BlockSpec((1, gw), const),
        ],
        out_specs=[
            pl.BlockSpec((tm, pw), row),
            pl.BlockSpec((tm, sw), row),
            pl.BlockSpec((tm, gw), row),
        ],
        out_shape=[
            jax.ShapeDtypeStruct((m, pw), F32),
            jax.ShapeDtypeStruct((m, sw), F32),
            jax.ShapeDtypeStruct((m, gw), BF16),
        ],
        compiler_params=_cparams(1),
        name="inproj",
    )(x2, g, wp, ws, wg, bg)


def _ssm_matrices(lam_re, lam_im, log_dt, b_re, b_im, c_re, c_im, d_skip):
    hi = lax.Precision.HIGHEST
    L, P, N, GP = SSM_CHUNK, SSM_GROUP_DIM, SSM_STATE, SSM_PACK
    G = lam_re.shape[0]
    K = G // GP
    lr, li = lam_re.astype(F32), lam_im.astype(F32)
    dt = jnp.exp(log_dt.astype(F32))[:, None]
    mag = jnp.exp(lr * dt)
    lb_re, lb_im = mag * jnp.cos(li * dt), mag * jnp.sin(li * dt)
    den = lr * lr + li * li
    xr, xi = lb_re - 1.0, lb_im
    f_re = (xr * lr + xi * li) / den
    f_im = (xi * lr - xr * li) / den
    br, bi = b_re.astype(F32), b_im.astype(F32)
    bb_re = f_re[..., None] * br - f_im[..., None] * bi
    bb_im = f_re[..., None] * bi + f_im[..., None] * br
    tau = jnp.arange(L + 1, dtype=F32)[None, :, None]
    pmag = jnp.exp(lr[:, None, :] * dt[:, None, :] * tau)
    pang = li[:, None, :] * dt[:, None, :] * tau
    pw_re, pw_im = pmag * jnp.cos(pang), pmag * jnp.sin(pang)
    ab_re = pw_re[:, :L, :, None] * bb_re[:, None] - pw_im[:, :L, :, None] * bb_im[:, None]
    ab_im = pw_re[:, :L, :, None] * bb_im[:, None] + pw_im[:, :L, :, None] * bb_re[:, None]
    cr, ci = c_re.astype(F32), c_im.astype(F32)
    kern = (jnp.einsum('gqn,glnp->glqp', cr, ab_re, precision=hi)
            - jnp.einsum('gqn,glnp->glqp', ci, ab_im, precision=hi))
    lag = jnp.arange(L)[None, :] - jnp.arange(L)[:, None]
    sel = (lag[:, :, None] == jnp.arange(L)[None, None, :]).astype(F32)
    eye = jnp.eye(GP, dtype=F32)
    toep = jnp.einsum('jtl,kilqp,ih->kjipthq', sel, kern.reshape(K, GP, L, P, P), eye, precision=hi)
    toep = toep.reshape(K, L * GP * P, L * GP * P)
    w_re = jnp.einsum('kijnp,ih->kjiphn', ab_re[:, ::-1].reshape(K, GP, L, N, P), eye, precision=hi)
    w_im = jnp.einsum('kijnp,ih->kjiphn', ab_im[:, ::-1].reshape(K, GP, L, N, P), eye, precision=hi)
    w = jnp.concatenate([w_re.reshape(K, L * GP * P, GP * N), w_im.reshape(K, L * GP * P, GP * N)], axis=2)
    p1_re, p1_im = pw_re[:, 1:], pw_im[:, 1:]
    v_re = cr[:, None] * p1_re[:, :, None, :] - ci[:, None] * p1_im[:, :, None, :]
    v_im = -(cr[:, None] * p1_im[:, :, None, :] + ci[:, None] * p1_re[:, :, None, :])
    v_re = jnp.einsum('kitqn,ih->kinthq', v_re.reshape(K, GP, L, P, N), eye, precision=hi)
    v_im = jnp.einsum('kitqn,ih->kinthq', v_im.reshape(K, GP, L, P, N), eye, precision=hi)
    v = jnp.concatenate([v_re.reshape(K, GP * N, L * GP * P), v_im.reshape(K, GP * N, L * GP * P)], axis=1)
    al = jnp.stack([pw_re[:, L].reshape(K, GP * N), pw_im[:, L].reshape(K, GP * N)], axis=1)
    dvec = jnp.tile(d_skip.astype(F32).reshape(K, 1, GP * P), (1, L, 1)).reshape(K, 1, L * GP * P)
    return toep.astype(BF16), w.astype(BF16), v.astype(BF16), al, dvec


def _ssm_kernel(u_ref, toep_ref, w_ref, v_ref, al_ref, d_ref, y_ref, lhs_s, e_s, s_s, carry_s):
    bsz, tt, lanes = u_ref.shape
    L = SSM_CHUNK
    n_chunks = tt // L
    ns = al_ref.shape[2]

    @pl.when(pl.program_id(1) == 0)
    def _():
        carry_s[...] = jnp.zeros_like(carry_s)

    for b in range(bsz):
        for t in range(L):
            lhs_s[t, pl.ds(b, n_chunks, stride=bsz), :] = u_ref[b, pl.ds(t, n_chunks, stride=L), :]
    lhs = jnp.concatenate([lhs_s[t] for t in range(L)], axis=1)
    lb = lhs.astype(BF16)
    e_s[...] = jnp.dot(lb, w_ref[0], preferred_element_type=F32)
    ar = jnp.broadcast_to(al_ref[0, 0:1, :], (bsz, ns))
    ai = jnp.broadcast_to(al_ref[0, 1:2, :], (bsz, ns))
    sr = carry_s[0]
    si = carry_s[1]
    for c in range(n_chunks):
        rows = slice(c * bsz, (c + 1) * bsz)
        s_s[rows, :ns] = sr
        s_s[rows, ns:] = si
        er = e_s[rows, :ns]
        ei = e_s[rows, ns:]
        sr, si = ar * sr - ai * si + er, ar * si + ai * sr + ei
    carry_s[0] = sr
    carry_s[1] = si
    y = (jnp.dot(lb, toep_ref[0], preferred_element_type=F32)
         + jnp.dot(s_s[...].astype(BF16), v_ref[0], preferred_element_type=F32)
         + d_ref[0] * lhs)
    yg = jax.nn.gelu(y)
    for t in range(L):
        lhs_s[t] = yg[:, t * lanes:(t + 1) * lanes]
    for b in range(bsz):
        for t in range(L):
            y_ref[b, pl.ds(t, n_chunks, stride=L), :] = lhs_s[t, pl.ds(b, n_chunks, stride=bsz), :]


def _ssm(u3, mats):
    toep, w, v, al, dvec = mats
    bsz, seq, sw = u3.shape
    k = toep.shape[0]
    lanes = sw // k
    tt = SSM_TILE
    rows = bsz * tt // SSM_CHUNK
    cl = SSM_CHUNK * lanes
    ns = al.shape[2]
    pack = lambda p, t: (p, 0, 0)
    tile = lambda p, t: (0, t, p)
    return pl.pallas_call(
        _ssm_kernel,
        grid=(k, seq // tt),
        in_specs=[
            pl.BlockSpec((bsz, tt, lanes), tile),
            pl.BlockSpec((1, cl, cl), pack),
            pl.BlockSpec((1, cl, 2 * ns), pack),
            pl.BlockSpec((1, 2 * ns, cl), pack),
            pl.BlockSpec((1, 2, ns), pack),
            pl.BlockSpec((1, 1, cl), pack),
        ],
        out_specs=pl.BlockSpec((bsz, tt, lanes), tile),
        out_shape=jax.ShapeDtypeStruct((bsz, seq, sw), F32),
        scratch_shapes=[
            pltpu.VMEM((SSM_CHUNK, rows, lanes), F32),
            pltpu.VMEM((rows, 2 * ns), F32),
            pltpu.VMEM((rows, 2 * ns), F32),
            pltpu.VMEM((2, bsz, ns), F32),
        ],
        compiler_params=_cparams(2),
        name="ssm",
    )(u3, toep, w, v, al, dvec)


def _mix_kernel(tiles_per_seq, x_ref, up_ref, halo_ref, ys_ref, gate_ref, poolw_ref, pscale_ref,
                gluw_ref, glub_ref, wpu_ref, wsu_ref, wout_ref, gffn_ref, rwt_ref, rb_ref, tri_ref,
                x1_ref, h3_ref, idx_ref, gw_ref, rank_ref, cnt_ref, ext_s, carry_s):
    i = pl.program_id(0)
    j = i % tiles_per_seq
    tm = x_ref.shape[0]
    d_model = x_ref.shape[1]
    gdim = poolw_ref.shape[1]

    @pl.when(i == 0)
    def _():
        carry_s[...] = jnp.zeros_like(carry_s)

    ext_s[0:POOL_HALO, :] = jnp.where(j == 0, 0.0, halo_ref[...])
    ext_s[POOL_HALO:, :] = up_ref[...]
    pos = (j * tm + 1 + lax.broadcasted_iota(jnp.int32, (tm, 1), 0)).astype(F32)
    parts = []
    for g, w in enumerate(POOL_WINDOWS):
        cols = slice(g * gdim, (g + 1) * gdim)
        cur = ext_s[POOL_HALO:POOL_HALO + tm, cols]
        s = cur
        for k in range(1, w):
            s = s + ext_s[POOL_HALO - k:POOL_HALO - k + tm, cols]
        dlt = (s / jnp.minimum(pos, float(w)) - cur).astype(BF16)
        parts.append(jnp.dot(dlt, poolw_ref[g], preferred_element_type=F32))
    yp = jnp.concatenate(parts, axis=1) * pscale_ref[...]
    y_pool = jnp.dot(yp.astype(BF16), wpu_ref[...], preferred_element_type=F32)

    yg = ys_ref[...]
    glu = yg * jax.nn.sigmoid(
        jnp.dot(yg.astype(BF16), gluw_ref[...], preferred_element_type=F32) + glub_ref[...])
    y_ssm = jnp.dot(glu.astype(BF16), wsu_ref[...], preferred_element_type=F32)

    z = (gate_ref[:, :d_model].astype(F32) * y_pool + gate_ref[:, d_model:].astype(F32) * y_ssm)
    x1 = x_ref[...] + jnp.dot(z.astype(BF16), wout_ref[...], preferred_element_type=F32)
    x1_ref[...] = x1

    ms = jnp.mean(x1 * x1, axis=-1, keepdims=True)
    h2 = x1 * lax.rsqrt(ms + RMS_EPS) * gffn_ref[...]
    for s in range(d_model // LANES):
        h3_ref[:, s, :] = h2[:, s * LANES:(s + 1) * LANES]

    logits = lax.dot_general(rwt_ref[...], h2, (((1,), (1,)), ((), ())),
                             precision=lax.Precision.HIGHEST,
                             preferred_element_type=F32) + rb_ref[...]
    n_e = logits.shape[0]
    iota_e = lax.broadcasted_iota(jnp.int32, (n_e, tm), 0)
    l = logits
    tops, hots = [], []
    for k in range(TOP_K):
        m = jnp.max(l, axis=0, keepdims=True)
        idx = jnp.min(jnp.where(l == m, iota_e, n_e), axis=0, keepdims=True)
        hot = iota_e == idx
        l = jnp.where(hot, -jnp.inf, l)
        tops.append(m)
        hots.append(hot)
        idx_ref[k:k + 1, :] = idx
    exps = [jnp.exp(m - tops[0]) for m in tops]
    den = exps[0] + exps[1] + exps[2] + exps[3]
    for k in range(TOP_K):
        gw_ref[k:k + 1, :] = exps[k] / den

    multi = sum(h.astype(F32) for h in hots)
    cum = jnp.dot(multi.astype(BF16), tri_ref[...], preferred_element_type=F32) + carry_s[:, 0:1]
    for k in range(TOP_K):
        rk = jnp.sum(jnp.where(hots[k], cum, 0.0), axis=0, keepdims=True)
        rank_ref[k:k + 1, :] = rk.astype(jnp.int32)
    carry_s[...] = carry_s[...] + jnp.sum(multi, axis=1, keepdims=True)
    cnt_ref[...] = carry_s[...]


def _mix(x2, u_pool, y_ssm, gates, pool_w, pool_scale, glu_w, glu_b, w_pool_up, w_ssm_up, w_out,
         g_ffn, rw_t, rb, seq_len):
    m, d = x2.shape
    pw = u_pool.shape[1]
    sw = y_ssm.shape[1]
    tm = ROW_TILE
    n_e = rw_t.shape[0]
    tiles_per_seq = seq_len // tm
    tri = (jnp.arange(tm)[:, None] < jnp.arange(tm)[None, :]).astype(BF16)
    row = lambda i: (i, 0)
    const2 = lambda i: (0, 0)
    const3 = lambda i: (0, 0, 0)
    col = lambda i: (0, i)
    halo = lambda i: (jnp.maximum(i * (tm // POOL_HALO) - 1, 0), 0)
    return pl.pallas_call(
        functools.partial(_mix_kernel, tiles_per_seq),
        grid=(m // tm,),
        in_specs=[
            pl.BlockSpec((tm, d), row),
            pl.BlockSpec((tm, pw), row),
            pl.BlockSpec((POOL_HALO, pw), halo),
            pl.BlockSpec((tm, sw), row),
            pl.BlockSpec((tm, 2 * d), row),
            pl.BlockSpec(pool_w.shape, const3),
            pl.BlockSpec((1, pw), const2),
            pl.BlockSpec((sw, sw), const2),
            pl.BlockSpec((1, sw), const2),
            pl.BlockSpec((pw, d), const2),
            pl.BlockSpec((sw, d), const2),
            pl.BlockSpec((d, d), const2),
            pl.BlockSpec((1, d), const2),
            pl.BlockSpec((n_e, d), const2),
            pl.BlockSpec((n_e, 1), const2),
            pl.BlockSpec((tm, tm), const2),
        ],
        out_specs=[
            pl.BlockSpec((tm, d), row),
            pl.BlockSpec((tm, d // LANES, LANES), lambda i: (i, 0, 0)),
            pl.BlockSpec((TOP_K, tm), col),
            pl.BlockSpec((TOP_K, tm), col),
            pl.BlockSpec((TOP_K, tm), col),
            pl.BlockSpec((n_e, LANES), const2),
        ],
        out_shape=[
            jax.ShapeDtypeStruct((m, d), F32),
            jax.ShapeDtypeStruct((m, d // LANES, LANES), F32),
            jax.ShapeDtypeStruct((TOP_K, m), jnp.int32),
            jax.ShapeDtypeStruct((TOP_K, m), F32),
            jax.ShapeDtypeStruct((TOP_K, m), jnp.int32),
            jax.ShapeDtypeStruct((n_e, LANES), F32),
        ],
        scratch_shapes=[
            pltpu.VMEM((POOL_HALO + tm, pw), F32),
            pltpu.VMEM((n_e, LANES), F32),
        ],
        compiler_params=_cparams(1),
        name="mix_route",
    )(x2, u_pool, u_pool, y_ssm, gates, pool_w, pool_scale, glu_w, glu_b, w_pool_up, w_ssm_up,
      w_out, g_ffn, rw_t, rb, tri)


def _row_copy(src, dst, sem):
    return pltpu.make_async_copy(src, dst, sem)


def _dispatch_kernel(dest_ref, zpos_ref, h3_ref, xs_ref, zero_s, sem):
    i = pl.program_id(0)
    tm = h3_ref.shape[0]
    zrows = zero_s.shape[0]

    @pl.when(i == 0)
    def _():
        zero_s[...] = jnp.zeros_like(zero_s)

        def fill(start):
            cp = _row_copy(zero_s, xs_ref.at[pl.ds(start, zrows)], sem)
            cp.start()
            cp.wait()

        for e in range(N_EXPERTS):
            fill(zpos_ref[e])

        def tail(t, carry):
            fill(pl.multiple_of(t * zrows, zrows))
            return carry

        lax.fori_loop(zpos_ref[N_EXPERTS], xs_ref.shape[0] // zrows, tail, 0)

    def issue(r, carry):
        for k in range(TOP_K):
            _row_copy(h3_ref.at[r], xs_ref.at[dest_ref[k * tm + r]], sem).start()
        return carry

    lax.fori_loop(0, tm, issue, 0)
    for k in range(TOP_K):
        _row_copy(h3_ref, xs_ref.at[pl.ds(0, tm)], sem).wait()


def _dispatch(dest_tiles, zpos, h3, n_slots):
    m, s, lanes = h3.shape
    tm = ROW_TILE
    return pl.pallas_call(
        _dispatch_kernel,
        grid=(m // tm,),
        in_specs=[
            pl.BlockSpec((TOP_K * tm,), lambda i: (i,), memory_space=pltpu.SMEM),
            pl.BlockSpec(memory_space=pltpu.SMEM),
            pl.BlockSpec((tm, s, lanes), lambda i: (i, 0, 0)),
        ],
        out_specs=pl.BlockSpec(memory_space=pl.ANY),
        out_shape=jax.ShapeDtypeStruct((n_slots + MOE_ROWS, s, lanes), F32),
        scratch_shapes=[pltpu.VMEM((MOE_ROWS, s, lanes), F32), pltpu.SemaphoreType.DMA(())],
        compiler_params=_cparams(1),
        name="dispatch",
    )(dest_tiles, zpos, h3)


def _expert_kernel(blk_e_ref, first_ref, nused_ref, xs_ref, w1_ref, b1_ref, w2_ref, b2_ref, perm_ref,
                   ys_ref, w1p_s, w2b_s):
    i = pl.program_id(0)
    n_s = xs_ref.shape[1]
    f2 = w1_ref.shape[2]

    @pl.when(first_ref[i] == 1)
    def _():
        for c in range(f2 // DEINT):
            cols = slice(c * DEINT, (c + 1) * DEINT)
            strip = w1_ref[0, :, cols].astype(BF16)
            w1p_s[:, cols] = jnp.dot(strip, perm_ref[...], preferred_element_type=F32).astype(BF16)
        w2b_s[...] = w2_ref[0].astype(BF16)

    @pl.when(i < nused_ref[0])
    def _():
        x = jnp.concatenate([xs_ref[:, s, :] for s in range(n_s)], axis=1).astype(BF16)
        h = jnp.dot(x, w1p_s[...], preferred_element_type=F32) + b1_ref[0]
        acts = []
        for c in range(f2 // DEINT):
            xg = jnp.minimum(h[:, c * DEINT:c * DEINT + LANES], SWIGLU_LIMIT)
            xl = jnp.clip(h[:, c * DEINT + LANES:(c + 1) * DEINT], -SWIGLU_LIMIT, SWIGLU_LIMIT)
            acts.append((xg * jax.nn.sigmoid(SWIGLU_ALPHA * xg) * (xl + 1.0)).astype(BF16))
        act = jnp.concatenate(acts, axis=1)
        y = jnp.dot(act, w2b_s[...], preferred_element_type=F32) + b2_ref[0]
        for s in range(n_s):
            ys_ref[:, s, :] = y[:, s * LANES:(s + 1) * LANES]

    @pl.when(i >= nused_ref[0])
    def _():
        ys_ref[...] = jnp.zeros_like(ys_ref)


def _experts(blk_e, first, n_used, xs, w1, b1p, w2, b2, n_blocks):
    _, s, lanes = xs.shape
    _, d, f2 = w1.shape
    f = w2.shape[1]
    half = jnp.arange(DEINT) // 2 + (jnp.arange(DEINT) % 2) * LANES
    perm = (half[:, None] == jnp.arange(DEINT)[None, :]).astype(BF16)
    xmap = lambda i, be, fi, nu: (jnp.minimum(i, nu[0] - 1), 0, 0)
    emap = lambda i, be, fi, nu: (be[i], 0, 0)
    grid_spec = pltpu.PrefetchScalarGridSpec(
        num_scalar_prefetch=3,
        grid=(n_blocks,),
        in_specs=[
            pl.BlockSpec((MOE_ROWS, s, lanes), xmap),
            pl.BlockSpec((1, d, f2), emap),
            pl.BlockSpec((1, 1, f2), emap),
            pl.BlockSpec((1, f, d), emap),
            pl.BlockSpec((1, 1, d), emap),
            pl.BlockSpec((DEINT, DEINT), lambda i, be, fi, nu: (0, 0)),
        ],
        out_specs=pl.BlockSpec((MOE_ROWS, s, lanes), lambda i, be, fi, nu: (i, 0, 0)),
        scratch_shapes=[pltpu.VMEM((d, f2), BF16), pltpu.VMEM((f, d), BF16)],
    )
    return pl.pallas_call(
        _expert_kernel,
        grid_spec=grid_spec,
        out_shape=jax.ShapeDtypeStruct((n_blocks * MOE_ROWS, s, lanes), F32),
        compiler_params=_cparams(1),
        name="experts",
    )(blk_e, first, n_used, xs, w1, b1p, w2, b2, perm)


def _combine_kernel(dest_ref, x1_ref, gw_ref, gfin_ref, ys_ref, out_ref, buf_s, sem):
    tm = x1_ref.shape[0]
    n_s = buf_s.shape[1]

    def issue(r, carry):
        _row_copy(ys_ref.at[dest_ref[r]], buf_s.at[r], sem).start()
        return carry

    lax.fori_loop(0, TOP_K * tm, issue, 0)
    for k in range(TOP_K):
        _row_copy(ys_ref.at[pl.ds(0, tm)], buf_s.at[pl.ds(0, tm)], sem).wait()

    acc = x1_ref[...]
    for k in range(TOP_K):
        yk = jnp.concatenate([buf_s[k * tm:(k + 1) * tm, s, :] for s in range(n_s)], axis=1)
        acc = acc + gw_ref[:, k:k + 1] * yk
    ms = jnp.mean(acc * acc, axis=-1, keepdims=True)
    out_ref[...] = acc * lax.rsqrt(ms + RMS_EPS) * gfin_ref[...]


def _combine(dest_tiles, x1, gw_rows, g_final, ys):
    m, d = x1.shape
    _, s, lanes = ys.shape
    tm = ROW_TILE
    return pl.pallas_call(
        _combine_kernel,
        grid=(m // tm,),
        in_specs=[
            pl.BlockSpec((TOP_K * tm,), lambda i: (i,), memory_space=pltpu.SMEM),
            pl.BlockSpec((tm, d), lambda i: (i, 0)),
            pl.BlockSpec((tm, TOP_K), lambda i: (i, 0)),
            pl.BlockSpec((1, d), lambda i: (0, 0)),
            pl.BlockSpec(memory_space=pl.ANY),
        ],
        out_specs=pl.BlockSpec((tm, d), lambda i: (i, 0)),
        out_shape=jax.ShapeDtypeStruct((m, d), F32),
        scratch_shapes=[pltpu.VMEM((TOP_K * tm, s, lanes), F32), pltpu.SemaphoreType.DMA(())],
        compiler_params=_cparams(1),
        name="combine",
    )(dest_tiles, x1, gw_rows, g_final, ys)


def kernel(x, norm_mix_g, w_in, b_gate, pool_w, pool_scale, ssm_lambda_re, ssm_lambda_im, ssm_log_dt, ssm_b_re, ssm_b_im, ssm_c_re, ssm_c_im, ssm_d, ssm_glu_w, ssm_glu_b, w_pool_up, w_ssm_up, w_out, norm_ffn_g, router_w, router_b, moe_w1, moe_b1, moe_w2, moe_b2, norm_final_g):
    bsz, seq, d = x.shape
    depth = w_in.shape[0]
    pw = pool_w.shape[1] * pool_w.shape[2]
    n_groups = ssm_lambda_re.shape[1]
    sw = n_groups * SSM_GROUP_DIM
    m = bsz * seq
    assert depth == 1
    assert bsz == SUBLANES and seq % ROW_TILE == 0 and seq % SSM_TILE == 0 and d % LANES == 0
    assert pool_w.shape[1] == len(POOL_WINDOWS) and n_groups % SSM_PACK == 0
    assert moe_w1.shape[3] % DEINT == 0

    x2 = x.reshape(m, d)
    for l in range(depth):
        wi = w_in[l].astype(BF16)
        u_pool, u_ssm, gates = _inproj(
            x2, norm_mix_g[l][None], wi[:, :pw], wi[:, pw:pw + sw], wi[:, pw + sw:], b_gate[l][None])
        mats = _ssm_matrices(ssm_lambda_re[l], ssm_lambda_im[l], ssm_log_dt[l], ssm_b_re[l],
                             ssm_b_im[l], ssm_c_re[l], ssm_c_im[l], ssm_d[l])
        y_ssm = _ssm(u_ssm.reshape(bsz, seq, sw), mats).reshape(m, sw)
        x1, h3, idx_t, gw_t, rank_t, cnt = _mix(
            x2, u_pool, y_ssm, gates, pool_w[l].astype(BF16), pool_scale[l][None],
            ssm_glu_w[l].astype(BF16), ssm_glu_b[l][None], w_pool_up[l].astype(BF16),
            w_ssm_up[l].astype(BF16), w_out[l].astype(BF16), norm_ffn_g[l][None],
            router_w[l].T, router_b[l][:, None], seq)

        n_assign = m * TOP_K
        n_blocks = (n_assign + N_EXPERTS * (MOE_ROWS - 1) + MOE_ROWS - 1) // MOE_ROWS
        n_slots = n_blocks * MOE_ROWS
        counts = cnt[:, 0].astype(jnp.int32)
        padded = (counts + MOE_ROWS - 1) // MOE_ROWS * MOE_ROWS
        pad_end = jnp.cumsum(padded)
        pad_start = pad_end - padded
        n_used = (pad_end[-1] // MOE_ROWS).astype(jnp.int32)[None]
        blk_row = jnp.arange(n_blocks, dtype=jnp.int32) * MOE_ROWS
        blk_e = jnp.minimum(jnp.sum(pad_end[None, :] <= blk_row[:, None], axis=1),
                            N_EXPERTS - 1).astype(jnp.int32)
        first = jnp.concatenate([jnp.ones((1,), jnp.int32),
                                 (blk_e[1:] != blk_e[:-1]).astype(jnp.int32)])
        e_ids = jnp.arange(N_EXPERTS, dtype=jnp.int32)[:, None, None]
        dest = jnp.sum(jnp.where(idx_t[None] == e_ids, pad_start[:, None, None], 0), axis=0) + rank_t
        dest_tiles = (dest.reshape(TOP_K, m // ROW_TILE, ROW_TILE)
                      .transpose(1, 0, 2).reshape(-1).astype(jnp.int32))
        zpos = jnp.concatenate([pad_start + counts, n_used]).astype(jnp.int32)

        xs = _dispatch(dest_tiles, zpos, h3, n_slots)
        f2 = moe_w1.shape[3]
        b1p = (moe_b1[l].reshape(N_EXPERTS, f2 // DEINT, LANES, 2)
               .transpose(0, 1, 3, 2).reshape(N_EXPERTS, 1, f2))
        ys = _experts(blk_e, first, n_used, xs, moe_w1[l], b1p, moe_w2[l], moe_b2[l][:, None, :],
                      n_blocks)
        x2 = _combine(dest_tiles, x1, gw_t.T, norm_final_g[None], ys)
    return x2.reshape(bsz, seq, d)
```

```python
import functools
import math

import jax
import jax.numpy as jnp
from jax import lax
from jax.experimental import pallas as pl
from jax.experimental.pallas import tpu as pltpu

F32 = jnp.float32
BF16 = jnp.bfloat16

RMS_EPS = 1e-6
POOL_WINDOWS = (2, 4, 8, 16)
POOL_HALO = 16
SSM_GROUP_DIM = 16
SSM_STATE = 64
N_EXPERTS = 32
TOP_K = 4
SWIGLU_ALPHA = 1.702
SWIGLU_LIMIT = 7.0

LANES = 128
SUBLANES = 8
SSM_CHUNK = SUBLANES
SSM_PACK = LANES // SSM_GROUP_DIM
SSM_TILE = 256
DEINT = 2 * LANES
ROW_TILE = 256
MOE_ROWS = 512
VMEM_LIMIT = 56 * 1024 * 1024


def _cparams(n_axes):
    return pltpu.CompilerParams(
        dimension_semantics=("arbitrary",) * n_axes, vmem_limit_bytes=VMEM_LIMIT)


def _inproj_kernel(x_ref, g_ref, wp_ref, ws_ref, wg_ref, bg_ref, up_ref, us_ref, gate_ref):
    x = x_ref[...]
    ms = jnp.mean(x * x, axis=-1, keepdims=True)
    h = (x * lax.rsqrt(ms + RMS_EPS) * g_ref[...]).astype(BF16)
    up_ref[...] = jnp.dot(h, wp_ref[...], preferred_element_type=F32)
    us_ref[...] = jnp.dot(h, ws_ref[...], preferred_element_type=F32)
    gl = jnp.dot(h, wg_ref[...], preferred_element_type=F32) + bg_ref[...]
    gate_ref[...] = jax.nn.sigmoid(gl).astype(BF16)


def _inproj(x2, g, wp, ws, wg, bg):
    m, d = x2.shape
    pw, sw, gw = wp.shape[1], ws.shape[1], wg.shape[1]
    tm = ROW_TILE
    const = lambda i: (0, 0)
    row = lambda i: (i, 0)
    return pl.pallas_call(
        _inproj_kernel,
        grid=(m // tm,),
        in_specs=[
            pl.BlockSpec((tm, d), row),
            pl.BlockSpec((1, d), const),
            pl.BlockSpec((d, pw), const),
            pl.BlockSpec((d, sw), const),
            pl.BlockSpec((d, gw), const),
            pl.BlockSpec((1, gw), const),
        ],
        out_specs=[
            pl.BlockSpec((tm, pw), row),
            pl.BlockSpec((tm, sw), row),
            pl.BlockSpec((tm, gw), row),
        ],
        out_shape=[
            jax.ShapeDtypeStruct((m, pw), F32),
            jax.ShapeDtypeStruct((m, sw), F32),
            jax.ShapeDtypeStruct((m, gw), BF16),
        ],
        compiler_params=_cparams(1),
        name="inproj",
    )(x2, g, wp, ws, wg, bg)


def _ssm_matrices(lam_re, lam_im, log_dt, b_re, b_im, c_re, c_im, d_skip):
    hi = lax.Precision.HIGHEST
    L, P, N, GP = SSM_CHUNK, SSM_GROUP_DIM, SSM_STATE, SSM_PACK
    G = lam_re.shape[0]
    K = G // GP
    lr, li = lam_re.astype(F32), lam_im.astype(F32)
    dt = jnp.exp(log_dt.astype(F32))[:, None]
    mag = jnp.exp(lr * dt)
    lb_re, lb_im = mag * jnp.cos(li * dt), mag * jnp.sin(li * dt)
    den = lr * lr + li * li
    xr, xi = lb_re - 1.0, lb_im
    f_re = (xr * lr + xi * li) / den
    f_im = (xi * lr - xr * li) / den
    br, bi = b_re.astype(F32), b_im.astype(F32)
    bb_re = f_re[..., None] * br - f_im[..., None] * bi
    bb_im = f_re[..., None] * bi + f_im[..., None] * br
    tau = jnp.arange(L + 1, dtype=F32)[None, :, None]
    pmag = jnp.exp(lr[:, None, :] * dt[:, None, :] * tau)
    pang = li[:, None, :] * dt[:, None, :] * tau
    pw_re, pw_im = pmag * jnp.cos(pang), pmag * jnp.sin(pang)
    ab_re = pw_re[:, :L, :, None] * bb_re[:, None] - pw_im[:, :L, :, None] * bb_im[:, None]
    ab_im = pw_re[:, :L, :, None] * bb_im[:, None] + pw_im[:, :L, :, None] * bb_re[:, None]
    cr, ci = c_re.astype(F32), c_im.astype(F32)
    kern = (jnp.einsum('gqn,glnp->glqp', cr, ab_re, precision=hi)
            - jnp.einsum('gqn,glnp->glqp', ci, ab_im, precision=hi))
    lag = jnp.arange(L)[None, :] - jnp.arange(L)[:, None]
    sel = (lag[:, :, None] == jnp.arange(L)[None, None, :]).astype(F32)
    eye = jnp.eye(GP, dtype=F32)
    toep = jnp.einsum('jtl,kilqp,ih->kjipthq', sel, kern.reshape(K, GP, L, P, P), eye, precision=hi)
    toep = toep.reshape(K, L * GP * P, L * GP * P)
    w_re = jnp.einsum('kijnp,ih->kjiphn', ab_re[:, ::-1].reshape(K, GP, L, N, P), eye, precision=hi)
    w_im = jnp.einsum('kijnp,ih->kjiphn', ab_im[:, ::-1].reshape(K, GP, L, N, P), eye, precision=hi)
    w = jnp.concatenate([w_re.reshape(K, L * GP * P, GP * N), w_im.reshape(K, L * GP * P, GP * N)], axis=2)
    p1_re, p1_im = pw_re[:, 1:], pw_im[:, 1:]
    v_re = cr[:, None] * p1_re[:, :, None, :] - ci[:, None] * p1_im[:, :, None, :]
    v_im = -(cr[:, None] * p1_im[:, :, None, :] + ci[:, None] * p1_re[:, :, None, :])
    v_re = jnp.einsum('kitqn,ih->kinthq', v_re.reshape(K, GP, L, P, N), eye, precision=hi)
    v_im = jnp.einsum('kitqn,ih->kinthq', v_im.reshape(K, GP, L, P, N), eye, precision=hi)
    v = jnp.concatenate([v_re.reshape(K, GP * N, L * GP * P), v_im.reshape(K, GP * N, L * GP * P)], axis=1)
    al = jnp.stack([pw_re[:, L].reshape(K, GP * N), pw_im[:, L].reshape(K, GP * N)], axis=1)
    dvec = jnp.tile(d_skip.astype(F32).reshape(K, 1, GP * P), (1, L, 1)).reshape(K, 1, L * GP * P)
    return toep.astype(BF16), w.astype(BF16), v.astype(BF16), al, dvec


def _ssm_kernel(u_ref, toep_ref, w_ref, v_ref, al_ref, d_ref, y_ref, lhs_s, e_s, s_s, carry_s):
    bsz, tt, lanes = u_ref.shape
    L = SSM_CHUNK
    n_chunks = tt // L
    ns = al_ref.shape[2]

    @pl.when(pl.program_id(1) == 0)
    def _():
        carry_s[...] = jnp.zeros_like(carry_s)

    for b in range(bsz):
        for t in range(L):
            lhs_s[t, pl.ds(b, n_chunks, stride=bsz), :] = u_ref[b, pl.ds(t, n_chunks, stride=L), :]
    lhs = jnp.concatenate([lhs_s[t] for t in range(L)], axis=1)
    lb = lhs.astype(BF16)
    e_s[...] = jnp.dot(lb, w_ref[0], preferred_element_type=F32)
    ar = jnp.broadcast_to(al_ref[0, 0:1, :], (bsz, ns))
    ai = jnp.broadcast_to(al_ref[0, 1:2, :], (bsz, ns))
    sr = carry_s[0]
    si = carry_s[1]
    for c in range(n_chunks):
        rows = slice(c * bsz, (c + 1) * bsz)
        s_s[rows, :ns] = sr
        s_s[rows, ns:] = si
        er = e_s[rows, :ns]
        ei = e_s[rows, ns:]
        sr, si = ar * sr - ai * si + er, ar * si + ai * sr + ei
    carry_s[0] = sr
    carry_s[1] = si
    y = (jnp.dot(lb, toep_ref[0], preferred_element_type=F32)
         + jnp.dot(s_s[...].astype(BF16), v_ref[0], preferred_element_type=F32)
         + d_ref[0] * lhs)
    yg = jax.nn.gelu(y)
    for t in range(L):
        lhs_s[t] = yg[:, t * lanes:(t + 1) * lanes]
    for b in range(bsz):
        for t in range(L):
            y_ref[b, pl.ds(t, n_chunks, stride=L), :] = lhs_s[t, pl.ds(b, n_chunks, stride=bsz), :]


def _ssm(u3, mats):
    toep, w, v, al, dvec = mats
    bsz, seq, sw = u3.shape
    k = toep.shape[0]
    lanes = sw // k
    tt = SSM_TILE
    rows = bsz * tt // SSM_CHUNK
    cl = SSM_CHUNK * lanes
    ns = al.shape[2]
    pack = lambda p, t: (p, 0, 0)
    tile = lambda p, t: (0, t, p)
    return pl.pallas_call(
        _ssm_kernel,
        grid=(k, seq // tt),
        in_specs=[
            pl.BlockSpec((bsz, tt, lanes), tile),
            pl.BlockSpec((1, cl, cl), pack),
            pl.BlockSpec((1, cl, 2 * ns), pack),
            pl.BlockSpec((1, 2 * ns, cl), pack),
            pl.BlockSpec((1, 2, ns), pack),
            pl.BlockSpec((1, 1, cl), pack),
        ],
        out_specs=pl.BlockSpec((bsz, tt, lanes), tile),
        out_shape=jax.ShapeDtypeStruct((bsz, seq, sw), F32),
        scratch_shapes=[
            pltpu.VMEM((SSM_CHUNK, rows, lanes), F32),
            pltpu.VMEM((rows, 2 * ns), F32),
            pltpu.VMEM((rows, 2 * ns), F32),
            pltpu.VMEM((2, bsz, ns), F32),
        ],
        compiler_params=_cparams(2),
        name="ssm",
    )(u3, toep, w, v, al, dvec)


def _mix_kernel(tiles_per_seq, x_ref, up_ref, halo_ref, ys_ref, gate_ref, poolw_ref, pscale_ref,
                gluw_ref, glub_ref, wpu_ref, wsu_ref, wout_ref, gffn_ref, rwt_ref, rb_ref, tri_ref,
                x1_ref, h3_ref, idx_ref, gw_ref, rank_ref, cnt_ref, ext_s, carry_s):
    i = pl.program_id(0)
    j = i % tiles_per_seq
    tm = x_ref.shape[0]
    d_model = x_ref.shape[1]
    gdim = poolw_ref.shape[1]

    @pl.when(i == 0)
    def _():
        carry_s[...] = jnp.zeros_like(carry_s)

    ext_s[0:POOL_HALO, :] = jnp.where(j == 0, 0.0, halo_ref[...])
    ext_s[POOL_HALO:, :] = up_ref[...]
    pos = (j * tm + 1 + lax.broadcasted_iota(jnp.int32, (tm, 1), 0)).astype(F32)
    parts = []
    for g, w in enumerate(POOL_WINDOWS):
        cols = slice(g * gdim, (g + 1) * gdim)
        cur = ext_s[POOL_HALO:POOL_HALO + tm, cols]
        s = cur
        for k in range(1, w):
            s = s + ext_s[POOL_HALO - k:POOL_HALO - k + tm, cols]
        dlt = (s / jnp.minimum(pos, float(w)) - cur).astype(BF16)
        parts.append(jnp.dot(dlt, poolw_ref[g], preferred_element_type=F32))
    yp = jnp.concatenate(parts, axis=1) * pscale_ref[...]
    y_pool = jnp.dot(yp.astype(BF16), wpu_ref[...], preferred_element_type=F32)

    yg = ys_ref[...]
    glu = yg * jax.nn.sigmoid(
        jnp.dot(yg.astype(BF16), gluw_ref[...], preferred_element_type=F32) + glub_ref[...])
    y_ssm = jnp.dot(glu.astype(BF16), wsu_ref[...], preferred_element_type=F32)

    z = (gate_ref[:, :d_model].astype(F32) * y_pool + gate_ref[:, d_model:].astype(F32) * y_ssm)
    x1 = x_ref[...] + jnp.dot(z.astype(BF16), wout_ref[...], preferred_element_type=F32)
    x1_ref[...] = x1

    ms = jnp.mean(x1 * x1, axis=-1, keepdims=True)
    h2 = x1 * lax.rsqrt(ms + RMS_EPS) * gffn_ref[...]
    n_s = d_model // LANES
    for s in range(n_s):
        h3_ref[pl.ds(s, tm, stride=n_s), :] = h2[:, s * LANES:(s + 1) * LANES]

    logits = lax.dot_general(rwt_ref[...], h2, (((1,), (1,)), ((), ())),
                             precision=lax.Precision.HIGHEST,
                             preferred_element_type=F32) + rb_ref[...]
    n_e = logits.shape[0]
    iota_e = lax.broadcasted_iota(jnp.int32, (n_e, tm), 0)
    l = logits
    tops, hots = [], []
    for k in range(TOP_K):
        m = jnp.max(l, axis=0, keepdims=True)
        idx = jnp.min(jnp.where(l == m, iota_e, n_e), axis=0, keepdims=True)
        hot = iota_e == idx
        l = jnp.where(hot, -jnp.inf, l)
        tops.append(m)
        hots.append(hot)
        idx_ref[k:k + 1, :] = idx
    exps = [jnp.exp(m - tops[0]) for m in tops]
    den = exps[0] + exps[1] + exps[2] + exps[3]
    for k in range(TOP_K):
        gw_ref[k:k + 1, :] = exps[k] / den

    multi = sum(h.astype(F32) for h in hots)
    cum = jnp.dot(multi.astype(BF16), tri_ref[...], preferred_element_type=F32) + carry_s[:, 0:1]
    for k in range(TOP_K):
        rk = jnp.sum(jnp.where(hots[k], cum, 0.0), axis=0, keepdims=True)
        rank_ref[k:k + 1, :] = rk.astype(jnp.int32)
    carry_s[...] = carry_s[...] + jnp.sum(multi, axis=1, keepdims=True)
    cnt_ref[...] = carry_s[...]


def _mix(x2, u_pool, y_ssm, gates, pool_w, pool_scale, glu_w, glu_b, w_pool_up, w_ssm_up, w_out,
         g_ffn, rw_t, rb, seq_len):
    m, d = x2.shape
    pw = u_pool.shape[1]
    sw = y_ssm.shape[1]
    tm = ROW_TILE
    n_e = rw_t.shape[0]
    tiles_per_seq = seq_len // tm
    tri = (jnp.arange(tm)[:, None] < jnp.arange(tm)[None, :]).astype(BF16)
    row = lambda i: (i, 0)
    const2 = lambda i: (0, 0)
    const3 = lambda i: (0, 0, 0)
    col = lambda i: (0, i)
    halo = lambda i: (jnp.maximum(i * (tm // POOL_HALO) - 1, 0), 0)
    return pl.pallas_call(
        functools.partial(_mix_kernel, tiles_per_seq),
        grid=(m // tm,),
        in_specs=[
            pl.BlockSpec((tm, d), row),
            pl.BlockSpec((tm, pw), row),
            pl.BlockSpec((POOL_HALO, pw), halo),
            pl.BlockSpec((tm, sw), row),
            pl.BlockSpec((tm, 2 * d), row),
            pl.BlockSpec(pool_w.shape, const3),
            pl.BlockSpec((1, pw), const2),
            pl.BlockSpec((sw, sw), const2),
            pl.BlockSpec((1, sw), const2),
            pl.BlockSpec((pw, d), const2),
            pl.BlockSpec((sw, d), const2),
            pl.BlockSpec((d, d), const2),
            pl.BlockSpec((1, d), const2),
            pl.BlockSpec((n_e, d), const2),
            pl.BlockSpec((n_e, 1), const2),
            pl.BlockSpec((tm, tm), const2),
        ],
        out_specs=[
            pl.BlockSpec((tm, d), row),
            pl.BlockSpec((tm * (d // LANES), LANES), row),
            pl.BlockSpec((TOP_K, tm), col),
            pl.BlockSpec((TOP_K, tm), col),
            pl.BlockSpec((TOP_K, tm), col),
            pl.BlockSpec((n_e, LANES), const2),
        ],
        out_shape=[
            jax.ShapeDtypeStruct((m, d), F32),
            jax.ShapeDtypeStruct((m * (d // LANES), LANES), F32),
            jax.ShapeDtypeStruct((TOP_K, m), jnp.int32),
            jax.ShapeDtypeStruct((TOP_K, m), F32),
            jax.ShapeDtypeStruct((TOP_K, m), jnp.int32),
            jax.ShapeDtypeStruct((n_e, LANES), F32),
        ],
        scratch_shapes=[
            pltpu.VMEM((POOL_HALO + tm, pw), F32),
            pltpu.VMEM((n_e, LANES), F32),
        ],
        compiler_params=_cparams(1),
        name="mix_route",
    )(x2, u_pool, u_pool, y_ssm, gates, pool_w, pool_scale, glu_w, glu_b, w_pool_up, w_ssm_up,
      w_out, g_ffn, rw_t, rb, tri)


def _row_copy(src, dst, sem):
    return pltpu.make_async_copy(src, dst, sem)


def _token_rows(ref, tok, n_s):
    return ref.at[pl.ds(pl.multiple_of(tok * n_s, n_s), n_s)]


def _dispatch_kernel(n_s, dest_ref, zpos_ref, h3_ref, xs_ref, zero_s, sem):
    i = pl.program_id(0)
    tm = h3_ref.shape[0] // n_s
    zrows = zero_s.shape[0]

    @pl.when(i == 0)
    def _():
        zero_s[...] = jnp.zeros_like(zero_s)

        def fill(start):
            cp = _row_copy(zero_s, xs_ref.at[pl.ds(pl.multiple_of(start * n_s, n_s), zrows)], sem)
            cp.start()
            cp.wait()

        for e in range(N_EXPERTS):
            fill(zpos_ref[e])

        def tail(t, carry):
            fill(t * MOE_ROWS)
            return carry

        lax.fori_loop(zpos_ref[N_EXPERTS], xs_ref.shape[0] // zrows, tail, 0)

    def issue(r, carry):
        for k in range(TOP_K):
            _row_copy(_token_rows(h3_ref, r, n_s),
                      _token_rows(xs_ref, dest_ref[k * tm + r], n_s), sem).start(priority=k % 2)
        return carry

    lax.fori_loop(0, tm, issue, 0)
    for k in range(TOP_K):
        _row_copy(h3_ref, xs_ref.at[pl.ds(0, tm * n_s)], sem).wait()


def _dispatch(dest_tiles, zpos, h3, n_slots, n_s):
    rows, lanes = h3.shape
    tm = ROW_TILE
    return pl.pallas_call(
        functools.partial(_dispatch_kernel, n_s),
        grid=(rows // (tm * n_s),),
        in_specs=[
            pl.BlockSpec((TOP_K * tm,), lambda i: (i,), memory_space=pltpu.SMEM),
            pl.BlockSpec(memory_space=pltpu.SMEM),
            pl.BlockSpec((tm * n_s, lanes), lambda i: (i, 0)),
        ],
        out_specs=pl.BlockSpec(memory_space=pl.ANY),
        out_shape=jax.ShapeDtypeStruct(((n_slots + MOE_ROWS) * n_s, lanes), F32),
        scratch_shapes=[pltpu.VMEM((MOE_ROWS * n_s, lanes), F32), pltpu.SemaphoreType.DMA(())],
        compiler_params=_cparams(1),
        name="dispatch",
    )(dest_tiles, zpos, h3)


def _expert_kernel(blk_e_ref, first_ref, nused_ref, xs_ref, w1_ref, b1_ref, w2_ref, b2_ref, perm_ref,
                   ys_ref, w1p_s, w2b_s):
    i = pl.program_id(0)
    n_s = w1_ref.shape[1] // LANES
    rows = xs_ref.shape[0] // n_s
    f2 = w1_ref.shape[2]

    @pl.when(first_ref[i] == 1)
    def _():
        for c in range(f2 // DEINT):
            cols = slice(c * DEINT, (c + 1) * DEINT)
            strip = w1_ref[0, :, cols].astype(BF16)
            w1p_s[:, cols] = jnp.dot(strip, perm_ref[...], preferred_element_type=F32).astype(BF16)
        w2b_s[...] = w2_ref[0].astype(BF16)

    @pl.when(i < nused_ref[0])
    def _():
        x = jnp.concatenate([xs_ref[pl.ds(s, rows, stride=n_s), :] for s in range(n_s)],
                            axis=1).astype(BF16)
        h = jnp.dot(x, w1p_s[...], preferred_element_type=F32) + b1_ref[0]
        acts = []
        for c in range(f2 // DEINT):
            xg = jnp.minimum(h[:, c * DEINT:c * DEINT + LANES], SWIGLU_LIMIT)
            xl = jnp.clip(h[:, c * DEINT + LANES:(c + 1) * DEINT], -SWIGLU_LIMIT, SWIGLU_LIMIT)
            acts.append((xg * jax.nn.sigmoid(SWIGLU_ALPHA * xg) * (xl + 1.0)).astype(BF16))
        act = jnp.concatenate(acts, axis=1)
        y = jnp.dot(act, w2b_s[...], preferred_element_type=F32) + b2_ref[0]
        for s in range(n_s):
            ys_ref[pl.ds(s, rows, stride=n_s), :] = y[:, s * LANES:(s + 1) * LANES]

    @pl.when(i >= nused_ref[0])
    def _():
        ys_ref[...] = jnp.zeros_like(ys_ref)


def _experts(blk_e, first, n_used, xs, w1, b1p, w2, b2, n_blocks):
    _, lanes = xs.shape
    _, d, f2 = w1.shape
    blk = MOE_ROWS * (d // lanes)
    f = w2.shape[1]
    half = jnp.arange(DEINT) // 2 + (jnp.arange(DEINT) % 2) * LANES
    perm = (half[:, None] == jnp.arange(DEINT)[None, :]).astype(BF16)
    xmap = lambda i, be, fi, nu: (jnp.minimum(i, nu[0] - 1), 0)
    emap = lambda i, be, fi, nu: (be[i], 0, 0)
    grid_spec = pltpu.PrefetchScalarGridSpec(
        num_scalar_prefetch=3,
        grid=(n_blocks,),
        in_specs=[
            pl.BlockSpec((blk, lanes), xmap),
            pl.BlockSpec((1, d, f2), emap),
            pl.BlockSpec((1, 1, f2), emap),
            pl.BlockSpec((1, f, d), emap),
            pl.BlockSpec((1, 1, d), emap),
            pl.BlockSpec((DEINT, DEINT), lambda i, be, fi, nu: (0, 0)),
        ],
        out_specs=pl.BlockSpec((blk, lanes), lambda i, be, fi, nu: (i, 0)),
        scratch_shapes=[pltpu.VMEM((d, f2), BF16), pltpu.VMEM((f, d), BF16)],
    )
    return pl.pallas_call(
        _expert_kernel,
        grid_spec=grid_spec,
        out_shape=jax.ShapeDtypeStruct((n_blocks * blk, lanes), F32),
        compiler_params=_cparams(1),
        name="experts",
    )(blk_e, first, n_used, xs, w1, b1p, w2, b2, perm)


def _combine_kernel(dest_ref, x1_ref, gw_ref, gfin_ref, ys_ref, out_ref, buf_s, sem):
    tm, d = x1_ref.shape
    n_s = d // LANES

    def issue(r, carry):
        for k in range(TOP_K):
            _row_copy(_token_rows(ys_ref, dest_ref[k * tm + r], n_s),
                      _token_rows(buf_s, k * tm + r, n_s), sem).start(priority=k % 2)
        return carry

    lax.fori_loop(0, tm, issue, 0)
    for k in range(TOP_K):
        _row_copy(ys_ref.at[pl.ds(0, tm * n_s)], buf_s.at[pl.ds(0, tm * n_s)], sem).wait()

    acc = x1_ref[...]
    for k in range(TOP_K):
        yk = jnp.concatenate(
            [buf_s[pl.ds(k * tm * n_s + s, tm, stride=n_s), :] for s in range(n_s)], axis=1)
        acc = acc + gw_ref[:, k:k + 1] * yk
    ms = jnp.mean(acc * acc, axis=-1, keepdims=True)
    out_ref[...] = acc * lax.rsqrt(ms + RMS_EPS) * gfin_ref[...]


def _combine(dest_tiles, x1, gw_rows, g_final, ys):
    m, d = x1.shape
    _, lanes = ys.shape
    s = d // lanes
    tm = ROW_TILE
    return pl.pallas_call(
        _combine_kernel,
        grid=(m // tm,),
        in_specs=[
            pl.BlockSpec((TOP_K * tm,), lambda i: (i,), memory_space=pltpu.SMEM),
            pl.BlockSpec((tm, d), lambda i: (i, 0)),
            pl.BlockSpec((tm, TOP_K), lambda i: (i, 0)),
            pl.BlockSpec((1, d), lambda i: (0, 0)),
            pl.BlockSpec(memory_space=pl.ANY),
        ],
        out_specs=pl.BlockSpec((tm, d), lambda i: (i, 0)),
        out_shape=jax.ShapeDtypeStruct((m, d), F32),
        scratch_shapes=[pltpu.VMEM((TOP_K * tm * s, lanes), F32), pltpu.SemaphoreType.DMA(())],
        compiler_params=_cparams(1),
        name="combine",
    )(dest_tiles, x1, gw_rows, g_final, ys)


def kernel(x, norm_mix_g, w_in, b_gate, pool_w, pool_scale, ssm_lambda_re, ssm_lambda_im, ssm_log_dt, ssm_b_re, ssm_b_im, ssm_c_re, ssm_c_im, ssm_d, ssm_glu_w, ssm_glu_b, w_pool_up, w_ssm_up, w_out, norm_ffn_g, router_w, router_b, moe_w1, moe_b1, moe_w2, moe_b2, norm_final_g):
    bsz, seq, d = x.shape
    depth = w_in.shape[0]
    pw = pool_w.shape[1] * pool_w.shape[2]
    n_groups = ssm_lambda_re.shape[1]
    sw = n_groups * SSM_GROUP_DIM
    m = bsz * seq
    assert depth == 1
    assert bsz == SUBLANES and seq % ROW_TILE == 0 and seq % SSM_TILE == 0 and d % LANES == 0
    assert pool_w.shape[1] == len(POOL_WINDOWS) and n_groups % SSM_PACK == 0
    assert moe_w1.shape[3] % DEINT == 0

    x2 = x.reshape(m, d)
    for l in range(depth):
        wi = w_in[l].astype(BF16)
        u_pool, u_ssm, gates = _inproj(
            x2, norm_mix_g[l][None], wi[:, :pw], wi[:, pw:pw + sw], wi[:, pw + sw:], b_gate[l][None])
        mats = _ssm_matrices(ssm_lambda_re[l], ssm_lambda_im[l], ssm_log_dt[l], ssm_b_re[l],
                             ssm_b_im[l], ssm_c_re[l], ssm_c_im[l], ssm_d[l])
        y_ssm = _ssm(u_ssm.reshape(bsz, seq, sw), mats).reshape(m, sw)
        x1, h3, idx_t, gw_t, rank_t, cnt = _mix(
            x2, u_pool, y_ssm, gates, pool_w[l].astype(BF16), pool_scale[l][None],
            ssm_glu_w[l].astype(BF16), ssm_glu_b[l][None], w_pool_up[l].astype(BF16),
            w_ssm_up[l].astype(BF16), w_out[l].astype(BF16), norm_ffn_g[l][None],
            router_w[l].T, router_b[l][:, None], seq)

        n_assign = m * TOP_K
        n_blocks = (n_assign + N_EXPERTS * (MOE_ROWS - 1) + MOE_ROWS - 1) // MOE_ROWS
        n_slots = n_blocks * MOE_ROWS
        counts = cnt[:, 0].astype(jnp.int32)
        padded = (counts + MOE_ROWS - 1) // MOE_ROWS * MOE_ROWS
        pad_end = jnp.cumsum(padded)
        pad_start = pad_end - padded
        n_used = (pad_end[-1] // MOE_ROWS).astype(jnp.int32)[None]
        blk_row = jnp.arange(n_blocks, dtype=jnp.int32) * MOE_ROWS
        blk_e = jnp.minimum(jnp.sum(pad_end[None, :] <= blk_row[:, None], axis=1),
                            N_EXPERTS - 1).astype(jnp.int32)
        first = jnp.concatenate([jnp.ones((1,), jnp.int32),
                                 (blk_e[1:] != blk_e[:-1]).astype(jnp.int32)])
        e_ids = jnp.arange(N_EXPERTS, dtype=jnp.int32)[:, None, None]
        dest = jnp.sum(jnp.where(idx_t[None] == e_ids, pad_start[:, None, None], 0), axis=0) + rank_t
        dest_tiles = (dest.reshape(TOP_K, m // ROW_TILE, ROW_TILE)
                      .transpose(1, 0, 2).reshape(-1).astype(jnp.int32))
        zpos = jnp.concatenate([pad_start + counts, n_used]).astype(jnp.int32)

        xs = _dispatch(dest_tiles, zpos, h3, n_slots, d // LANES)
        f2 = moe_w1.shape[3]
        b1p = (moe_b1[l].reshape(N_EXPERTS, f2 // DEINT, LANES, 2)
               .transpose(0, 1, 3, 2).reshape(N_EXPERTS, 1, f2))
        ys = _experts(blk_e, first, n_used, xs, moe_w1[l], b1p, moe_w2[l], moe_b2[l][:, None, :],
                      n_blocks)
        x2 = _combine(dest_tiles, x1, gw_t.T, norm_final_g[None], ys)
    return x2.reshape(bsz, seq, d)
```

```python
import functools
import math

import jax
import jax.numpy as jnp
from jax import lax
from jax.experimental import pallas as pl
from jax.experimental.pallas import tpu as pltpu

F32 = jnp.float32
BF16 = jnp.bfloat16

RMS_EPS = 1e-6
POOL_WINDOWS = (2, 4, 8, 16)
POOL_HALO = 16
SSM_GROUP_DIM = 16
SSM_STATE = 64
N_EXPERTS = 32
TOP_K = 4
SWIGLU_ALPHA = 1.702
SWIGLU_LIMIT = 7.0

LANES = 128
SUBLANES = 8
SSM_CHUNK = SUBLANES
SSM_PACK = LANES // SSM_GROUP_DIM
SSM_TILE = 256
DEINT = 2 * LANES
ROW_TILE = 512
COMBINE_TILE = 256
MOE_ROWS = 512
VMEM_LIMIT = 56 * 1024 * 1024


def _cparams(n_axes):
    return pltpu.CompilerParams(
        dimension_semantics=("arbitrary",) * n_axes, vmem_limit_bytes=VMEM_LIMIT)


def _inproj_kernel(x_ref, g_ref, wp_ref, ws_ref, wg_ref, bg_ref, up_ref, us_ref, gate_ref):
    x = x_ref[...]
    ms = jnp.mean(x * x, axis=-1, keepdims=True)
    h = (x * lax.rsqrt(ms + RMS_EPS) * g_ref[...]).astype(BF16)
    up_ref[...] = jnp.dot(h, wp_ref[...], preferred_element_type=F32)
    us_ref[...] = jnp.dot(h, ws_ref[...], preferred_element_type=F32)
    gl = jnp.dot(h, wg_ref[...], preferred_element_type=F32) + bg_ref[...]
    gate_ref[...] = jax.nn.sigmoid(gl).astype(BF16)


def _inproj(x2, g, wp, ws, wg, bg):
    m, d = x2.shape
    pw, sw, gw = wp.shape[1], ws.shape[1], wg.shape[1]
    tm = ROW_TILE
    const = lambda i: (0, 0)
    row = lambda i: (i, 0)
    return pl.pallas_call(
        _inproj_kernel,
        grid=(m // tm,),
        in_specs=[
            pl.BlockSpec((tm, d), row),
            pl.BlockSpec((1, d), const),
            pl.BlockSpec((d, pw), const),
            pl.BlockSpec((d, sw), const),
            pl.BlockSpec((d, gw), const),
            pl.BlockSpec((1, gw), const),
        ],
        out_specs=[
            pl.BlockSpec((tm, pw), row),
            pl.BlockSpec((tm, sw), row),
            pl.BlockSpec((tm, gw), row),
        ],
        out_shape=[
            jax.ShapeDtypeStruct((m, pw), F32),
            jax.ShapeDtypeStruct((m, sw), F32),
            jax.ShapeDtypeStruct((m, gw), BF16),
        ],
        compiler_params=_cparams(1),
        name="inproj",
    )(x2, g, wp, ws, wg, bg)


def _ssm_matrices(lam_re, lam_im, log_dt, b_re, b_im, c_re, c_im, d_skip):
    hi = lax.Precision.HIGHEST
    L, P, N, GP = SSM_CHUNK, SSM_GROUP_DIM, SSM_STATE, SSM_PACK
    G = lam_re.shape[0]
    K = G // GP
    lr, li = lam_re.astype(F32), lam_im.astype(F32)
    dt = jnp.exp(log_dt.astype(F32))[:, None]
    mag = jnp.exp(lr * dt)
    lb_re, lb_im = mag * jnp.cos(li * dt), mag * jnp.sin(li * dt)
    den = lr * lr + li * li
    xr, xi = lb_re - 1.0, lb_im
    f_re = (xr * lr + xi * li) / den
    f_im = (xi * lr - xr * li) / den
    br, bi = b_re.astype(F32), b_im.astype(F32)
    bb_re = f_re[..., None] * br - f_im[..., None] * bi
    bb_im = f_re[..., None] * bi + f_im[..., None] * br
    tau = jnp.arange(L + 1, dtype=F32)[None, :, None]
    pmag = jnp.exp(lr[:, None, :] * dt[:, None, :] * tau)
    pang = li[:, None, :] * dt[:, None, :] * tau
    pw_re, pw_im = pmag * jnp.cos(pang), pmag * jnp.sin(pang)
    ab_re = pw_re[:, :L, :, None] * bb_re[:, None] - pw_im[:, :L, :, None] * bb_im[:, None]
    ab_im = pw_re[:, :L, :, None] * bb_im[:, None] + pw_im[:, :L, :, None] * bb_re[:, None]
    cr, ci = c_re.astype(F32), c_im.astype(F32)
    kern = (jnp.einsum('gqn,glnp->glqp', cr, ab_re, precision=hi)
            - jnp.einsum('gqn,glnp->glqp', ci, ab_im, precision=hi))
    eye = jnp.eye(GP, dtype=F32)
    kern_p = jnp.einsum('kilqp,ih->kiplhq', kern.reshape(K, GP, L, P, P), eye, precision=hi)
    kern_p = kern_p.reshape(K, GP * P, L, GP * P)
    kern_p = jnp.concatenate([jnp.zeros_like(kern_p), kern_p], axis=2)
    toep = jnp.stack([kern_p[:, :, L - j:2 * L - j, :].reshape(K, GP * P, L * GP * P)
                      for j in range(L)], axis=1)
    toep = toep.reshape(K, L * GP * P, L * GP * P)
    w_re = jnp.einsum('kijnp,ih->kjiphn', ab_re[:, ::-1].reshape(K, GP, L, N, P), eye, precision=hi)
    w_im = jnp.einsum('kijnp,ih->kjiphn', ab_im[:, ::-1].reshape(K, GP, L, N, P), eye, precision=hi)
    w = jnp.concatenate([w_re.reshape(K, L * GP * P, GP * N), w_im.reshape(K, L * GP * P, GP * N)], axis=2)
    p1_re, p1_im = pw_re[:, 1:], pw_im[:, 1:]
    v_re = cr[:, None] * p1_re[:, :, None, :] - ci[:, None] * p1_im[:, :, None, :]
    v_im = -(cr[:, None] * p1_im[:, :, None, :] + ci[:, None] * p1_re[:, :, None, :])
    v_re = jnp.einsum('kitqn,ih->kinthq', v_re.reshape(K, GP, L, P, N), eye, precision=hi)
    v_im = jnp.einsum('kitqn,ih->kinthq', v_im.reshape(K, GP, L, P, N), eye, precision=hi)
    v = jnp.concatenate([v_re.reshape(K, GP * N, L * GP * P), v_im.reshape(K, GP * N, L * GP * P)], axis=1)
    al = jnp.stack([pw_re[:, L].reshape(K, GP * N), pw_im[:, L].reshape(K, GP * N)], axis=1)
    dvec = jnp.tile(d_skip.astype(F32).reshape(K, 1, GP * P), (1, L, 1)).reshape(K, 1, L * GP * P)
    return toep.astype(BF16), w.astype(BF16), v.astype(BF16), al, dvec


def _ssm_kernel(u_ref, toep_ref, w_ref, v_ref, al_ref, d_ref, y_ref, lhs_s, e_s, s_s, carry_s):
    bsz, tt, lanes = u_ref.shape
    L = SSM_CHUNK
    n_chunks = tt // L
    ns = al_ref.shape[2]

    @pl.when(pl.program_id(1) == 0)
    def _():
        carry_s[...] = jnp.zeros_like(carry_s)

    for b in range(bsz):
        for t in range(L):
            lhs_s[t, pl.ds(b, n_chunks, stride=bsz), :] = u_ref[b, pl.ds(t, n_chunks, stride=L), :]
    lhs = jnp.concatenate([lhs_s[t] for t in range(L)], axis=1)
    lb = lhs.astype(BF16)
    e_s[...] = jnp.dot(lb, w_ref[0], preferred_element_type=F32)
    ar = jnp.broadcast_to(al_ref[0, 0:1, :], (bsz, ns))
    ai = jnp.broadcast_to(al_ref[0, 1:2, :], (bsz, ns))
    sr = carry_s[0]
    si = carry_s[1]
    for c in range(n_chunks):
        rows = slice(c * bsz, (c + 1) * bsz)
        s_s[rows, :ns] = sr
        s_s[rows, ns:] = si
        er = e_s[rows, :ns]
        ei = e_s[rows, ns:]
        sr, si = ar * sr - ai * si + er, ar * si + ai * sr + ei
    carry_s[0] = sr
    carry_s[1] = si
    y = (jnp.dot(lb, toep_ref[0], preferred_element_type=F32)
         + jnp.dot(s_s[...].astype(BF16), v_ref[0], preferred_element_type=F32)
         + d_ref[0] * lhs)
    yg = jax.nn.gelu(y)
    for t in range(L):
        lhs_s[t] = yg[:, t * lanes:(t + 1) * lanes]
    for b in range(bsz):
        for t in range(L):
            y_ref[b, pl.ds(t, n_chunks, stride=L), :] = lhs_s[t, pl.ds(b, n_chunks, stride=bsz), :]


def _ssm(u3, mats):
    toep, w, v, al, dvec = mats
    bsz, seq, sw = u3.shape
    k = toep.shape[0]
    lanes = sw // k
    tt = SSM_TILE
    rows = bsz * tt // SSM_CHUNK
    cl = SSM_CHUNK * lanes
    ns = al.shape[2]
    pack = lambda p, t: (p, 0, 0)
    tile = lambda p, t: (0, t, p)
    return pl.pallas_call(
        _ssm_kernel,
        grid=(k, seq // tt),
        in_specs=[
            pl.BlockSpec((bsz, tt, lanes), tile),
            pl.BlockSpec((1, cl, cl), pack),
            pl.BlockSpec((1, cl, 2 * ns), pack),
            pl.BlockSpec((1, 2 * ns, cl), pack),
            pl.BlockSpec((1, 2, ns), pack),
            pl.BlockSpec((1, 1, cl), pack),
        ],
        out_specs=pl.BlockSpec((bsz, tt, lanes), tile),
        out_shape=jax.ShapeDtypeStruct((bsz, seq, sw), F32),
        scratch_shapes=[
            pltpu.VMEM((SSM_CHUNK, rows, lanes), F32),
            pltpu.VMEM((rows, 2 * ns), F32),
            pltpu.VMEM((rows, 2 * ns), F32),
            pltpu.VMEM((2, bsz, ns), F32),
        ],
        compiler_params=_cparams(2),
        name="ssm",
    )(u3, toep, w, v, al, dvec)


def _mix_kernel(tiles_per_seq, x_ref, up_ref, halo_ref, ys_ref, gate_ref, poolw_ref, pscale_ref,
                gluw_ref, glub_ref, wpu_ref, wsu_ref, wout_ref, gffn_ref, rwt_ref, rb_ref, tri_ref,
                x1_ref, h3_ref, idx_ref, gw_ref, rank_ref, cnt_ref, ext_s, carry_s):
    i = pl.program_id(0)
    j = i % tiles_per_seq
    tm = x_ref.shape[0]
    d_model = x_ref.shape[1]
    gdim = poolw_ref.shape[1]

    @pl.when(i == 0)
    def _():
        carry_s[...] = jnp.zeros_like(carry_s)

    ext_s[0:POOL_HALO, :] = jnp.where(j == 0, 0.0, halo_ref[...])
    ext_s[POOL_HALO:, :] = up_ref[...]
    pos = (j * tm + 1 + lax.broadcasted_iota(jnp.int32, (tm, 1), 0)).astype(F32)
    parts = []
    for g, w in enumerate(POOL_WINDOWS):
        cols = slice(g * gdim, (g + 1) * gdim)
        cur = ext_s[POOL_HALO:POOL_HALO + tm, cols]
        s = cur
        for k in range(1, w):
            s = s + ext_s[POOL_HALO - k:POOL_HALO - k + tm, cols]
        dlt = (s / jnp.minimum(pos, float(w)) - cur).astype(BF16)
        parts.append(jnp.dot(dlt, poolw_ref[g], preferred_element_type=F32))
    yp = jnp.concatenate(parts, axis=1) * pscale_ref[...]
    y_pool = jnp.dot(yp.astype(BF16), wpu_ref[...], preferred_element_type=F32)

    yg = ys_ref[...]
    glu = yg * jax.nn.sigmoid(
        jnp.dot(yg.astype(BF16), gluw_ref[...], preferred_element_type=F32) + glub_ref[...])
    y_ssm = jnp.dot(glu.astype(BF16), wsu_ref[...], preferred_element_type=F32)

    z = (gate_ref[:, :d_model].astype(F32) * y_pool + gate_ref[:, d_model:].astype(F32) * y_ssm)
    x1 = x_ref[...] + jnp.dot(z.astype(BF16), wout_ref[...], preferred_element_type=F32)
    x1_ref[...] = x1

    ms = jnp.mean(x1 * x1, axis=-1, keepdims=True)
    h2 = x1 * lax.rsqrt(ms + RMS_EPS) * gffn_ref[...]
    n_s = d_model // LANES
    for s in range(n_s):
        h3_ref[pl.ds(s, tm, stride=n_s), :] = h2[:, s * LANES:(s + 1) * LANES]

    logits = lax.dot_general(rwt_ref[...], h2, (((1,), (1,)), ((), ())),
                             precision=lax.Precision.HIGHEST,
                             preferred_element_type=F32) + rb_ref[...]
    n_e = logits.shape[0]
    iota_e = lax.broadcasted_iota(jnp.int32, (n_e, tm), 0)
    l = logits
    tops, hots = [], []
    for k in range(TOP_K):
        m = jnp.max(l, axis=0, keepdims=True)
        idx = jnp.min(jnp.where(l == m, iota_e, n_e), axis=0, keepdims=True)
        hot = iota_e == idx
        l = jnp.where(hot, -jnp.inf, l)
        tops.append(m)
        hots.append(hot)
        idx_ref[k:k + 1, :] = idx
    exps = [jnp.exp(m - tops[0]) for m in tops]
    den = exps[0] + exps[1] + exps[2] + exps[3]
    for k in range(TOP_K):
        gw_ref[k:k + 1, :] = exps[k] / den

    multi = sum(h.astype(F32) for h in hots)
    cum = jnp.dot(multi.astype(BF16), tri_ref[...], preferred_element_type=F32) + carry_s[:, 0:1]
    for k in range(TOP_K):
        rk = jnp.sum(jnp.where(hots[k], cum, 0.0), axis=0, keepdims=True)
        rank_ref[k:k + 1, :] = rk.astype(jnp.int32)
    carry_s[...] = carry_s[...] + jnp.sum(multi, axis=1, keepdims=True)
    cnt_ref[...] = carry_s[...]


def _mix(x2, u_pool, y_ssm, gates, pool_w, pool_scale, glu_w, glu_b, w_pool_up, w_ssm_up, w_out,
         g_ffn, rw_t, rb, seq_len):
    m, d = x2.shape
    pw = u_pool.shape[1]
    sw = y_ssm.shape[1]
    tm = ROW_TILE
    n_e = rw_t.shape[0]
    tiles_per_seq = seq_len // tm
    tri = (jnp.arange(tm)[:, None] < jnp.arange(tm)[None, :]).astype(BF16)
    row = lambda i: (i, 0)
    const2 = lambda i: (0, 0)
    const3 = lambda i: (0, 0, 0)
    col = lambda i: (0, i)
    halo = lambda i: (jnp.maximum(i * (tm // POOL_HALO) - 1, 0), 0)
    return pl.pallas_call(
        functools.partial(_mix_kernel, tiles_per_seq),
        grid=(m // tm,),
        in_specs=[
            pl.BlockSpec((tm, d), row),
            pl.BlockSpec((tm, pw), row),
            pl.BlockSpec((POOL_HALO, pw), halo),
            pl.BlockSpec((tm, sw), row),
            pl.BlockSpec((tm, 2 * d), row),
            pl.BlockSpec(pool_w.shape, const3),
            pl.BlockSpec((1, pw), const2),
            pl.BlockSpec((sw, sw), const2),
            pl.BlockSpec((1, sw), const2),
            pl.BlockSpec((pw, d), const2),
            pl.BlockSpec((sw, d), const2),
            pl.BlockSpec((d, d), const2),
            pl.BlockSpec((1, d), const2),
            pl.BlockSpec((n_e, d), const2),
            pl.BlockSpec((n_e, 1), const2),
            pl.BlockSpec((tm, tm), const2),
        ],
        out_specs=[
            pl.BlockSpec((tm, d), row),
            pl.BlockSpec((tm * (d // LANES), LANES), row),
            pl.BlockSpec((TOP_K, tm), col),
            pl.BlockSpec((TOP_K, tm), col),
            pl.BlockSpec((TOP_K, tm), col),
            pl.BlockSpec((n_e, LANES), const2),
        ],
        out_shape=[
            jax.ShapeDtypeStruct((m, d), F32),
            jax.ShapeDtypeStruct((m * (d // LANES), LANES), F32),
            jax.ShapeDtypeStruct((TOP_K, m), jnp.int32),
            jax.ShapeDtypeStruct((TOP_K, m), F32),
            jax.ShapeDtypeStruct((TOP_K, m), jnp.int32),
            jax.ShapeDtypeStruct((n_e, LANES), F32),
        ],
        scratch_shapes=[
            pltpu.VMEM((POOL_HALO + tm, pw), F32),
            pltpu.VMEM((n_e, LANES), F32),
        ],
        compiler_params=_cparams(1),
        name="mix_route",
    )(x2, u_pool, u_pool, y_ssm, gates, pool_w, pool_scale, glu_w, glu_b, w_pool_up, w_ssm_up,
      w_out, g_ffn, rw_t, rb, tri)


def _row_copy(src, dst, sem):
    return pltpu.make_async_copy(src, dst, sem)


def _token_rows(ref, tok, n_s):
    return ref.at[pl.ds(pl.multiple_of(tok * n_s, n_s), n_s)]


def _dispatch_kernel(n_s, dest_ref, zpos_ref, h3_ref, xs_ref, zero_s, sem):
    i = pl.program_id(0)
    tm = h3_ref.shape[0] // n_s
    zrows = zero_s.shape[0]

    @pl.when(i == 0)
    def _():
        zero_s[...] = jnp.zeros_like(zero_s)

        def fill(start):
            cp = _row_copy(zero_s, xs_ref.at[pl.ds(pl.multiple_of(start * n_s, n_s), zrows)], sem)
            cp.start()
            cp.wait()

        for e in range(N_EXPERTS):
            fill(zpos_ref[e])

        def tail(t, carry):
            fill(t * MOE_ROWS)
            return carry

        lax.fori_loop(zpos_ref[N_EXPERTS], xs_ref.shape[0] // zrows, tail, 0)

    def issue(r, carry):
        for k in range(TOP_K):
            _row_copy(_token_rows(h3_ref, r, n_s),
                      _token_rows(xs_ref, dest_ref[k * tm + r], n_s), sem).start(priority=k % 2)
        return carry

    lax.fori_loop(0, tm, issue, 0)
    for k in range(TOP_K):
        _row_copy(h3_ref, xs_ref.at[pl.ds(0, tm * n_s)], sem).wait()


def _dispatch(dest_tiles, zpos, h3, n_slots, n_s):
    rows, lanes = h3.shape
    tm = ROW_TILE
    return pl.pallas_call(
        functools.partial(_dispatch_kernel, n_s),
        grid=(rows // (tm * n_s),),
        in_specs=[
            pl.BlockSpec((TOP_K * tm,), lambda i: (i,), memory_space=pltpu.SMEM),
            pl.BlockSpec(memory_space=pltpu.SMEM),
            pl.BlockSpec((tm * n_s, lanes), lambda i: (i, 0)),
        ],
        out_specs=pl.BlockSpec(memory_space=pl.ANY),
        out_shape=jax.ShapeDtypeStruct(((n_slots + MOE_ROWS) * n_s, lanes), F32),
        scratch_shapes=[pltpu.VMEM((MOE_ROWS * n_s, lanes), F32), pltpu.SemaphoreType.DMA(())],
        compiler_params=_cparams(1),
        name="dispatch",
    )(dest_tiles, zpos, h3)


def _expert_kernel(blk_e_ref, first_ref, nused_ref, xs_ref, w1_ref, b1_ref, w2_ref, b2_ref, perm_ref,
                   ys_ref, w1p_s, w2b_s):
    i = pl.program_id(0)
    n_s = w1_ref.shape[1] // LANES
    rows = xs_ref.shape[0] // n_s
    f2 = w1_ref.shape[2]

    @pl.when(first_ref[i] == 1)
    def _():
        for c in range(f2 // DEINT):
            cols = slice(c * DEINT, (c + 1) * DEINT)
            strip = w1_ref[0, :, cols].astype(BF16)
            w1p_s[:, cols] = jnp.dot(strip, perm_ref[...], preferred_element_type=F32).astype(BF16)
        w2b_s[...] = w2_ref[0].astype(BF16)

    @pl.when(i < nused_ref[0])
    def _():
        x = jnp.concatenate([xs_ref[pl.ds(s, rows, stride=n_s), :] for s in range(n_s)],
                            axis=1).astype(BF16)
        h = jnp.dot(x, w1p_s[...], preferred_element_type=F32) + b1_ref[0]
        acts = []
        for c in range(f2 // DEINT):
            xg = jnp.minimum(h[:, c * DEINT:c * DEINT + LANES], SWIGLU_LIMIT)
            xl = jnp.clip(h[:, c * DEINT + LANES:(c + 1) * DEINT], -SWIGLU_LIMIT, SWIGLU_LIMIT)
            acts.append((xg * jax.nn.sigmoid(SWIGLU_ALPHA * xg) * (xl + 1.0)).astype(BF16))
        act = jnp.concatenate(acts, axis=1)
        y = jnp.dot(act, w2b_s[...], preferred_element_type=F32) + b2_ref[0]
        for s in range(n_s):
            ys_ref[pl.ds(s, rows, stride=n_s), :] = y[:, s * LANES:(s + 1) * LANES]

    @pl.when(i >= nused_ref[0])
    def _():
        ys_ref[...] = jnp.zeros_like(ys_ref)


def _experts(blk_e, first, n_used, xs, w1, b1p, w2, b2, n_blocks):
    _, lanes = xs.shape
    _, d, f2 = w1.shape
    blk = MOE_ROWS * (d // lanes)
    f = w2.shape[1]
    half = jnp.arange(DEINT) // 2 + (jnp.arange(DEINT) % 2) * LANES
    perm = (half[:, None] == jnp.arange(DEINT)[None, :]).astype(BF16)
    xmap = lambda i, be, fi, nu: (jnp.minimum(i, nu[0] - 1), 0)
    emap = lambda i, be, fi, nu: (be[i], 0, 0)
    grid_spec = pltpu.PrefetchScalarGridSpec(
        num_scalar_prefetch=3,
        grid=(n_blocks,),
        in_specs=[
            pl.BlockSpec((blk, lanes), xmap),
            pl.BlockSpec((1, d, f2), emap),
            pl.BlockSpec((1, 1, f2), emap),
            pl.BlockSpec((1, f, d), emap),
            pl.BlockSpec((1, 1, d), emap),
            pl.BlockSpec((DEINT, DEINT), lambda i, be, fi, nu: (0, 0)),
        ],
        out_specs=pl.BlockSpec((blk, lanes), lambda i, be, fi, nu: (i, 0)),
        scratch_shapes=[pltpu.VMEM((d, f2), BF16), pltpu.VMEM((f, d), BF16)],
    )
    return pl.pallas_call(
        _expert_kernel,
        grid_spec=grid_spec,
        out_shape=jax.ShapeDtypeStruct((n_blocks * blk, lanes), F32),
        compiler_params=_cparams(1),
        name="experts",
    )(blk_e, first, n_used, xs, w1, b1p, w2, b2, perm)


def _combine_kernel(dest_ref, nxt_ref, x1_ref, gw_ref, gfin_ref, ys_ref, out_ref,
                    buf0_s, buf1_s, sem):
    j = pl.program_id(0)
    tm = x1_ref.shape[0] // 2
    n_s = x1_ref.shape[1] // LANES
    per_tile = TOP_K * tm

    def gather(idx_ref, base, buf, slot):
        def issue(r, carry):
            for k in range(TOP_K):
                _row_copy(_token_rows(ys_ref, idx_ref[base + k * tm + r], n_s),
                          _token_rows(buf, k * tm + r, n_s), sem.at[slot]).start(priority=k % 2)
            return carry

        lax.fori_loop(0, tm, issue, 0)

    def reduce(buf, slot, rows):
        for k in range(TOP_K):
            _row_copy(ys_ref.at[pl.ds(0, tm * n_s)], buf.at[pl.ds(0, tm * n_s)], sem.at[slot]).wait()
        acc = x1_ref[rows, :]
        for k in range(TOP_K):
            yk = jnp.concatenate(
                [buf[pl.ds(k * tm * n_s + s, tm, stride=n_s), :] for s in range(n_s)], axis=1)
            acc = acc + gw_ref[rows, k:k + 1] * yk
        ms = jnp.mean(acc * acc, axis=-1, keepdims=True)
        out_ref[rows, :] = acc * lax.rsqrt(ms + RMS_EPS) * gfin_ref[...]

    @pl.when(j == 0)
    def _():
        gather(dest_ref, 0, buf0_s, 0)

    gather(dest_ref, per_tile, buf1_s, 1)
    reduce(buf0_s, 0, slice(0, tm))

    @pl.when(j + 1 < pl.num_programs(0))
    def _():
        gather(nxt_ref, 0, buf0_s, 0)

    reduce(buf1_s, 1, slice(tm, 2 * tm))


def _combine(dest_tiles, x1, gw_rows, g_final, ys):
    m, d = x1.shape
    _, lanes = ys.shape
    s = d // lanes
    tm = COMBINE_TILE
    n_pairs = m // (2 * tm)
    return pl.pallas_call(
        _combine_kernel,
        grid=(n_pairs,),
        in_specs=[
            pl.BlockSpec((2 * TOP_K * tm,), lambda j: (j,), memory_space=pltpu.SMEM),
            pl.BlockSpec((TOP_K * tm,), lambda j: (jnp.minimum(2 * j + 2, 2 * n_pairs - 1),),
                         memory_space=pltpu.SMEM),
            pl.BlockSpec((2 * tm, d), lambda j: (j, 0)),
            pl.BlockSpec((2 * tm, TOP_K), lambda j: (j, 0)),
            pl.BlockSpec((1, d), lambda j: (0, 0)),
            pl.BlockSpec(memory_space=pl.ANY),
        ],
        out_specs=pl.BlockSpec((2 * tm, d), lambda j: (j, 0)),
        out_shape=jax.ShapeDtypeStruct((m, d), F32),
        scratch_shapes=[pltpu.VMEM((TOP_K * tm * s, lanes), F32),
                        pltpu.VMEM((TOP_K * tm * s, lanes), F32),
                        pltpu.SemaphoreType.DMA((2,))],
        compiler_params=_cparams(1),
        name="combine",
    )(dest_tiles, dest_tiles, x1, gw_rows, g_final, ys)


def kernel(x, norm_mix_g, w_in, b_gate, pool_w, pool_scale, ssm_lambda_re, ssm_lambda_im, ssm_log_dt, ssm_b_re, ssm_b_im, ssm_c_re, ssm_c_im, ssm_d, ssm_glu_w, ssm_glu_b, w_pool_up, w_ssm_up, w_out, norm_ffn_g, router_w, router_b, moe_w1, moe_b1, moe_w2, moe_b2, norm_final_g):
    bsz, seq, d = x.shape
    depth = w_in.shape[0]
    pw = pool_w.shape[1] * pool_w.shape[2]
    n_groups = ssm_lambda_re.shape[1]
    sw = n_groups * SSM_GROUP_DIM
    m = bsz * seq
    assert depth == 1
    assert bsz == SUBLANES and seq % ROW_TILE == 0 and seq % SSM_TILE == 0 and d % LANES == 0
    assert pool_w.shape[1] == len(POOL_WINDOWS) and n_groups % SSM_PACK == 0
    assert moe_w1.shape[3] % DEINT == 0

    x2 = x.reshape(m, d)
    for l in range(depth):
        wi = w_in[l].astype(BF16)
        u_pool, u_ssm, gates = _inproj(
            x2, norm_mix_g[l][None], wi[:, :pw], wi[:, pw:pw + sw], wi[:, pw + sw:], b_gate[l][None])
        mats = _ssm_matrices(ssm_lambda_re[l], ssm_lambda_im[l], ssm_log_dt[l], ssm_b_re[l],
                             ssm_b_im[l], ssm_c_re[l], ssm_c_im[l], ssm_d[l])
        y_ssm = _ssm(u_ssm.reshape(bsz, seq, sw), mats).reshape(m, sw)
        x1, h3, idx_t, gw_t, rank_t, cnt = _mix(
            x2, u_pool, y_ssm, gates, pool_w[l].astype(BF16), pool_scale[l][None],
            ssm_glu_w[l].astype(BF16), ssm_glu_b[l][None], w_pool_up[l].astype(BF16),
            w_ssm_up[l].astype(BF16), w_out[l].astype(BF16), norm_ffn_g[l][None],
            router_w[l].T, router_b[l][:, None], seq)

        n_assign = m * TOP_K
        n_blocks = (n_assign + N_EXPERTS * (MOE_ROWS - 1) + MOE_ROWS - 1) // MOE_ROWS
        n_slots = n_blocks * MOE_ROWS
        counts = cnt[:, 0].astype(jnp.int32)
        padded = (counts + MOE_ROWS - 1) // MOE_ROWS * MOE_ROWS
        pad_end = jnp.cumsum(padded)
        pad_start = pad_end - padded
        n_used = (pad_end[-1] // MOE_ROWS).astype(jnp.int32)[None]
        blk_row = jnp.arange(n_blocks, dtype=jnp.int32) * MOE_ROWS
        blk_e = jnp.minimum(jnp.sum(pad_end[None, :] <= blk_row[:, None], axis=1),
                            N_EXPERTS - 1).astype(jnp.int32)
        first = jnp.concatenate([jnp.ones((1,), jnp.int32),
                                 (blk_e[1:] != blk_e[:-1]).astype(jnp.int32)])
        e_ids = jnp.arange(N_EXPERTS, dtype=jnp.int32)[:, None, None]
        dest = jnp.sum(jnp.where(idx_t[None] == e_ids, pad_start[:, None, None], 0), axis=0) + rank_t
        by_tile = lambda t: (dest.reshape(TOP_K, m // t, t).transpose(1, 0, 2)
                             .reshape(-1).astype(jnp.int32))
        zpos = jnp.concatenate([pad_start + counts, n_used]).astype(jnp.int32)

        xs = _dispatch(by_tile(ROW_TILE), zpos, h3, n_slots, d // LANES)
        f2 = moe_w1.shape[3]
        b1p = (moe_b1[l].reshape(N_EXPERTS, f2 // DEINT, LANES, 2)
               .transpose(0, 1, 3, 2).reshape(N_EXPERTS, 1, f2))
        ys = _experts(blk_e, first, n_used, xs, moe_w1[l], b1p, moe_w2[l], moe_b2[l][:, None, :],
                      n_blocks)
        x2 = _combine(by_tile(COMBINE_TILE), x1, gw_t.T, norm_final_g[None], ys)
    return x2.reshape(bsz, seq, d)
```

```python
import functools
import math

import jax
import jax.numpy as jnp
from jax import lax
from jax.experimental import pallas as pl
from jax.experimental.pallas import tpu as pltpu

F32 = jnp.float32
BF16 = jnp.bfloat16

RMS_EPS = 1e-6
POOL_WINDOWS = (2, 4, 8, 16)
POOL_HALO = 16
SSM_GROUP_DIM = 16
SSM_STATE = 64
N_EXPERTS = 32
TOP_K = 4
SWIGLU_ALPHA = 1.702
SWIGLU_LIMIT = 7.0

LANES = 128
SUBLANES = 8
SSM_CHUNK = SUBLANES
SSM_PACK = LANES // SSM_GROUP_DIM
SSM_TILE = 256
DEINT = 2 * LANES
ROW_TILE = 512
COMBINE_TILE = 512
COMBINE_UNROLL = 4
RUN_TAB = 4 * N_EXPERTS
MOE_ROWS = 512
VMEM_LIMIT = 56 * 1024 * 1024


def _cparams(n_axes):
    return pltpu.CompilerParams(
        dimension_semantics=("arbitrary",) * n_axes, vmem_limit_bytes=VMEM_LIMIT)


def _inproj_kernel(x_ref, g_ref, wp_ref, ws_ref, wg_ref, bg_ref, up_ref, us_ref, gate_ref):
    x = x_ref[...]
    ms = jnp.mean(x * x, axis=-1, keepdims=True)
    h = (x * lax.rsqrt(ms + RMS_EPS) * g_ref[...]).astype(BF16)
    up_ref[...] = jnp.dot(h, wp_ref[...], preferred_element_type=F32)
    us_ref[...] = jnp.dot(h, ws_ref[...], preferred_element_type=F32)
    gl = jnp.dot(h, wg_ref[...], preferred_element_type=F32) + bg_ref[...]
    gate_ref[...] = jax.nn.sigmoid(gl).astype(BF16)


def _inproj(x2, g, wp, ws, wg, bg):
    m, d = x2.shape
    pw, sw, gw = wp.shape[1], ws.shape[1], wg.shape[1]
    tm = ROW_TILE
    const = lambda i: (0, 0)
    row = lambda i: (i, 0)
    return pl.pallas_call(
        _inproj_kernel,
        grid=(m // tm,),
        in_specs=[
            pl.BlockSpec((tm, d), row),
            pl.BlockSpec((1, d), const),
            pl.BlockSpec((d, pw), const),
            pl.BlockSpec((d, sw), const),
            pl.BlockSpec((d, gw), const),
            pl.BlockSpec((1, gw), const),
        ],
        out_specs=[
            pl.BlockSpec((tm, pw), row),
            pl.BlockSpec((tm, sw), row),
            pl.BlockSpec((tm, gw), row),
        ],
        out_shape=[
            jax.ShapeDtypeStruct((m, pw), F32),
            jax.ShapeDtypeStruct((m, sw), F32),
            jax.ShapeDtypeStruct((m, gw), BF16),
        ],
        compiler_params=_cparams(1),
        name="inproj",
    )(x2, g, wp, ws, wg, bg)


def _ssm_matrices(lam_re, lam_im, log_dt, b_re, b_im, c_re, c_im, d_skip):
    hi = lax.Precision.HIGHEST
    L, P, N, GP = SSM_CHUNK, SSM_GROUP_DIM, SSM_STATE, SSM_PACK
    G = lam_re.shape[0]
    K = G // GP
    lr, li = lam_re.astype(F32), lam_im.astype(F32)
    dt = jnp.exp(log_dt.astype(F32))[:, None]
    mag = jnp.exp(lr * dt)
    lb_re, lb_im = mag * jnp.cos(li * dt), mag * jnp.sin(li * dt)
    den = lr * lr + li * li
    xr, xi = lb_re - 1.0, lb_im
    f_re = (xr * lr + xi * li) / den
    f_im = (xi * lr - xr * li) / den
    br, bi = b_re.astype(F32), b_im.astype(F32)
    bb_re = f_re[..., None] * br - f_im[..., None] * bi
    bb_im = f_re[..., None] * bi + f_im[..., None] * br
    tau = jnp.arange(L + 1, dtype=F32)[None, :, None]
    pmag = jnp.exp(lr[:, None, :] * dt[:, None, :] * tau)
    pang = li[:, None, :] * dt[:, None, :] * tau
    pw_re, pw_im = pmag * jnp.cos(pang), pmag * jnp.sin(pang)
    ab_re = pw_re[:, :L, :, None] * bb_re[:, None] - pw_im[:, :L, :, None] * bb_im[:, None]
    ab_im = pw_re[:, :L, :, None] * bb_im[:, None] + pw_im[:, :L, :, None] * bb_re[:, None]
    cr, ci = c_re.astype(F32), c_im.astype(F32)
    kern = (jnp.einsum('gqn,glnp->glqp', cr, ab_re, precision=hi)
            - jnp.einsum('gqn,glnp->glqp', ci, ab_im, precision=hi))
    eye = jnp.eye(GP, dtype=F32)
    kern_p = jnp.einsum('kilqp,ih->kiplhq', kern.reshape(K, GP, L, P, P), eye, precision=hi)
    kern_p = kern_p.reshape(K, GP * P, L, GP * P)
    kern_p = jnp.concatenate([jnp.zeros_like(kern_p), kern_p], axis=2)
    toep = jnp.stack([kern_p[:, :, L - j:2 * L - j, :].reshape(K, GP * P, L * GP * P)
                      for j in range(L)], axis=1)
    toep = toep.reshape(K, L * GP * P, L * GP * P)
    w_re = jnp.einsum('kijnp,ih->kjiphn', ab_re[:, ::-1].reshape(K, GP, L, N, P), eye, precision=hi)
    w_im = jnp.einsum('kijnp,ih->kjiphn', ab_im[:, ::-1].reshape(K, GP, L, N, P), eye, precision=hi)
    w = jnp.concatenate([w_re.reshape(K, L * GP * P, GP * N), w_im.reshape(K, L * GP * P, GP * N)], axis=2)
    p1_re, p1_im = pw_re[:, 1:], pw_im[:, 1:]
    v_re = cr[:, None] * p1_re[:, :, None, :] - ci[:, None] * p1_im[:, :, None, :]
    v_im = -(cr[:, None] * p1_im[:, :, None, :] + ci[:, None] * p1_re[:, :, None, :])
    v_re = jnp.einsum('kitqn,ih->kinthq', v_re.reshape(K, GP, L, P, N), eye, precision=hi)
    v_im = jnp.einsum('kitqn,ih->kinthq', v_im.reshape(K, GP, L, P, N), eye, precision=hi)
    v = jnp.concatenate([v_re.reshape(K, GP * N, L * GP * P), v_im.reshape(K, GP * N, L * GP * P)], axis=1)
    al = jnp.stack([pw_re[:, L].reshape(K, GP * N), pw_im[:, L].reshape(K, GP * N)], axis=1)
    dvec = jnp.tile(d_skip.astype(F32).reshape(K, 1, GP * P), (1, L, 1)).reshape(K, 1, L * GP * P)
    return toep.astype(BF16), w.astype(BF16), v.astype(BF16), al, dvec


def _ssm_kernel(u_ref, toep_ref, w_ref, v_ref, al_ref, d_ref, y_ref, lhs_s, e_s, s_s, carry_s):
    bsz, tt, lanes = u_ref.shape
    L = SSM_CHUNK
    n_chunks = tt // L
    ns = al_ref.shape[2]

    @pl.when(pl.program_id(1) == 0)
    def _():
        carry_s[...] = jnp.zeros_like(carry_s)

    for b in range(bsz):
        for t in range(L):
            lhs_s[t, pl.ds(b, n_chunks, stride=bsz), :] = u_ref[b, pl.ds(t, n_chunks, stride=L), :]
    lhs = jnp.concatenate([lhs_s[t] for t in range(L)], axis=1)
    lb = lhs.astype(BF16)
    e_s[...] = jnp.dot(lb, w_ref[0], preferred_element_type=F32)
    ar = jnp.broadcast_to(al_ref[0, 0:1, :], (bsz, ns))
    ai = jnp.broadcast_to(al_ref[0, 1:2, :], (bsz, ns))
    sr = carry_s[0]
    si = carry_s[1]
    for c in range(n_chunks):
        rows = slice(c * bsz, (c + 1) * bsz)
        s_s[rows, :ns] = sr
        s_s[rows, ns:] = si
        er = e_s[rows, :ns]
        ei = e_s[rows, ns:]
        sr, si = ar * sr - ai * si + er, ar * si + ai * sr + ei
    carry_s[0] = sr
    carry_s[1] = si
    y = (jnp.dot(lb, toep_ref[0], preferred_element_type=F32)
         + jnp.dot(s_s[...].astype(BF16), v_ref[0], preferred_element_type=F32)
         + d_ref[0] * lhs)
    yg = jax.nn.gelu(y)
    for t in range(L):
        lhs_s[t] = yg[:, t * lanes:(t + 1) * lanes]
    for b in range(bsz):
        for t in range(L):
            y_ref[b, pl.ds(t, n_chunks, stride=L), :] = lhs_s[t, pl.ds(b, n_chunks, stride=bsz), :]


def _ssm(u3, mats):
    toep, w, v, al, dvec = mats
    bsz, seq, sw = u3.shape
    k = toep.shape[0]
    lanes = sw // k
    tt = SSM_TILE
    rows = bsz * tt // SSM_CHUNK
    cl = SSM_CHUNK * lanes
    ns = al.shape[2]
    pack = lambda p, t: (p, 0, 0)
    tile = lambda p, t: (0, t, p)
    return pl.pallas_call(
        _ssm_kernel,
        grid=(k, seq // tt),
        in_specs=[
            pl.BlockSpec((bsz, tt, lanes), tile),
            pl.BlockSpec((1, cl, cl), pack),
            pl.BlockSpec((1, cl, 2 * ns), pack),
            pl.BlockSpec((1, 2 * ns, cl), pack),
            pl.BlockSpec((1, 2, ns), pack),
            pl.BlockSpec((1, 1, cl), pack),
        ],
        out_specs=pl.BlockSpec((bsz, tt, lanes), tile),
        out_shape=jax.ShapeDtypeStruct((bsz, seq, sw), F32),
        scratch_shapes=[
            pltpu.VMEM((SSM_CHUNK, rows, lanes), F32),
            pltpu.VMEM((rows, 2 * ns), F32),
            pltpu.VMEM((rows, 2 * ns), F32),
            pltpu.VMEM((2, bsz, ns), F32),
        ],
        compiler_params=_cparams(2),
        name="ssm",
    )(u3, toep, w, v, al, dvec)


def _mix_kernel(tiles_per_seq, x_ref, up_ref, halo_ref, ys_ref, gate_ref, poolw_ref, pscale_ref,
                gluw_ref, glub_ref, wpu_ref, wsu_ref, wout_ref, gffn_ref, rwt_ref, rb_ref, tri_ref,
                x1_ref, h3_ref, idx_ref, gw_ref, rank_ref, cnt_ref, ext_s, carry_s):
    i = pl.program_id(0)
    j = i % tiles_per_seq
    tm = x_ref.shape[0]
    d_model = x_ref.shape[1]
    gdim = poolw_ref.shape[1]

    @pl.when(i == 0)
    def _():
        carry_s[...] = jnp.zeros_like(carry_s)

    ext_s[0:POOL_HALO, :] = jnp.where(j == 0, 0.0, halo_ref[...])
    ext_s[POOL_HALO:, :] = up_ref[...]
    pos = (j * tm + 1 + lax.broadcasted_iota(jnp.int32, (tm, 1), 0)).astype(F32)
    parts = []
    for g, w in enumerate(POOL_WINDOWS):
        cols = slice(g * gdim, (g + 1) * gdim)
        cur = ext_s[POOL_HALO:POOL_HALO + tm, cols]
        s = cur
        for k in range(1, w):
            s = s + ext_s[POOL_HALO - k:POOL_HALO - k + tm, cols]
        dlt = (s / jnp.minimum(pos, float(w)) - cur).astype(BF16)
        parts.append(jnp.dot(dlt, poolw_ref[g], preferred_element_type=F32))
    yp = jnp.concatenate(parts, axis=1) * pscale_ref[...]
    y_pool = jnp.dot(yp.astype(BF16), wpu_ref[...], preferred_element_type=F32)

    yg = ys_ref[...]
    glu = yg * jax.nn.sigmoid(
        jnp.dot(yg.astype(BF16), gluw_ref[...], preferred_element_type=F32) + glub_ref[...])
    y_ssm = jnp.dot(glu.astype(BF16), wsu_ref[...], preferred_element_type=F32)

    z = (gate_ref[:, :d_model].astype(F32) * y_pool + gate_ref[:, d_model:].astype(F32) * y_ssm)
    x1 = x_ref[...] + jnp.dot(z.astype(BF16), wout_ref[...], preferred_element_type=F32)
    x1_ref[...] = x1

    ms = jnp.mean(x1 * x1, axis=-1, keepdims=True)
    h2 = x1 * lax.rsqrt(ms + RMS_EPS) * gffn_ref[...]
    n_s = d_model // LANES
    for s in range(n_s):
        h3_ref[pl.ds(s, tm, stride=n_s), :] = h2[:, s * LANES:(s + 1) * LANES]

    logits = lax.dot_general(rwt_ref[...], h2, (((1,), (1,)), ((), ())),
                             precision=lax.Precision.HIGHEST,
                             preferred_element_type=F32) + rb_ref[...]
    n_e = logits.shape[0]
    iota_e = lax.broadcasted_iota(jnp.int32, (n_e, tm), 0)
    l = logits
    tops, hots = [], []
    for k in range(TOP_K):
        m = jnp.max(l, axis=0, keepdims=True)
        idx = jnp.min(jnp.where(l == m, iota_e, n_e), axis=0, keepdims=True)
        hot = iota_e == idx
        l = jnp.where(hot, -jnp.inf, l)
        tops.append(m)
        hots.append(hot)
        idx_ref[k:k + 1, :] = idx
    exps = [jnp.exp(m - tops[0]) for m in tops]
    den = exps[0] + exps[1] + exps[2] + exps[3]
    for k in range(TOP_K):
        gw_ref[k:k + 1, :] = exps[k] / den

    multi = sum(h.astype(F32) for h in hots)
    cum = jnp.dot(multi.astype(BF16), tri_ref[...], preferred_element_type=F32) + carry_s[:, 0:1]
    for k in range(TOP_K):
        rk = jnp.sum(jnp.where(hots[k], cum, 0.0), axis=0, keepdims=True)
        rank_ref[k:k + 1, :] = rk.astype(jnp.int32)
    carry_s[...] = carry_s[...] + jnp.sum(multi, axis=1, keepdims=True)
    cnt_ref[...] = carry_s[...]


def _mix(x2, u_pool, y_ssm, gates, pool_w, pool_scale, glu_w, glu_b, w_pool_up, w_ssm_up, w_out,
         g_ffn, rw_t, rb, seq_len):
    m, d = x2.shape
    pw = u_pool.shape[1]
    sw = y_ssm.shape[1]
    tm = ROW_TILE
    n_e = rw_t.shape[0]
    tiles_per_seq = seq_len // tm
    tri = (jnp.arange(tm)[:, None] < jnp.arange(tm)[None, :]).astype(BF16)
    row = lambda i: (i, 0)
    const2 = lambda i: (0, 0)
    const3 = lambda i: (0, 0, 0)
    col = lambda i: (0, i)
    halo = lambda i: (jnp.maximum(i * (tm // POOL_HALO) - 1, 0), 0)
    return pl.pallas_call(
        functools.partial(_mix_kernel, tiles_per_seq),
        grid=(m // tm,),
        in_specs=[
            pl.BlockSpec((tm, d), row),
            pl.BlockSpec((tm, pw), row),
            pl.BlockSpec((POOL_HALO, pw), halo),
            pl.BlockSpec((tm, sw), row),
            pl.BlockSpec((tm, 2 * d), row),
            pl.BlockSpec(pool_w.shape, const3),
            pl.BlockSpec((1, pw), const2),
            pl.BlockSpec((sw, sw), const2),
            pl.BlockSpec((1, sw), const2),
            pl.BlockSpec((pw, d), const2),
            pl.BlockSpec((sw, d), const2),
            pl.BlockSpec((d, d), const2),
            pl.BlockSpec((1, d), const2),
            pl.BlockSpec((n_e, d), const2),
            pl.BlockSpec((n_e, 1), const2),
            pl.BlockSpec((tm, tm), const2),
        ],
        out_specs=[
            pl.BlockSpec((tm, d), row),
            pl.BlockSpec((tm * (d // LANES), LANES), row),
            pl.BlockSpec((TOP_K, tm), col),
            pl.BlockSpec((TOP_K, tm), col),
            pl.BlockSpec((TOP_K, tm), col),
            pl.BlockSpec((n_e, LANES), const2),
        ],
        out_shape=[
            jax.ShapeDtypeStruct((m, d), F32),
            jax.ShapeDtypeStruct((m * (d // LANES), LANES), F32),
            jax.ShapeDtypeStruct((TOP_K, m), jnp.int32),
            jax.ShapeDtypeStruct((TOP_K, m), F32),
            jax.ShapeDtypeStruct((TOP_K, m), jnp.int32),
            jax.ShapeDtypeStruct((n_e, LANES), F32),
        ],
        scratch_shapes=[
            pltpu.VMEM((POOL_HALO + tm, pw), F32),
            pltpu.VMEM((n_e, LANES), F32),
        ],
        compiler_params=_cparams(1),
        name="mix_route",
    )(x2, u_pool, u_pool, y_ssm, gates, pool_w, pool_scale, glu_w, glu_b, w_pool_up, w_ssm_up,
      w_out, g_ffn, rw_t, rb, tri)


def _row_copy(src, dst, sem):
    return pltpu.make_async_copy(src, dst, sem)


def _token_rows(ref, tok, n_s):
    return ref.at[pl.ds(pl.multiple_of(tok * n_s, n_s), n_s)]


def _dispatch_kernel(n_s, dest_ref, zpos_ref, h3_ref, xs_ref, zero_s, sem):
    i = pl.program_id(0)
    tm = h3_ref.shape[0] // n_s
    zrows = zero_s.shape[0]

    @pl.when(i == 0)
    def _():
        zero_s[...] = jnp.zeros_like(zero_s)

        def fill(start):
            cp = _row_copy(zero_s, xs_ref.at[pl.ds(pl.multiple_of(start * n_s, n_s), zrows)], sem)
            cp.start()
            cp.wait()

        for e in range(N_EXPERTS):
            fill(zpos_ref[e])

        def tail(t, carry):
            fill(t * MOE_ROWS)
            return carry

        lax.fori_loop(zpos_ref[N_EXPERTS], xs_ref.shape[0] // zrows, tail, 0)

    def issue(r, carry):
        for k in range(TOP_K):
            _row_copy(_token_rows(h3_ref, r, n_s),
                      _token_rows(xs_ref, dest_ref[k * tm + r], n_s), sem).start(priority=k % 2)
        return carry

    lax.fori_loop(0, tm, issue, 0)
    for k in range(TOP_K):
        _row_copy(h3_ref, xs_ref.at[pl.ds(0, tm * n_s)], sem).wait()


def _dispatch(dest_tiles, zpos, h3, n_slots, n_s):
    rows, lanes = h3.shape
    tm = ROW_TILE
    return pl.pallas_call(
        functools.partial(_dispatch_kernel, n_s),
        grid=(rows // (tm * n_s),),
        in_specs=[
            pl.BlockSpec((TOP_K * tm,), lambda i: (i,), memory_space=pltpu.SMEM),
            pl.BlockSpec(memory_space=pltpu.SMEM),
            pl.BlockSpec((tm * n_s, lanes), lambda i: (i, 0)),
        ],
        out_specs=pl.BlockSpec(memory_space=pl.ANY),
        out_shape=jax.ShapeDtypeStruct(((n_slots + MOE_ROWS) * n_s, lanes), F32),
        scratch_shapes=[pltpu.VMEM((MOE_ROWS * n_s, lanes), F32), pltpu.SemaphoreType.DMA(())],
        compiler_params=_cparams(1),
        name="dispatch",
    )(dest_tiles, zpos, h3)


def _expert_kernel(blk_e_ref, first_ref, nused_ref, xs_ref, w1_ref, b1_ref, w2_ref, b2_ref, perm_ref,
                   ys_ref, w1p_s, w2b_s):
    i = pl.program_id(0)
    n_s = w1_ref.shape[1] // LANES
    rows = xs_ref.shape[0] // n_s
    f2 = w1_ref.shape[2]

    @pl.when(first_ref[i] == 1)
    def _():
        for c in range(f2 // DEINT):
            cols = slice(c * DEINT, (c + 1) * DEINT)
            strip = w1_ref[0, :, cols].astype(BF16)
            w1p_s[:, cols] = jnp.dot(strip, perm_ref[...], preferred_element_type=F32).astype(BF16)
        w2b_s[...] = w2_ref[0].astype(BF16)

    @pl.when(i < nused_ref[0])
    def _():
        x = jnp.concatenate([xs_ref[pl.ds(s, rows, stride=n_s), :] for s in range(n_s)],
                            axis=1).astype(BF16)
        h = jnp.dot(x, w1p_s[...], preferred_element_type=F32) + b1_ref[0]
        acts = []
        for c in range(f2 // DEINT):
            xg = jnp.minimum(h[:, c * DEINT:c * DEINT + LANES], SWIGLU_LIMIT)
            xl = jnp.clip(h[:, c * DEINT + LANES:(c + 1) * DEINT], -SWIGLU_LIMIT, SWIGLU_LIMIT)
            acts.append((xg * jax.nn.sigmoid(SWIGLU_ALPHA * xg) * (xl + 1.0)).astype(BF16))
        act = jnp.concatenate(acts, axis=1)
        y = jnp.dot(act, w2b_s[...], preferred_element_type=F32) + b2_ref[0]
        for s in range(n_s):
            ys_ref[pl.ds(s, rows, stride=n_s), :] = y[:, s * LANES:(s + 1) * LANES]

    @pl.when(i >= nused_ref[0])
    def _():
        ys_ref[...] = jnp.zeros_like(ys_ref)


def _experts(blk_e, first, n_used, xs, w1, b1p, w2, b2, n_blocks):
    _, lanes = xs.shape
    _, d, f2 = w1.shape
    blk = MOE_ROWS * (d // lanes)
    f = w2.shape[1]
    half = jnp.arange(DEINT) // 2 + (jnp.arange(DEINT) % 2) * LANES
    perm = (half[:, None] == jnp.arange(DEINT)[None, :]).astype(BF16)
    xmap = lambda i, be, fi, nu: (jnp.minimum(i, nu[0] - 1), 0)
    emap = lambda i, be, fi, nu: (be[i], 0, 0)
    grid_spec = pltpu.PrefetchScalarGridSpec(
        num_scalar_prefetch=3,
        grid=(n_blocks,),
        in_specs=[
            pl.BlockSpec((blk, lanes), xmap),
            pl.BlockSpec((1, d, f2), emap),
            pl.BlockSpec((1, 1, f2), emap),
            pl.BlockSpec((1, f, d), emap),
            pl.BlockSpec((1, 1, d), emap),
            pl.BlockSpec((DEINT, DEINT), lambda i, be, fi, nu: (0, 0)),
        ],
        out_specs=pl.BlockSpec((blk, lanes), lambda i, be, fi, nu: (i, 0)),
        scratch_shapes=[pltpu.VMEM((d, f2), BF16), pltpu.VMEM((f, d), BF16)],
    )
    return pl.pallas_call(
        _expert_kernel,
        grid_spec=grid_spec,
        out_shape=jax.ShapeDtypeStruct((n_blocks * blk, lanes), F32),
        compiler_params=_cparams(1),
        name="experts",
    )(blk_e, first, n_used, xs, w1, b1p, w2, b2, perm)


def _combine_kernel(tab_ref, nxt_ref, lpos_ref, gw_ref, x1_ref, gfin_ref, ys_ref, out_ref,
                    buf0_s, buf1_s, acc_s, sem):
    j = pl.program_id(0)
    tm = x1_ref.shape[0] // 2
    n_s = x1_ref.shape[1] // LANES
    per_tile = TOP_K * tm
    tab_w = RUN_TAB

    def gather(tab, tb, buf, slot):
        def per_expert(e, carry):
            n = tab[tb + e]
            off = tab[tb + N_EXPERTS + e]
            src = tab[tb + 2 * N_EXPERTS + e]
            for b in range(tm.bit_length()):
                size = 1 << b

                @pl.when(((n >> b) & 1) == 1)
                def _():
                    done = n & (size - 1)
                    _row_copy(ys_ref.at[pl.ds(pl.multiple_of((src + done) * n_s, n_s), size * n_s)],
                              buf.at[pl.ds(pl.multiple_of((off + done) * n_s, n_s), size * n_s)],
                              sem.at[slot]).start()
            return carry

        lax.fori_loop(0, N_EXPERTS, per_expert, 0)

    def reduce(buf, slot, t0):
        _row_copy(ys_ref.at[pl.ds(0, per_tile * n_s)], buf, sem.at[slot]).wait()
        base = (t0 // tm) * per_tile

        def tokens(g, carry):
            for u in range(COMBINE_UNROLL):
                t = g * COMBINE_UNROLL + u
                acc = None
                for k in range(TOP_K):
                    row = pl.multiple_of(lpos_ref[base + k * tm + t], n_s)
                    term = gw_ref[base + k * tm + t] * buf[pl.ds(row, n_s), :]
                    acc = term if acc is None else acc + term
                acc_s[pl.ds(pl.multiple_of(t * n_s, n_s), n_s), :] = acc
            return carry

        lax.fori_loop(0, tm // COMBINE_UNROLL, tokens, 0)
        y = jnp.concatenate([acc_s[pl.ds(s, tm, stride=n_s), :] for s in range(n_s)], axis=1)
        acc = x1_ref[t0:t0 + tm, :] + y
        ms = jnp.mean(acc * acc, axis=-1, keepdims=True)
        out_ref[t0:t0 + tm, :] = acc * lax.rsqrt(ms + RMS_EPS) * gfin_ref[...]

    @pl.when(j == 0)
    def _():
        gather(tab_ref, 0, buf0_s, 0)

    gather(tab_ref, tab_w, buf1_s, 1)
    reduce(buf0_s, 0, 0)

    @pl.when(j + 1 < pl.num_programs(0))
    def _():
        gather(nxt_ref, 0, buf0_s, 0)

    reduce(buf1_s, 1, tm)


def _combine(tab, lpos_tiles, gw_tiles, x1, g_final, ys):
    m, d = x1.shape
    _, lanes = ys.shape
    s = d // lanes
    tm = COMBINE_TILE
    n_pairs = m // (2 * tm)
    tab_w = RUN_TAB
    smem = pltpu.SMEM
    return pl.pallas_call(
        _combine_kernel,
        grid=(n_pairs,),
        in_specs=[
            pl.BlockSpec((2 * tab_w,), lambda j: (j,), memory_space=smem),
            pl.BlockSpec((tab_w,), lambda j: (jnp.minimum(2 * j + 2, 2 * n_pairs - 1),),
                         memory_space=smem),
            pl.BlockSpec((2 * TOP_K * tm,), lambda j: (j,), memory_space=smem),
            pl.BlockSpec((2 * TOP_K * tm,), lambda j: (j,), memory_space=smem),
            pl.BlockSpec((2 * tm, d), lambda j: (j, 0)),
            pl.BlockSpec((1, d), lambda j: (0, 0)),
            pl.BlockSpec(memory_space=pl.ANY),
        ],
        out_specs=pl.BlockSpec((2 * tm, d), lambda j: (j, 0)),
        out_shape=jax.ShapeDtypeStruct((m, d), F32),
        scratch_shapes=[pltpu.VMEM((TOP_K * tm * s, lanes), F32),
                        pltpu.VMEM((TOP_K * tm * s, lanes), F32),
                        pltpu.VMEM((tm * s, lanes), F32),
                        pltpu.SemaphoreType.DMA((2,))],
        compiler_params=_cparams(1),
        name="combine",
    )(tab, tab, lpos_tiles, gw_tiles, x1, g_final, ys)


def kernel(x, norm_mix_g, w_in, b_gate, pool_w, pool_scale, ssm_lambda_re, ssm_lambda_im, ssm_log_dt, ssm_b_re, ssm_b_im, ssm_c_re, ssm_c_im, ssm_d, ssm_glu_w, ssm_glu_b, w_pool_up, w_ssm_up, w_out, norm_ffn_g, router_w, router_b, moe_w1, moe_b1, moe_w2, moe_b2, norm_final_g):
    bsz, seq, d = x.shape
    depth = w_in.shape[0]
    pw = pool_w.shape[1] * pool_w.shape[2]
    n_groups = ssm_lambda_re.shape[1]
    sw = n_groups * SSM_GROUP_DIM
    m = bsz * seq
    assert depth == 1
    assert bsz == SUBLANES and seq % ROW_TILE == 0 and seq % SSM_TILE == 0 and d % LANES == 0
    assert pool_w.shape[1] == len(POOL_WINDOWS) and n_groups % SSM_PACK == 0
    assert moe_w1.shape[3] % DEINT == 0

    x2 = x.reshape(m, d)
    for l in range(depth):
        wi = w_in[l].astype(BF16)
        u_pool, u_ssm, gates = _inproj(
            x2, norm_mix_g[l][None], wi[:, :pw], wi[:, pw:pw + sw], wi[:, pw + sw:], b_gate[l][None])
        mats = _ssm_matrices(ssm_lambda_re[l], ssm_lambda_im[l], ssm_log_dt[l], ssm_b_re[l],
                             ssm_b_im[l], ssm_c_re[l], ssm_c_im[l], ssm_d[l])
        y_ssm = _ssm(u_ssm.reshape(bsz, seq, sw), mats).reshape(m, sw)
        x1, h3, idx_t, gw_t, rank_t, cnt = _mix(
            x2, u_pool, y_ssm, gates, pool_w[l].astype(BF16), pool_scale[l][None],
            ssm_glu_w[l].astype(BF16), ssm_glu_b[l][None], w_pool_up[l].astype(BF16),
            w_ssm_up[l].astype(BF16), w_out[l].astype(BF16), norm_ffn_g[l][None],
            router_w[l].T, router_b[l][:, None], seq)

        n_assign = m * TOP_K
        n_blocks = (n_assign + N_EXPERTS * (MOE_ROWS - 1) + MOE_ROWS - 1) // MOE_ROWS
        n_slots = n_blocks * MOE_ROWS
        counts = cnt[:, 0].astype(jnp.int32)
        padded = (counts + MOE_ROWS - 1) // MOE_ROWS * MOE_ROWS
        pad_end = jnp.cumsum(padded)
        pad_start = pad_end - padded
        n_used = (pad_end[-1] // MOE_ROWS).astype(jnp.int32)[None]
        blk_row = jnp.arange(n_blocks, dtype=jnp.int32) * MOE_ROWS
        blk_e = jnp.minimum(jnp.sum(pad_end[None, :] <= blk_row[:, None], axis=1),
                            N_EXPERTS - 1).astype(jnp.int32)
        first = jnp.concatenate([jnp.ones((1,), jnp.int32),
                                 (blk_e[1:] != blk_e[:-1]).astype(jnp.int32)])
        e_ids = jnp.arange(N_EXPERTS, dtype=jnp.int32)[:, None, None]
        dest = jnp.sum(jnp.where(idx_t[None] == e_ids, pad_start[:, None, None], 0), axis=0) + rank_t
        by_tile = lambda t: (dest.reshape(TOP_K, m // t, t).transpose(1, 0, 2)
                             .reshape(-1).astype(jnp.int32))
        zpos = jnp.concatenate([pad_start + counts, n_used]).astype(jnp.int32)

        xs = _dispatch(by_tile(ROW_TILE), zpos, h3, n_slots, d // LANES)
        f2 = moe_w1.shape[3]
        b1p = (moe_b1[l].reshape(N_EXPERTS, f2 // DEINT, LANES, 2)
               .transpose(0, 1, 3, 2).reshape(N_EXPERTS, 1, f2))
        ys = _experts(blk_e, first, n_used, xs, moe_w1[l], b1p, moe_w2[l], moe_b2[l][:, None, :],
                      n_blocks)
        ct = COMBINE_TILE
        n_ct = m // ct
        cnt_te = jnp.sum((idx_t.reshape(TOP_K, n_ct, ct)[None] == e_ids[..., None]).astype(jnp.int32),
                         axis=(1, 3))
        carry_te = jnp.cumsum(cnt_te, axis=1) - cnt_te
        offs_te = jnp.cumsum(cnt_te, axis=0) - cnt_te
        tab = jnp.stack([cnt_te, offs_te, pad_start[:, None] + carry_te, jnp.zeros_like(cnt_te)],
                        axis=0)
        tab = tab.transpose(2, 0, 1).reshape(-1).astype(jnp.int32)
        delta = jnp.repeat(offs_te - carry_te, ct, axis=1)
        lpos = rank_t + jnp.sum(jnp.where(idx_t[None] == e_ids, delta[:, None, :], 0), axis=0)
        tile_major = lambda a: a.reshape(TOP_K, n_ct, ct).transpose(1, 0, 2).reshape(-1)
        x2 = _combine(tab, tile_major((lpos * (d // LANES)).astype(jnp.int32)), tile_major(gw_t), x1,
                      norm_final_g[None], ys)
    return x2.reshape(bsz, seq, d)
```

```python
import functools
import math

import jax
import jax.numpy as jnp
from jax import lax
from jax.experimental import pallas as pl
from jax.experimental.pallas import tpu as pltpu

F32 = jnp.float32
BF16 = jnp.bfloat16

RMS_EPS = 1e-6
POOL_WINDOWS = (2, 4, 8, 16)
POOL_HALO = 16
SSM_GROUP_DIM = 16
SSM_STATE = 64
N_EXPERTS = 32
TOP_K = 4
SWIGLU_ALPHA = 1.702
SWIGLU_LIMIT = 7.0

LANES = 128
SUBLANES = 8
SSM_CHUNK = SUBLANES
SSM_PACK = LANES // SSM_GROUP_DIM
SSM_TILE = 256
DEINT = 2 * LANES
ROW_TILE = 512
COMBINE_TILE = 512
COMBINE_UNROLL = 4
RUN_TAB = 4 * N_EXPERTS
MOE_ROWS = 512
VMEM_LIMIT = 56 * 1024 * 1024


def _cparams(n_axes):
    return pltpu.CompilerParams(
        dimension_semantics=("arbitrary",) * n_axes, vmem_limit_bytes=VMEM_LIMIT)


def _inproj_kernel(x_ref, g_ref, wp_ref, ws_ref, wg_ref, bg_ref, up_ref, us_ref, gate_ref):
    x = x_ref[...]
    ms = jnp.mean(x * x, axis=-1, keepdims=True)
    h = (x * lax.rsqrt(ms + RMS_EPS) * g_ref[...]).astype(BF16)
    up_ref[...] = jnp.dot(h, wp_ref[...], preferred_element_type=F32)
    us_ref[...] = jnp.dot(h, ws_ref[...], preferred_element_type=F32)
    gl = jnp.dot(h, wg_ref[...], preferred_element_type=F32) + bg_ref[...]
    gate_ref[...] = jax.nn.sigmoid(gl).astype(BF16)


def _inproj(x2, g, wp, ws, wg, bg):
    m, d = x2.shape
    pw, sw, gw = wp.shape[1], ws.shape[1], wg.shape[1]
    tm = ROW_TILE
    const = lambda i: (0, 0)
    row = lambda i: (i, 0)
    return pl.pallas_call(
        _inproj_kernel,
        grid=(m // tm,),
        in_specs=[
            pl.BlockSpec((tm, d), row),
            pl.BlockSpec((1, d), const),
            pl.BlockSpec((d, pw), const),
            pl.BlockSpec((d, sw), const),
            pl.BlockSpec((d, gw), const),
            pl.BlockSpec((1, gw), const),
        ],
        out_specs=[
            pl.BlockSpec((tm, pw), row),
            pl.BlockSpec((tm, sw), row),
            pl.BlockSpec((tm, gw), row),
        ],
        out_shape=[
            jax.ShapeDtypeStruct((m, pw), F32),
            jax.ShapeDtypeStruct((m, sw), F32),
            jax.ShapeDtypeStruct((m, gw), BF16),
        ],
        compiler_params=_cparams(1),
        name="inproj",
    )(x2, g, wp, ws, wg, bg)


def _ssm_matrices(lam_re, lam_im, log_dt, b_re, b_im, c_re, c_im, d_skip):
    hi = lax.Precision.HIGHEST
    L, P, N, GP = SSM_CHUNK, SSM_GROUP_DIM, SSM_STATE, SSM_PACK
    G = lam_re.shape[0]
    K = G // GP
    lr, li = lam_re.astype(F32), lam_im.astype(F32)
    dt = jnp.exp(log_dt.astype(F32))[:, None]
    mag = jnp.exp(lr * dt)
    lb_re, lb_im = mag * jnp.cos(li * dt), mag * jnp.sin(li * dt)
    den = lr * lr + li * li
    xr, xi = lb_re - 1.0, lb_im
    f_re = (xr * lr + xi * li) / den
    f_im = (xi * lr - xr * li) / den
    br, bi = b_re.astype(F32), b_im.astype(F32)
    bb_re = f_re[..., None] * br - f_im[..., None] * bi
    bb_im = f_re[..., None] * bi + f_im[..., None] * br
    tau = jnp.arange(L + 1, dtype=F32)[None, :, None]
    pmag = jnp.exp(lr[:, None, :] * dt[:, None, :] * tau)
    pang = li[:, None, :] * dt[:, None, :] * tau
    pw_re, pw_im = pmag * jnp.cos(pang), pmag * jnp.sin(pang)
    ab_re = pw_re[:, :L, :, None] * bb_re[:, None] - pw_im[:, :L, :, None] * bb_im[:, None]
    ab_im = pw_re[:, :L, :, None] * bb_im[:, None] + pw_im[:, :L, :, None] * bb_re[:, None]
    cr, ci = c_re.astype(F32), c_im.astype(F32)
    kern = (jnp.einsum('gqn,glnp->glqp', cr, ab_re, precision=hi)
            - jnp.einsum('gqn,glnp->glqp', ci, ab_im, precision=hi))
    eye = jnp.eye(GP, dtype=F32)
    kern_p = jnp.einsum('kilqp,ih->kiplhq', kern.reshape(K, GP, L, P, P), eye, precision=hi)
    kern_p = kern_p.reshape(K, GP * P, L, GP * P)
    kern_p = jnp.concatenate([jnp.zeros_like(kern_p), kern_p], axis=2)
    toep = jnp.stack([kern_p[:, :, L - j:2 * L - j, :].reshape(K, GP * P, L * GP * P)
                      for j in range(L)], axis=1)
    toep = toep.reshape(K, L * GP * P, L * GP * P)
    w_re = jnp.einsum('kijnp,ih->kjiphn', ab_re[:, ::-1].reshape(K, GP, L, N, P), eye, precision=hi)
    w_im = jnp.einsum('kijnp,ih->kjiphn', ab_im[:, ::-1].reshape(K, GP, L, N, P), eye, precision=hi)
    w = jnp.concatenate([w_re.reshape(K, L * GP * P, GP * N), w_im.reshape(K, L * GP * P, GP * N)], axis=2)
    p1_re, p1_im = pw_re[:, 1:], pw_im[:, 1:]
    v_re = cr[:, None] * p1_re[:, :, None, :] - ci[:, None] * p1_im[:, :, None, :]
    v_im = -(cr[:, None] * p1_im[:, :, None, :] + ci[:, None] * p1_re[:, :, None, :])
    v_re = jnp.einsum('kitqn,ih->kinthq', v_re.reshape(K, GP, L, P, N), eye, precision=hi)
    v_im = jnp.einsum('kitqn,ih->kinthq', v_im.reshape(K, GP, L, P, N), eye, precision=hi)
    v = jnp.concatenate([v_re.reshape(K, GP * N, L * GP * P), v_im.reshape(K, GP * N, L * GP * P)], axis=1)
    al = jnp.stack([pw_re[:, L].reshape(K, GP * N), pw_im[:, L].reshape(K, GP * N)], axis=1)
    dvec = jnp.tile(d_skip.astype(F32).reshape(K, 1, GP * P), (1, L, 1)).reshape(K, 1, L * GP * P)
    return toep.astype(BF16), w.astype(BF16), v.astype(BF16), al, dvec


def _ssm_kernel(u_ref, toep_ref, w_ref, v_ref, al_ref, d_ref, y_ref, lhs_s, e_s, s_s, carry_s):
    bsz, tt, lanes = u_ref.shape
    L = SSM_CHUNK
    n_chunks = tt // L
    ns = al_ref.shape[2]

    @pl.when(pl.program_id(1) == 0)
    def _():
        carry_s[...] = jnp.zeros_like(carry_s)

    for b in range(bsz):
        for t in range(L):
            lhs_s[t, pl.ds(b, n_chunks, stride=bsz), :] = u_ref[b, pl.ds(t, n_chunks, stride=L), :]
    lhs = jnp.concatenate([lhs_s[t] for t in range(L)], axis=1)
    lb = lhs.astype(BF16)
    e_s[...] = jnp.dot(lb, w_ref[0], preferred_element_type=F32)
    ar = jnp.broadcast_to(al_ref[0, 0:1, :], (bsz, ns))
    ai = jnp.broadcast_to(al_ref[0, 1:2, :], (bsz, ns))
    sr = carry_s[0]
    si = carry_s[1]
    for c in range(n_chunks):
        rows = slice(c * bsz, (c + 1) * bsz)
        s_s[rows, :ns] = sr
        s_s[rows, ns:] = si
        er = e_s[rows, :ns]
        ei = e_s[rows, ns:]
        sr, si = ar * sr - ai * si + er, ar * si + ai * sr + ei
    carry_s[0] = sr
    carry_s[1] = si
    y = (jnp.dot(lb, toep_ref[0], preferred_element_type=F32)
         + jnp.dot(s_s[...].astype(BF16), v_ref[0], preferred_element_type=F32)
         + d_ref[0] * lhs)
    yg = jax.nn.gelu(y)
    for t in range(L):
        lhs_s[t] = yg[:, t * lanes:(t + 1) * lanes]
    for b in range(bsz):
        for t in range(L):
            y_ref[b, pl.ds(t, n_chunks, stride=L), :] = lhs_s[t, pl.ds(b, n_chunks, stride=bsz), :]


def _ssm(u3, mats):
    toep, w, v, al, dvec = mats
    bsz, seq, sw = u3.shape
    k = toep.shape[0]
    lanes = sw // k
    tt = SSM_TILE
    rows = bsz * tt // SSM_CHUNK
    cl = SSM_CHUNK * lanes
    ns = al.shape[2]
    pack = lambda p, t: (p, 0, 0)
    tile = lambda p, t: (0, t, p)
    return pl.pallas_call(
        _ssm_kernel,
        grid=(k, seq // tt),
        in_specs=[
            pl.BlockSpec((bsz, tt, lanes), tile),
            pl.BlockSpec((1, cl, cl), pack),
            pl.BlockSpec((1, cl, 2 * ns), pack),
            pl.BlockSpec((1, 2 * ns, cl), pack),
            pl.BlockSpec((1, 2, ns), pack),
            pl.BlockSpec((1, 1, cl), pack),
        ],
        out_specs=pl.BlockSpec((bsz, tt, lanes), tile),
        out_shape=jax.ShapeDtypeStruct((bsz, seq, sw), F32),
        scratch_shapes=[
            pltpu.VMEM((SSM_CHUNK, rows, lanes), F32),
            pltpu.VMEM((rows, 2 * ns), F32),
            pltpu.VMEM((rows, 2 * ns), F32),
            pltpu.VMEM((2, bsz, ns), F32),
        ],
        compiler_params=_cparams(2),
        name="ssm",
    )(u3, toep, w, v, al, dvec)


def _mix_kernel(tiles_per_seq, x_ref, up_ref, halo_ref, ys_ref, gate_ref, poolw_ref, pscale_ref,
                gluw_ref, glub_ref, wpu_ref, wsu_ref, wout_ref, gffn_ref, rwt_ref, rb_ref, tri_ref,
                x1_ref, h3_ref, idx_ref, gw_ref, rank_ref, cnt_ref, ext_s, carry_s):
    i = pl.program_id(0)
    j = i % tiles_per_seq
    tm = x_ref.shape[0]
    d_model = x_ref.shape[1]
    gdim = poolw_ref.shape[1]

    @pl.when(i == 0)
    def _():
        carry_s[...] = jnp.zeros_like(carry_s)

    ext_s[0:POOL_HALO, :] = jnp.where(j == 0, 0.0, halo_ref[...])
    ext_s[POOL_HALO:, :] = up_ref[...]
    pos = (j * tm + 1 + lax.broadcasted_iota(jnp.int32, (tm, 1), 0)).astype(F32)
    parts = []
    for g, w in enumerate(POOL_WINDOWS):
        cols = slice(g * gdim, (g + 1) * gdim)
        cur = ext_s[POOL_HALO:POOL_HALO + tm, cols]
        s = cur
        for k in range(1, w):
            s = s + ext_s[POOL_HALO - k:POOL_HALO - k + tm, cols]
        dlt = (s / jnp.minimum(pos, float(w)) - cur).astype(BF16)
        parts.append(jnp.dot(dlt, poolw_ref[g], preferred_element_type=F32))
    yp = jnp.concatenate(parts, axis=1) * pscale_ref[...]
    y_pool = jnp.dot(yp.astype(BF16), wpu_ref[...], preferred_element_type=F32)

    yg = ys_ref[...]
    glu = yg * jax.nn.sigmoid(
        jnp.dot(yg.astype(BF16), gluw_ref[...], preferred_element_type=F32) + glub_ref[...])
    y_ssm = jnp.dot(glu.astype(BF16), wsu_ref[...], preferred_element_type=F32)

    z = (gate_ref[:, :d_model].astype(F32) * y_pool + gate_ref[:, d_model:].astype(F32) * y_ssm)
    x1 = x_ref[...] + jnp.dot(z.astype(BF16), wout_ref[...], preferred_element_type=F32)
    x1_ref[...] = x1

    ms = jnp.mean(x1 * x1, axis=-1, keepdims=True)
    h2 = x1 * lax.rsqrt(ms + RMS_EPS) * gffn_ref[...]
    n_s = d_model // LANES
    for s in range(n_s):
        h3_ref[pl.ds(s, tm, stride=n_s), :] = h2[:, s * LANES:(s + 1) * LANES]

    logits = lax.dot_general(rwt_ref[...], h2, (((1,), (1,)), ((), ())),
                             precision=lax.Precision.HIGHEST,
                             preferred_element_type=F32) + rb_ref[...]
    n_e = logits.shape[0]
    iota_e = lax.broadcasted_iota(jnp.int32, (n_e, tm), 0)
    l = logits
    tops, hots = [], []
    for k in range(TOP_K):
        m = jnp.max(l, axis=0, keepdims=True)
        idx = jnp.min(jnp.where(l == m, iota_e, n_e), axis=0, keepdims=True)
        hot = iota_e == idx
        l = jnp.where(hot, -jnp.inf, l)
        tops.append(m)
        hots.append(hot)
        idx_ref[k:k + 1, :] = idx
    exps = [jnp.exp(m - tops[0]) for m in tops]
    den = exps[0] + exps[1] + exps[2] + exps[3]
    for k in range(TOP_K):
        gw_ref[k:k + 1, :] = exps[k] / den

    multi = sum(h.astype(F32) for h in hots)
    cum = jnp.dot(multi.astype(BF16), tri_ref[...], preferred_element_type=F32) + carry_s[:, 0:1]
    for k in range(TOP_K):
        rk = jnp.sum(jnp.where(hots[k], cum, 0.0), axis=0, keepdims=True)
        rank_ref[k:k + 1, :] = rk.astype(jnp.int32)
    carry_s[...] = carry_s[...] + jnp.sum(multi, axis=1, keepdims=True)
    cnt_ref[...] = carry_s[...]


def _mix(x2, u_pool, y_ssm, gates, pool_w, pool_scale, glu_w, glu_b, w_pool_up, w_ssm_up, w_out,
         g_ffn, rw_t, rb, seq_len):
    m, d = x2.shape
    pw = u_pool.shape[1]
    sw = y_ssm.shape[1]
    tm = ROW_TILE
    n_e = rw_t.shape[0]
    tiles_per_seq = seq_len // tm
    tri = (jnp.arange(tm)[:, None] < jnp.arange(tm)[None, :]).astype(BF16)
    row = lambda i: (i, 0)
    const2 = lambda i: (0, 0)
    const3 = lambda i: (0, 0, 0)
    col = lambda i: (0, i)
    halo = lambda i: (jnp.maximum(i * (tm // POOL_HALO) - 1, 0), 0)
    return pl.pallas_call(
        functools.partial(_mix_kernel, tiles_per_seq),
        grid=(m // tm,),
        in_specs=[
            pl.BlockSpec((tm, d), row),
            pl.BlockSpec((tm, pw), row),
            pl.BlockSpec((POOL_HALO, pw), halo),
            pl.BlockSpec((tm, sw), row),
            pl.BlockSpec((tm, 2 * d), row),
            pl.BlockSpec(pool_w.shape, const3),
            pl.BlockSpec((1, pw), const2),
            pl.BlockSpec((sw, sw), const2),
            pl.BlockSpec((1, sw), const2),
            pl.BlockSpec((pw, d), const2),
            pl.BlockSpec((sw, d), const2),
            pl.BlockSpec((d, d), const2),
            pl.BlockSpec((1, d), const2),
            pl.BlockSpec((n_e, d), const2),
            pl.BlockSpec((n_e, 1), const2),
            pl.BlockSpec((tm, tm), const2),
        ],
        out_specs=[
            pl.BlockSpec((tm, d), row),
            pl.BlockSpec((tm * (d // LANES), LANES), row),
            pl.BlockSpec((TOP_K, tm), col),
            pl.BlockSpec((TOP_K, tm), col),
            pl.BlockSpec((TOP_K, tm), col),
            pl.BlockSpec((n_e, LANES), const2),
        ],
        out_shape=[
            jax.ShapeDtypeStruct((m, d), F32),
            jax.ShapeDtypeStruct((m * (d // LANES), LANES), F32),
            jax.ShapeDtypeStruct((TOP_K, m), jnp.int32),
            jax.ShapeDtypeStruct((TOP_K, m), F32),
            jax.ShapeDtypeStruct((TOP_K, m), jnp.int32),
            jax.ShapeDtypeStruct((n_e, LANES), F32),
        ],
        scratch_shapes=[
            pltpu.VMEM((POOL_HALO + tm, pw), F32),
            pltpu.VMEM((n_e, LANES), F32),
        ],
        compiler_params=_cparams(1),
        name="mix_route",
    )(x2, u_pool, u_pool, y_ssm, gates, pool_w, pool_scale, glu_w, glu_b, w_pool_up, w_ssm_up,
      w_out, g_ffn, rw_t, rb, tri)


def _row_copy(src, dst, sem):
    return pltpu.make_async_copy(src, dst, sem)


def _for_each_run_piece(tab, tb, max_rows, n_s, fn):
    def per_expert(e, carry):
        n = tab[tb + e]
        off = tab[tb + N_EXPERTS + e]
        slot = tab[tb + 2 * N_EXPERTS + e]
        for b in range(max_rows.bit_length()):
            size = 1 << b

            @pl.when(((n >> b) & 1) == 1)
            def _():
                done = n & (size - 1)
                fn(pl.multiple_of((slot + done) * n_s, n_s), pl.multiple_of((off + done) * n_s, n_s),
                   size * n_s)
        return carry

    lax.fori_loop(0, N_EXPERTS, per_expert, 0)


def _dispatch_kernel(n_s, tab_ref, lpos_ref, zpos_ref, h3_ref, xs_ref, st0_s, st1_s, sem):
    j = pl.program_id(0)
    tm = h3_ref.shape[0] // (2 * n_s)
    per_tile = TOP_K * tm
    zrows = MOE_ROWS * n_s

    @pl.when(j == 0)
    def _():
        st0_s[0:zrows, :] = jnp.zeros((zrows, st0_s.shape[1]), F32)

        def fill(start):
            cp = _row_copy(st0_s.at[pl.ds(0, zrows)],
                           xs_ref.at[pl.ds(pl.multiple_of(start * n_s, n_s), zrows)], sem.at[0])
            cp.start()
            cp.wait()

        for e in range(N_EXPERTS):
            fill(zpos_ref[e])

        def tail(t, carry):
            fill(t * MOE_ROWS)
            return carry

        lax.fori_loop(zpos_ref[N_EXPERTS], xs_ref.shape[0] // zrows, tail, 0)

    def drain(stage, slot):
        _row_copy(stage, xs_ref.at[pl.ds(0, per_tile * n_s)], sem.at[slot]).wait()

    def scatter(stage, slot):
        @pl.when(j > 0)
        def _():
            drain(stage, slot)

        base = slot * per_tile

        def tokens(g, carry):
            for u in range(COMBINE_UNROLL):
                t = g * COMBINE_UNROLL + u
                rows = h3_ref[pl.ds(pl.multiple_of((slot * tm + t) * n_s, n_s), n_s), :]
                for k in range(TOP_K):
                    row = pl.multiple_of(lpos_ref[base + k * tm + t], n_s)
                    stage[pl.ds(row, n_s), :] = rows
            return carry

        lax.fori_loop(0, tm // COMBINE_UNROLL, tokens, 0)
        _for_each_run_piece(
            tab_ref, slot * RUN_TAB, tm, n_s,
            lambda s, b, r: _row_copy(stage.at[pl.ds(b, r)], xs_ref.at[pl.ds(s, r)],
                                      sem.at[slot]).start())

    scatter(st0_s, 0)
    scatter(st1_s, 1)

    @pl.when(j + 1 == pl.num_programs(0))
    def _():
        drain(st0_s, 0)
        drain(st1_s, 1)


def _dispatch(tab, lpos_tiles, zpos, h3, n_slots, n_s):
    rows, lanes = h3.shape
    tm = COMBINE_TILE
    smem = pltpu.SMEM
    return pl.pallas_call(
        functools.partial(_dispatch_kernel, n_s),
        grid=(rows // (2 * tm * n_s),),
        in_specs=[
            pl.BlockSpec((2 * RUN_TAB,), lambda j: (j,), memory_space=smem),
            pl.BlockSpec((2 * TOP_K * tm,), lambda j: (j,), memory_space=smem),
            pl.BlockSpec(memory_space=smem),
            pl.BlockSpec((2 * tm * n_s, lanes), lambda j: (j, 0)),
        ],
        out_specs=pl.BlockSpec(memory_space=pl.ANY),
        out_shape=jax.ShapeDtypeStruct(((n_slots + MOE_ROWS) * n_s, lanes), F32),
        scratch_shapes=[pltpu.VMEM((TOP_K * tm * n_s, lanes), F32),
                        pltpu.VMEM((TOP_K * tm * n_s, lanes), F32),
                        pltpu.SemaphoreType.DMA((2,))],
        compiler_params=_cparams(1),
        name="dispatch",
    )(tab, lpos_tiles, zpos, h3)


def _expert_kernel(blk_e_ref, first_ref, nused_ref, xs_ref, w1_ref, b1_ref, w2_ref, b2_ref, perm_ref,
                   ys_ref, w1p_s, w2b_s):
    i = pl.program_id(0)
    n_s = w1_ref.shape[1] // LANES
    rows = xs_ref.shape[0] // n_s
    f2 = w1_ref.shape[2]

    @pl.when(first_ref[i] == 1)
    def _():
        for c in range(f2 // DEINT):
            cols = slice(c * DEINT, (c + 1) * DEINT)
            strip = w1_ref[0, :, cols].astype(BF16)
            w1p_s[:, cols] = jnp.dot(strip, perm_ref[...], preferred_element_type=F32).astype(BF16)
        w2b_s[...] = w2_ref[0].astype(BF16)

    @pl.when(i < nused_ref[0])
    def _():
        x = jnp.concatenate([xs_ref[pl.ds(s, rows, stride=n_s), :] for s in range(n_s)],
                            axis=1).astype(BF16)
        h = jnp.dot(x, w1p_s[...], preferred_element_type=F32) + b1_ref[0]
        acts = []
        for c in range(f2 // DEINT):
            xg = jnp.minimum(h[:, c * DEINT:c * DEINT + LANES], SWIGLU_LIMIT)
            xl = jnp.clip(h[:, c * DEINT + LANES:(c + 1) * DEINT], -SWIGLU_LIMIT, SWIGLU_LIMIT)
            acts.append((xg * jax.nn.sigmoid(SWIGLU_ALPHA * xg) * (xl + 1.0)).astype(BF16))
        act = jnp.concatenate(acts, axis=1)
        y = jnp.dot(act, w2b_s[...], preferred_element_type=F32) + b2_ref[0]
        for s in range(n_s):
            ys_ref[pl.ds(s, rows, stride=n_s), :] = y[:, s * LANES:(s + 1) * LANES]

    @pl.when(i >= nused_ref[0])
    def _():
        ys_ref[...] = jnp.zeros_like(ys_ref)


def _experts(blk_e, first, n_used, xs, w1, b1p, w2, b2, n_blocks):
    _, lanes = xs.shape
    _, d, f2 = w1.shape
    blk = MOE_ROWS * (d // lanes)
    f = w2.shape[1]
    half = jnp.arange(DEINT) // 2 + (jnp.arange(DEINT) % 2) * LANES
    perm = (half[:, None] == jnp.arange(DEINT)[None, :]).astype(BF16)
    xmap = lambda i, be, fi, nu: (jnp.minimum(i, nu[0] - 1), 0)
    emap = lambda i, be, fi, nu: (be[i], 0, 0)
    grid_spec = pltpu.PrefetchScalarGridSpec(
        num_scalar_prefetch=3,
        grid=(n_blocks,),
        in_specs=[
            pl.BlockSpec((blk, lanes), xmap),
            pl.BlockSpec((1, d, f2), emap),
            pl.BlockSpec((1, 1, f2), emap),
            pl.BlockSpec((1, f, d), emap),
            pl.BlockSpec((1, 1, d), emap),
            pl.BlockSpec((DEINT, DEINT), lambda i, be, fi, nu: (0, 0)),
        ],
        out_specs=pl.BlockSpec((blk, lanes), lambda i, be, fi, nu: (i, 0)),
        scratch_shapes=[pltpu.VMEM((d, f2), BF16), pltpu.VMEM((f, d), BF16)],
    )
    return pl.pallas_call(
        _expert_kernel,
        grid_spec=grid_spec,
        out_shape=jax.ShapeDtypeStruct((n_blocks * blk, lanes), F32),
        compiler_params=_cparams(1),
        name="experts",
    )(blk_e, first, n_used, xs, w1, b1p, w2, b2, perm)


def _combine_kernel(tab_ref, nxt_ref, lpos_ref, gw_ref, x1_ref, gfin_ref, ys_ref, out_ref,
                    buf0_s, buf1_s, acc_s, sem):
    j = pl.program_id(0)
    tm = x1_ref.shape[0] // 2
    n_s = x1_ref.shape[1] // LANES
    per_tile = TOP_K * tm
    tab_w = RUN_TAB

    def gather(tab, tb, buf, slot):
        _for_each_run_piece(
            tab, tb, tm, n_s,
            lambda s, b, r: _row_copy(ys_ref.at[pl.ds(s, r)], buf.at[pl.ds(b, r)],
                                      sem.at[slot]).start())

    def reduce(buf, slot, t0):
        _row_copy(ys_ref.at[pl.ds(0, per_tile * n_s)], buf, sem.at[slot]).wait()
        base = (t0 // tm) * per_tile

        def tokens(g, carry):
            for u in range(COMBINE_UNROLL):
                t = g * COMBINE_UNROLL + u
                acc = None
                for k in range(TOP_K):
                    row = pl.multiple_of(lpos_ref[base + k * tm + t], n_s)
                    term = gw_ref[base + k * tm + t] * buf[pl.ds(row, n_s), :]
                    acc = term if acc is None else acc + term
                acc_s[pl.ds(pl.multiple_of(t * n_s, n_s), n_s), :] = acc
            return carry

        lax.fori_loop(0, tm // COMBINE_UNROLL, tokens, 0)
        y = jnp.concatenate([acc_s[pl.ds(s, tm, stride=n_s), :] for s in range(n_s)], axis=1)
        acc = x1_ref[t0:t0 + tm, :] + y
        ms = jnp.mean(acc * acc, axis=-1, keepdims=True)
        out_ref[t0:t0 + tm, :] = acc * lax.rsqrt(ms + RMS_EPS) * gfin_ref[...]

    @pl.when(j == 0)
    def _():
        gather(tab_ref, 0, buf0_s, 0)

    gather(tab_ref, tab_w, buf1_s, 1)
    reduce(buf0_s, 0, 0)

    @pl.when(j + 1 < pl.num_programs(0))
    def _():
        gather(nxt_ref, 0, buf0_s, 0)

    reduce(buf1_s, 1, tm)


def _combine(tab, lpos_tiles, gw_tiles, x1, g_final, ys):
    m, d = x1.shape
    _, lanes = ys.shape
    s = d // lanes
    tm = COMBINE_TILE
    n_pairs = m // (2 * tm)
    tab_w = RUN_TAB
    smem = pltpu.SMEM
    return pl.pallas_call(
        _combine_kernel,
        grid=(n_pairs,),
        in_specs=[
            pl.BlockSpec((2 * tab_w,), lambda j: (j,), memory_space=smem),
            pl.BlockSpec((tab_w,), lambda j: (jnp.minimum(2 * j + 2, 2 * n_pairs - 1),),
                         memory_space=smem),
            pl.BlockSpec((2 * TOP_K * tm,), lambda j: (j,), memory_space=smem),
            pl.BlockSpec((2 * TOP_K * tm,), lambda j: (j,), memory_space=smem),
            pl.BlockSpec((2 * tm, d), lambda j: (j, 0)),
            pl.BlockSpec((1, d), lambda j: (0, 0)),
            pl.BlockSpec(memory_space=pl.ANY),
        ],
        out_specs=pl.BlockSpec((2 * tm, d), lambda j: (j, 0)),
        out_shape=jax.ShapeDtypeStruct((m, d), F32),
        scratch_shapes=[pltpu.VMEM((TOP_K * tm * s, lanes), F32),
                        pltpu.VMEM((TOP_K * tm * s, lanes), F32),
                        pltpu.VMEM((tm * s, lanes), F32),
                        pltpu.SemaphoreType.DMA((2,))],
        compiler_params=_cparams(1),
        name="combine",
    )(tab, tab, lpos_tiles, gw_tiles, x1, g_final, ys)


def kernel(x, norm_mix_g, w_in, b_gate, pool_w, pool_scale, ssm_lambda_re, ssm_lambda_im, ssm_log_dt, ssm_b_re, ssm_b_im, ssm_c_re, ssm_c_im, ssm_d, ssm_glu_w, ssm_glu_b, w_pool_up, w_ssm_up, w_out, norm_ffn_g, router_w, router_b, moe_w1, moe_b1, moe_w2, moe_b2, norm_final_g):
    bsz, seq, d = x.shape
    depth = w_in.shape[0]
    pw = pool_w.shape[1] * pool_w.shape[2]
    n_groups = ssm_lambda_re.shape[1]
    sw = n_groups * SSM_GROUP_DIM
    m = bsz * seq
    assert depth == 1
    assert bsz == SUBLANES and seq % ROW_TILE == 0 and seq % SSM_TILE == 0 and d % LANES == 0
    assert pool_w.shape[1] == len(POOL_WINDOWS) and n_groups % SSM_PACK == 0
    assert moe_w1.shape[3] % DEINT == 0

    x2 = x.reshape(m, d)
    for l in range(depth):
        wi = w_in[l].astype(BF16)
        u_pool, u_ssm, gates = _inproj(
            x2, norm_mix_g[l][None], wi[:, :pw], wi[:, pw:pw + sw], wi[:, pw + sw:], b_gate[l][None])
        mats = _ssm_matrices(ssm_lambda_re[l], ssm_lambda_im[l], ssm_log_dt[l], ssm_b_re[l],
                             ssm_b_im[l], ssm_c_re[l], ssm_c_im[l], ssm_d[l])
        y_ssm = _ssm(u_ssm.reshape(bsz, seq, sw), mats).reshape(m, sw)
        x1, h3, idx_t, gw_t, rank_t, cnt = _mix(
            x2, u_pool, y_ssm, gates, pool_w[l].astype(BF16), pool_scale[l][None],
            ssm_glu_w[l].astype(BF16), ssm_glu_b[l][None], w_pool_up[l].astype(BF16),
            w_ssm_up[l].astype(BF16), w_out[l].astype(BF16), norm_ffn_g[l][None],
            router_w[l].T, router_b[l][:, None], seq)

        n_assign = m * TOP_K
        n_blocks = (n_assign + N_EXPERTS * (MOE_ROWS - 1) + MOE_ROWS - 1) // MOE_ROWS
        n_slots = n_blocks * MOE_ROWS
        counts = cnt[:, 0].astype(jnp.int32)
        padded = (counts + MOE_ROWS - 1) // MOE_ROWS * MOE_ROWS
        pad_end = jnp.cumsum(padded)
        pad_start = pad_end - padded
        n_used = (pad_end[-1] // MOE_ROWS).astype(jnp.int32)[None]
        blk_row = jnp.arange(n_blocks, dtype=jnp.int32) * MOE_ROWS
        blk_e = jnp.minimum(jnp.sum(pad_end[None, :] <= blk_row[:, None], axis=1),
                            N_EXPERTS - 1).astype(jnp.int32)
        first = jnp.concatenate([jnp.ones((1,), jnp.int32),
                                 (blk_e[1:] != blk_e[:-1]).astype(jnp.int32)])
        e_ids = jnp.arange(N_EXPERTS, dtype=jnp.int32)[:, None, None]
        zpos = jnp.concatenate([pad_start + counts, n_used]).astype(jnp.int32)
        ct = COMBINE_TILE
        n_ct = m // ct
        cnt_te = jnp.sum((idx_t.reshape(TOP_K, n_ct, ct)[None] == e_ids[..., None]).astype(jnp.int32),
                         axis=(1, 3))
        carry_te = jnp.cumsum(cnt_te, axis=1) - cnt_te
        offs_te = jnp.cumsum(cnt_te, axis=0) - cnt_te
        tab = jnp.stack([cnt_te, offs_te, pad_start[:, None] + carry_te, jnp.zeros_like(cnt_te)],
                        axis=0)
        tab = tab.transpose(2, 0, 1).reshape(-1).astype(jnp.int32)
        delta = jnp.repeat(offs_te - carry_te, ct, axis=1)
        lpos = rank_t + jnp.sum(jnp.where(idx_t[None] == e_ids, delta[:, None, :], 0), axis=0)
        tile_major = lambda a: a.reshape(TOP_K, n_ct, ct).transpose(1, 0, 2).reshape(-1)
        lpos_tiles = tile_major((lpos * (d // LANES)).astype(jnp.int32))

        xs = _dispatch(tab, lpos_tiles, zpos, h3, n_slots, d // LANES)
        f2 = moe_w1.shape[3]
        b1p = (moe_b1[l].reshape(N_EXPERTS, f2 // DEINT, LANES, 2)
               .transpose(0, 1, 3, 2).reshape(N_EXPERTS, 1, f2))
        ys = _experts(blk_e, first, n_used, xs, moe_w1[l], b1p, moe_w2[l], moe_b2[l][:, None, :],
                      n_blocks)
        x2 = _combine(tab, lpos_tiles, tile_major(gw_t), x1, norm_final_g[None], ys)
    return x2.reshape(bsz, seq, d)
```

```python
import functools
import math

import jax
import jax.numpy as jnp
from jax import lax
from jax.experimental import pallas as pl
from jax.experimental.pallas import tpu as pltpu

F32 = jnp.float32
BF16 = jnp.bfloat16

RMS_EPS = 1e-6
POOL_WINDOWS = (2, 4, 8, 16)
POOL_HALO = 16
SSM_GROUP_DIM = 16
SSM_STATE = 64
N_EXPERTS = 32
TOP_K = 4
SWIGLU_ALPHA = 1.702
SWIGLU_LIMIT = 7.0

LANES = 128
SUBLANES = 8
SSM_CHUNK = SUBLANES
SSM_PACK = LANES // SSM_GROUP_DIM
SSM_TILE = 512
DEINT = 2 * LANES
ROW_TILE = 512
COMBINE_TILE = 512
COMBINE_UNROLL = 4
RUN_TAB = 4 * N_EXPERTS
MOE_ROWS = 512
VMEM_LIMIT = 56 * 1024 * 1024


def _cparams(n_axes):
    return pltpu.CompilerParams(
        dimension_semantics=("arbitrary",) * n_axes, vmem_limit_bytes=VMEM_LIMIT)


def _inproj_kernel(x_ref, g_ref, wp_ref, ws_ref, wg_ref, bg_ref, up_ref, us_ref, gate_ref):
    x = x_ref[...]
    ms = jnp.mean(x * x, axis=-1, keepdims=True)
    h = (x * lax.rsqrt(ms + RMS_EPS) * g_ref[...]).astype(BF16)
    up_ref[...] = jnp.dot(h, wp_ref[...], preferred_element_type=F32)
    us_ref[...] = jnp.dot(h, ws_ref[...], preferred_element_type=F32)
    gl = jnp.dot(h, wg_ref[...], preferred_element_type=F32) + bg_ref[...]
    gate_ref[...] = jax.nn.sigmoid(gl).astype(BF16)


def _inproj(x2, g, wp, ws, wg, bg):
    m, d = x2.shape
    pw, sw, gw = wp.shape[1], ws.shape[1], wg.shape[1]
    tm = ROW_TILE
    const = lambda i: (0, 0)
    row = lambda i: (i, 0)
    return pl.pallas_call(
        _inproj_kernel,
        grid=(m // tm,),
        in_specs=[
            pl.BlockSpec((tm, d), row),
            pl.BlockSpec((1, d), const),
            pl.BlockSpec((d, pw), const),
            pl.BlockSpec((d, sw), const),
            pl.BlockSpec((d, gw), const),
            pl.BlockSpec((1, gw), const),
        ],
        out_specs=[
            pl.BlockSpec((tm, pw), row),
            pl.BlockSpec((tm, sw), row),
            pl.BlockSpec((tm, gw), row),
        ],
        out_shape=[
            jax.ShapeDtypeStruct((m, pw), F32),
            jax.ShapeDtypeStruct((m, sw), F32),
            jax.ShapeDtypeStruct((m, gw), BF16),
        ],
        compiler_params=_cparams(1),
        name="inproj",
    )(x2, g, wp, ws, wg, bg)


def _ssm_matrices(lam_re, lam_im, log_dt, b_re, b_im, c_re, c_im, d_skip):
    hi = lax.Precision.HIGHEST
    L, P, N, GP = SSM_CHUNK, SSM_GROUP_DIM, SSM_STATE, SSM_PACK
    G = lam_re.shape[0]
    K = G // GP
    lr, li = lam_re.astype(F32), lam_im.astype(F32)
    dt = jnp.exp(log_dt.astype(F32))[:, None]
    mag = jnp.exp(lr * dt)
    lb_re, lb_im = mag * jnp.cos(li * dt), mag * jnp.sin(li * dt)
    den = lr * lr + li * li
    xr, xi = lb_re - 1.0, lb_im
    f_re = (xr * lr + xi * li) / den
    f_im = (xi * lr - xr * li) / den
    br, bi = b_re.astype(F32), b_im.astype(F32)
    bb_re = f_re[..., None] * br - f_im[..., None] * bi
    bb_im = f_re[..., None] * bi + f_im[..., None] * br
    tau = jnp.arange(L + 1, dtype=F32)[None, :, None]
    pmag = jnp.exp(lr[:, None, :] * dt[:, None, :] * tau)
    pang = li[:, None, :] * dt[:, None, :] * tau
    pw_re, pw_im = pmag * jnp.cos(pang), pmag * jnp.sin(pang)
    ab_re = pw_re[:, :L, :, None] * bb_re[:, None] - pw_im[:, :L, :, None] * bb_im[:, None]
    ab_im = pw_re[:, :L, :, None] * bb_im[:, None] + pw_im[:, :L, :, None] * bb_re[:, None]
    cr, ci = c_re.astype(F32), c_im.astype(F32)
    kern = (jnp.einsum('gqn,glnp->glqp', cr, ab_re, precision=hi)
            - jnp.einsum('gqn,glnp->glqp', ci, ab_im, precision=hi))
    eye = jnp.eye(GP, dtype=F32)
    kern_p = jnp.einsum('kilqp,ih->kiplhq', kern.reshape(K, GP, L, P, P), eye, precision=hi)
    kern_p = kern_p.reshape(K, GP * P, L, GP * P)
    kern_p = jnp.concatenate([jnp.zeros_like(kern_p), kern_p], axis=2)
    toep = jnp.stack([kern_p[:, :, L - j:2 * L - j, :].reshape(K, GP * P, L * GP * P)
                      for j in range(L)], axis=1)
    toep = toep.reshape(K, L * GP * P, L * GP * P)
    w_re = jnp.einsum('kijnp,ih->kjiphn', ab_re[:, ::-1].reshape(K, GP, L, N, P), eye, precision=hi)
    w_im = jnp.einsum('kijnp,ih->kjiphn', ab_im[:, ::-1].reshape(K, GP, L, N, P), eye, precision=hi)
    w = jnp.concatenate([w_re.reshape(K, L * GP * P, GP * N), w_im.reshape(K, L * GP * P, GP * N)], axis=2)
    p1_re, p1_im = pw_re[:, 1:], pw_im[:, 1:]
    v_re = cr[:, None] * p1_re[:, :, None, :] - ci[:, None] * p1_im[:, :, None, :]
    v_im = -(cr[:, None] * p1_im[:, :, None, :] + ci[:, None] * p1_re[:, :, None, :])
    v_re = jnp.einsum('kitqn,ih->kinthq', v_re.reshape(K, GP, L, P, N), eye, precision=hi)
    v_im = jnp.einsum('kitqn,ih->kinthq', v_im.reshape(K, GP, L, P, N), eye, precision=hi)
    v = jnp.concatenate([v_re.reshape(K, GP * N, L * GP * P), v_im.reshape(K, GP * N, L * GP * P)], axis=1)
    al = jnp.stack([pw_re[:, L].reshape(K, GP * N), pw_im[:, L].reshape(K, GP * N)], axis=1)
    dvec = jnp.tile(d_skip.astype(F32).reshape(K, 1, GP * P), (1, L, 1)).reshape(K, 1, L * GP * P)
    return toep.astype(BF16), w.astype(BF16), v.astype(BF16), al, dvec


def _ssm_kernel(u_ref, toep_ref, w_ref, v_ref, al_ref, d_ref, y_ref, lhs_s, e_s, s_s, carry_s):
    bsz, tt, lanes = u_ref.shape
    L = SSM_CHUNK
    n_chunks = tt // L
    ns = al_ref.shape[2]

    @pl.when(pl.program_id(1) == 0)
    def _():
        carry_s[...] = jnp.zeros_like(carry_s)

    for b in range(bsz):
        for t in range(L):
            lhs_s[t, pl.ds(b, n_chunks, stride=bsz), :] = u_ref[b, pl.ds(t, n_chunks, stride=L), :]
    lhs = jnp.concatenate([lhs_s[t] for t in range(L)], axis=1)
    lb = lhs.astype(BF16)
    e_s[...] = jnp.dot(lb, w_ref[0], preferred_element_type=F32)
    ar = jnp.broadcast_to(al_ref[0, 0:1, :], (bsz, ns))
    ai = jnp.broadcast_to(al_ref[0, 1:2, :], (bsz, ns))
    sr = carry_s[0]
    si = carry_s[1]
    for c in range(n_chunks):
        rows = slice(c * bsz, (c + 1) * bsz)
        s_s[rows, :ns] = sr
        s_s[rows, ns:] = si
        er = e_s[rows, :ns]
        ei = e_s[rows, ns:]
        sr, si = ar * sr - ai * si + er, ar * si + ai * sr + ei
    carry_s[0] = sr
    carry_s[1] = si
    y = (jnp.dot(lb, toep_ref[0], preferred_element_type=F32)
         + jnp.dot(s_s[...].astype(BF16), v_ref[0], preferred_element_type=F32)
         + d_ref[0] * lhs)
    yg = jax.nn.gelu(y)
    for t in range(L):
        lhs_s[t] = yg[:, t * lanes:(t + 1) * lanes]
    for b in range(bsz):
        for t in range(L):
            y_ref[b, pl.ds(t, n_chunks, stride=L), :] = lhs_s[t, pl.ds(b, n_chunks, stride=bsz), :]


def _ssm(u3, mats):
    toep, w, v, al, dvec = mats
    bsz, seq, sw = u3.shape
    k = toep.shape[0]
    lanes = sw // k
    tt = SSM_TILE
    rows = bsz * tt // SSM_CHUNK
    cl = SSM_CHUNK * lanes
    ns = al.shape[2]
    pack = lambda p, t: (p, 0, 0)
    tile = lambda p, t: (0, t, p)
    return pl.pallas_call(
        _ssm_kernel,
        grid=(k, seq // tt),
        in_specs=[
            pl.BlockSpec((bsz, tt, lanes), tile),
            pl.BlockSpec((1, cl, cl), pack),
            pl.BlockSpec((1, cl, 2 * ns), pack),
            pl.BlockSpec((1, 2 * ns, cl), pack),
            pl.BlockSpec((1, 2, ns), pack),
            pl.BlockSpec((1, 1, cl), pack),
        ],
        out_specs=pl.BlockSpec((bsz, tt, lanes), tile),
        out_shape=jax.ShapeDtypeStruct((bsz, seq, sw), F32),
        scratch_shapes=[
            pltpu.VMEM((SSM_CHUNK, rows, lanes), F32),
            pltpu.VMEM((rows, 2 * ns), F32),
            pltpu.VMEM((rows, 2 * ns), F32),
            pltpu.VMEM((2, bsz, ns), F32),
        ],
        compiler_params=_cparams(2),
        name="ssm",
    )(u3, toep, w, v, al, dvec)


def _mix_kernel(tiles_per_seq, x_ref, up_ref, halo_ref, ys_ref, gate_ref, poolw_ref, pscale_ref,
                gluw_ref, glub_ref, wpu_ref, wsu_ref, wout_ref, gffn_ref, rwh_ref, rwl_ref, rb_ref,
                tri_ref,
                x1_ref, h3_ref, idx_ref, gw_ref, rank_ref, cnt_ref, ext_s, carry_s):
    i = pl.program_id(0)
    j = i % tiles_per_seq
    tm = x_ref.shape[0]
    d_model = x_ref.shape[1]
    gdim = poolw_ref.shape[1]

    @pl.when(i == 0)
    def _():
        carry_s[...] = jnp.zeros_like(carry_s)

    ext_s[0:POOL_HALO, :] = jnp.where(j == 0, 0.0, halo_ref[...])
    ext_s[POOL_HALO:, :] = up_ref[...]
    pos = (j * tm + 1 + lax.broadcasted_iota(jnp.int32, (tm, 1), 0)).astype(F32)
    parts = []
    for g, w in enumerate(POOL_WINDOWS):
        cols = slice(g * gdim, (g + 1) * gdim)
        s = ext_s[:, cols]
        k = 1
        while k < w:
            s = s + pltpu.roll(s, k, 0)
            k *= 2
        s = s[POOL_HALO:POOL_HALO + tm]
        cur = ext_s[POOL_HALO:POOL_HALO + tm, cols]
        dlt = (s / jnp.minimum(pos, float(w)) - cur).astype(BF16)
        parts.append(jnp.dot(dlt, poolw_ref[g], preferred_element_type=F32))
    yp = jnp.concatenate(parts, axis=1) * pscale_ref[...]
    y_pool = jnp.dot(yp.astype(BF16), wpu_ref[...], preferred_element_type=F32)

    yg = ys_ref[...]
    glu = yg * jax.nn.sigmoid(
        jnp.dot(yg.astype(BF16), gluw_ref[...], preferred_element_type=F32) + glub_ref[...])
    y_ssm = jnp.dot(glu.astype(BF16), wsu_ref[...], preferred_element_type=F32)

    z = (gate_ref[:, :d_model].astype(F32) * y_pool + gate_ref[:, d_model:].astype(F32) * y_ssm)
    x1 = x_ref[...] + jnp.dot(z.astype(BF16), wout_ref[...], preferred_element_type=F32)
    x1_ref[...] = x1

    ms = jnp.mean(x1 * x1, axis=-1, keepdims=True)
    h2 = x1 * lax.rsqrt(ms + RMS_EPS) * gffn_ref[...]
    n_s = d_model // LANES
    for s in range(n_s):
        h3_ref[pl.ds(s, tm, stride=n_s), :] = h2[:, s * LANES:(s + 1) * LANES]

    h_hi = h2.astype(BF16)
    h_lo = (h2 - h_hi.astype(F32)).astype(BF16)
    nt = (((1,), (1,)), ((), ()))
    logits = (lax.dot_general(rwh_ref[...], h_hi, nt, preferred_element_type=F32)
              + lax.dot_general(rwh_ref[...], h_lo, nt, preferred_element_type=F32)
              + lax.dot_general(rwl_ref[...], h_hi, nt, preferred_element_type=F32)
              + rb_ref[...])
    n_e = logits.shape[0]
    iota_e = lax.broadcasted_iota(jnp.int32, (n_e, tm), 0)
    l = logits
    tops, hots = [], []
    for k in range(TOP_K):
        m = jnp.max(l, axis=0, keepdims=True)
        idx = jnp.min(jnp.where(l == m, iota_e, n_e), axis=0, keepdims=True)
        hot = iota_e == idx
        l = jnp.where(hot, -jnp.inf, l)
        tops.append(m)
        hots.append(hot)
        idx_ref[k:k + 1, :] = idx
    exps = [jnp.exp(m - tops[0]) for m in tops]
    den = exps[0] + exps[1] + exps[2] + exps[3]
    for k in range(TOP_K):
        gw_ref[k:k + 1, :] = exps[k] / den

    multi = sum(h.astype(F32) for h in hots)
    cum = jnp.dot(multi.astype(BF16), tri_ref[...], preferred_element_type=F32) + carry_s[:, 0:1]
    for k in range(TOP_K):
        rk = jnp.sum(jnp.where(hots[k], cum, 0.0), axis=0, keepdims=True)
        rank_ref[k:k + 1, :] = rk.astype(jnp.int32)
    carry_s[...] = carry_s[...] + jnp.sum(multi, axis=1, keepdims=True)
    cnt_ref[...] = carry_s[...]


def _mix(x2, u_pool, y_ssm, gates, pool_w, pool_scale, glu_w, glu_b, w_pool_up, w_ssm_up, w_out,
         g_ffn, rw_t, rb, seq_len):
    m, d = x2.shape
    pw = u_pool.shape[1]
    sw = y_ssm.shape[1]
    tm = ROW_TILE
    n_e = rw_t.shape[0]
    rw_hi = rw_t.astype(BF16)
    rw_lo = (rw_t - rw_hi.astype(F32)).astype(BF16)
    tiles_per_seq = seq_len // tm
    tri = (jnp.arange(tm)[:, None] < jnp.arange(tm)[None, :]).astype(BF16)
    row = lambda i: (i, 0)
    const2 = lambda i: (0, 0)
    const3 = lambda i: (0, 0, 0)
    col = lambda i: (0, i)
    halo = lambda i: (jnp.maximum(i * (tm // POOL_HALO) - 1, 0), 0)
    return pl.pallas_call(
        functools.partial(_mix_kernel, tiles_per_seq),
        grid=(m // tm,),
        in_specs=[
            pl.BlockSpec((tm, d), row),
            pl.BlockSpec((tm, pw), row),
            pl.BlockSpec((POOL_HALO, pw), halo),
            pl.BlockSpec((tm, sw), row),
            pl.BlockSpec((tm, 2 * d), row),
            pl.BlockSpec(pool_w.shape, const3),
            pl.BlockSpec((1, pw), const2),
            pl.BlockSpec((sw, sw), const2),
            pl.BlockSpec((1, sw), const2),
            pl.BlockSpec((pw, d), const2),
            pl.BlockSpec((sw, d), const2),
            pl.BlockSpec((d, d), const2),
            pl.BlockSpec((1, d), const2),
            pl.BlockSpec((n_e, d), const2),
            pl.BlockSpec((n_e, d), const2),
            pl.BlockSpec((n_e, 1), const2),
            pl.BlockSpec((tm, tm), const2),
        ],
        out_specs=[
            pl.BlockSpec((tm, d), row),
            pl.BlockSpec((tm * (d // LANES), LANES), row),
            pl.BlockSpec((TOP_K, tm), col),
            pl.BlockSpec((TOP_K, tm), col),
            pl.BlockSpec((TOP_K, tm), col),
            pl.BlockSpec((n_e, LANES), const2),
        ],
        out_shape=[
            jax.ShapeDtypeStruct((m, d), F32),
            jax.ShapeDtypeStruct((m * (d // LANES), LANES), F32),
            jax.ShapeDtypeStruct((TOP_K, m), jnp.int32),
            jax.ShapeDtypeStruct((TOP_K, m), F32),
            jax.ShapeDtypeStruct((TOP_K, m), jnp.int32),
            jax.ShapeDtypeStruct((n_e, LANES), F32),
        ],
        scratch_shapes=[
            pltpu.VMEM((POOL_HALO + tm, pw), F32),
            pltpu.VMEM((n_e, LANES), F32),
        ],
        compiler_params=_cparams(1),
        name="mix_route",
    )(x2, u_pool, u_pool, y_ssm, gates, pool_w, pool_scale, glu_w, glu_b, w_pool_up, w_ssm_up,
      w_out, g_ffn, rw_hi, rw_lo, rb, tri)


def _row_copy(src, dst, sem):
    return pltpu.make_async_copy(src, dst, sem)


def _for_each_run_piece(tab, tb, max_rows, n_s, fn):
    def per_expert(e, carry):
        n = tab[tb + e]
        off = tab[tb + N_EXPERTS + e]
        slot = tab[tb + 2 * N_EXPERTS + e]
        for b in range(max_rows.bit_length()):
            size = 1 << b

            @pl.when(((n >> b) & 1) == 1)
            def _():
                done = n & (size - 1)
                fn(pl.multiple_of((slot + done) * n_s, n_s), pl.multiple_of((off + done) * n_s, n_s),
                   size * n_s)
        return carry

    lax.fori_loop(0, N_EXPERTS, per_expert, 0)


def _dispatch_kernel(n_s, tab_ref, lpos_ref, zpos_ref, h3_ref, xs_ref, st0_s, st1_s, sem):
    j = pl.program_id(0)
    tm = h3_ref.shape[0] // (2 * n_s)
    per_tile = TOP_K * tm
    zrows = MOE_ROWS * n_s

    @pl.when(j == 0)
    def _():
        st0_s[0:zrows, :] = jnp.zeros((zrows, st0_s.shape[1]), F32)

        def fill(start):
            cp = _row_copy(st0_s.at[pl.ds(0, zrows)],
                           xs_ref.at[pl.ds(pl.multiple_of(start * n_s, n_s), zrows)], sem.at[0])
            cp.start()
            cp.wait()

        for e in range(N_EXPERTS):
            fill(zpos_ref[e])

        def tail(t, carry):
            fill(t * MOE_ROWS)
            return carry

        lax.fori_loop(zpos_ref[N_EXPERTS], xs_ref.shape[0] // zrows, tail, 0)

    def drain(stage, slot):
        _row_copy(stage, xs_ref.at[pl.ds(0, per_tile * n_s)], sem.at[slot]).wait()

    def scatter(stage, slot):
        @pl.when(j > 0)
        def _():
            drain(stage, slot)

        base = slot * per_tile

        def tokens(g, carry):
            for u in range(COMBINE_UNROLL):
                t = g * COMBINE_UNROLL + u
                rows = h3_ref[pl.ds(pl.multiple_of((slot * tm + t) * n_s, n_s), n_s), :]
                for k in range(TOP_K):
                    row = pl.multiple_of(lpos_ref[base + k * tm + t], n_s)
                    stage[pl.ds(row, n_s), :] = rows
            return carry

        lax.fori_loop(0, tm // COMBINE_UNROLL, tokens, 0)
        _for_each_run_piece(
            tab_ref, slot * RUN_TAB, tm, n_s,
            lambda s, b, r: _row_copy(stage.at[pl.ds(b, r)], xs_ref.at[pl.ds(s, r)],
                                      sem.at[slot]).start())

    scatter(st0_s, 0)
    scatter(st1_s, 1)

    @pl.when(j + 1 == pl.num_programs(0))
    def _():
        drain(st0_s, 0)
        drain(st1_s, 1)


def _dispatch(tab, lpos_tiles, zpos, h3, n_slots, n_s):
    rows, lanes = h3.shape
    tm = COMBINE_TILE
    smem = pltpu.SMEM
    return pl.pallas_call(
        functools.partial(_dispatch_kernel, n_s),
        grid=(rows // (2 * tm * n_s),),
        in_specs=[
            pl.BlockSpec((2 * RUN_TAB,), lambda j: (j,), memory_space=smem),
            pl.BlockSpec((2 * TOP_K * tm,), lambda j: (j,), memory_space=smem),
            pl.BlockSpec(memory_space=smem),
            pl.BlockSpec((2 * tm * n_s, lanes), lambda j: (j, 0)),
        ],
        out_specs=pl.BlockSpec(memory_space=pl.ANY),
        out_shape=jax.ShapeDtypeStruct(((n_slots + MOE_ROWS) * n_s, lanes), F32),
        scratch_shapes=[pltpu.VMEM((TOP_K * tm * n_s, lanes), F32),
                        pltpu.VMEM((TOP_K * tm * n_s, lanes), F32),
                        pltpu.SemaphoreType.DMA((2,))],
        compiler_params=_cparams(1),
        name="dispatch",
    )(tab, lpos_tiles, zpos, h3)


def _expert_kernel(blk_e_ref, first_ref, nused_ref, xs_ref, w1_ref, b1_ref, w2_ref, b2_ref, perm_ref,
                   ys_ref, w1p_s, w2b_s):
    i = pl.program_id(0)
    n_s = w1_ref.shape[1] // LANES
    rows = xs_ref.shape[0] // n_s
    f2 = w1_ref.shape[2]

    @pl.when(first_ref[i] == 1)
    def _():
        for c in range(f2 // DEINT):
            cols = slice(c * DEINT, (c + 1) * DEINT)
            strip = w1_ref[0, :, cols].astype(BF16)
            w1p_s[:, cols] = jnp.dot(strip, perm_ref[...], preferred_element_type=F32).astype(BF16)
        w2b_s[...] = w2_ref[0].astype(BF16)

    @pl.when(i < nused_ref[0])
    def _():
        x = jnp.concatenate([xs_ref[pl.ds(s, rows, stride=n_s), :] for s in range(n_s)],
                            axis=1).astype(BF16)
        h = jnp.dot(x, w1p_s[...], preferred_element_type=F32) + b1_ref[0]
        acts = []
        for c in range(f2 // DEINT):
            xg = jnp.minimum(h[:, c * DEINT:c * DEINT + LANES], SWIGLU_LIMIT)
            xl = jnp.clip(h[:, c * DEINT + LANES:(c + 1) * DEINT], -SWIGLU_LIMIT, SWIGLU_LIMIT)
            acts.append((xg * jax.nn.sigmoid(SWIGLU_ALPHA * xg) * (xl + 1.0)).astype(BF16))
        act = jnp.concatenate(acts, axis=1)
        y = jnp.dot(act, w2b_s[...], preferred_element_type=F32) + b2_ref[0]
        for s in range(n_s):
            ys_ref[pl.ds(s, rows, stride=n_s), :] = y[:, s * LANES:(s + 1) * LANES]

    @pl.when(i >= nused_ref[0])
    def _():
        ys_ref[...] = jnp.zeros_like(ys_ref)


def _experts(blk_e, first, n_used, xs, w1, b1p, w2, b2, n_blocks):
    _, lanes = xs.shape
    _, d, f2 = w1.shape
    blk = MOE_ROWS * (d // lanes)
    f = w2.shape[1]
    half = jnp.arange(DEINT) // 2 + (jnp.arange(DEINT) % 2) * LANES
    perm = (half[:, None] == jnp.arange(DEINT)[None, :]).astype(BF16)
    xmap = lambda i, be, fi, nu: (jnp.minimum(i, nu[0] - 1), 0)
    emap = lambda i, be, fi, nu: (be[i], 0, 0)
    grid_spec = pltpu.PrefetchScalarGridSpec(
        num_scalar_prefetch=3,
        grid=(n_blocks,),
        in_specs=[
            pl.BlockSpec((blk, lanes), xmap),
            pl.BlockSpec((1, d, f2), emap),
            pl.BlockSpec((1, 1, f2), emap),
            pl.BlockSpec((1, f, d), emap),
            pl.BlockSpec((1, 1, d), emap),
            pl.BlockSpec((DEINT, DEINT), lambda i, be, fi, nu: (0, 0)),
        ],
        out_specs=pl.BlockSpec((blk, lanes), lambda i, be, fi, nu: (i, 0)),
        scratch_shapes=[pltpu.VMEM((d, f2), BF16), pltpu.VMEM((f, d), BF16)],
    )
    return pl.pallas_call(
        _expert_kernel,
        grid_spec=grid_spec,
        out_shape=jax.ShapeDtypeStruct((n_blocks * blk, lanes), F32),
        compiler_params=_cparams(1),
        name="experts",
    )(blk_e, first, n_used, xs, w1, b1p, w2, b2, perm)


def _combine_kernel(tab_ref, nxt_ref, lpos_ref, gw_ref, x1_ref, gfin_ref, ys_ref, out_ref,
                    buf0_s, buf1_s, acc_s, sem):
    j = pl.program_id(0)
    tm = x1_ref.shape[0] // 2
    n_s = x1_ref.shape[1] // LANES
    per_tile = TOP_K * tm
    tab_w = RUN_TAB

    def gather(tab, tb, buf, slot):
        _for_each_run_piece(
            tab, tb, tm, n_s,
            lambda s, b, r: _row_copy(ys_ref.at[pl.ds(s, r)], buf.at[pl.ds(b, r)],
                                      sem.at[slot]).start())

    def reduce(buf, slot, t0):
        _row_copy(ys_ref.at[pl.ds(0, per_tile * n_s)], buf, sem.at[slot]).wait()
        base = (t0 // tm) * per_tile

        def tokens(g, carry):
            for u in range(COMBINE_UNROLL):
                t = g * COMBINE_UNROLL + u
                acc = None
                for k in range(TOP_K):
                    row = pl.multiple_of(lpos_ref[base + k * tm + t], n_s)
                    term = gw_ref[base + k * tm + t] * buf[pl.ds(row, n_s), :]
                    acc = term if acc is None else acc + term
                acc_s[pl.ds(pl.multiple_of(t * n_s, n_s), n_s), :] = acc
            return carry

        lax.fori_loop(0, tm // COMBINE_UNROLL, tokens, 0)
        y = jnp.concatenate([acc_s[pl.ds(s, tm, stride=n_s), :] for s in range(n_s)], axis=1)
        acc = x1_ref[t0:t0 + tm, :] + y
        ms = jnp.mean(acc * acc, axis=-1, keepdims=True)
        out_ref[t0:t0 + tm, :] = acc * lax.rsqrt(ms + RMS_EPS) * gfin_ref[...]

    @pl.when(j == 0)
    def _():
        gather(tab_ref, 0, buf0_s, 0)

    gather(tab_ref, tab_w, buf1_s, 1)
    reduce(buf0_s, 0, 0)

    @pl.when(j + 1 < pl.num_programs(0))
    def _():
        gather(nxt_ref, 0, buf0_s, 0)

    reduce(buf1_s, 1, tm)


def _combine(tab, lpos_tiles, gw_tiles, x1, g_final, ys):
    m, d = x1.shape
    _, lanes = ys.shape
    s = d // lanes
    tm = COMBINE_TILE
    n_pairs = m // (2 * tm)
    tab_w = RUN_TAB
    smem = pltpu.SMEM
    return pl.pallas_call(
        _combine_kernel,
        grid=(n_pairs,),
        in_specs=[
            pl.BlockSpec((2 * tab_w,), lambda j: (j,), memory_space=smem),
            pl.BlockSpec((tab_w,), lambda j: (jnp.minimum(2 * j + 2, 2 * n_pairs - 1),),
                         memory_space=smem),
            pl.BlockSpec((2 * TOP_K * tm,), lambda j: (j,), memory_space=smem),
            pl.BlockSpec((2 * TOP_K * tm,), lambda j: (j,), memory_space=smem),
            pl.BlockSpec((2 * tm, d), lambda j: (j, 0)),
            pl.BlockSpec((1, d), lambda j: (0, 0)),
            pl.BlockSpec(memory_space=pl.ANY),
        ],
        out_specs=pl.BlockSpec((2 * tm, d), lambda j: (j, 0)),
        out_shape=jax.ShapeDtypeStruct((m, d), F32),
        scratch_shapes=[pltpu.VMEM((TOP_K * tm * s, lanes), F32),
                        pltpu.VMEM((TOP_K * tm * s, lanes), F32),
                        pltpu.VMEM((tm * s, lanes), F32),
                        pltpu.SemaphoreType.DMA((2,))],
        compiler_params=_cparams(1),
        name="combine",
    )(tab, tab, lpos_tiles, gw_tiles, x1, g_final, ys)


def kernel(x, norm_mix_g, w_in, b_gate, pool_w, pool_scale, ssm_lambda_re, ssm_lambda_im, ssm_log_dt, ssm_b_re, ssm_b_im, ssm_c_re, ssm_c_im, ssm_d, ssm_glu_w, ssm_glu_b, w_pool_up, w_ssm_up, w_out, norm_ffn_g, router_w, router_b, moe_w1, moe_b1, moe_w2, moe_b2, norm_final_g):
    bsz, seq, d = x.shape
    depth = w_in.shape[0]
    pw = pool_w.shape[1] * pool_w.shape[2]
    n_groups = ssm_lambda_re.shape[1]
    sw = n_groups * SSM_GROUP_DIM
    m = bsz * seq
    assert depth == 1
    assert bsz == SUBLANES and seq % ROW_TILE == 0 and seq % SSM_TILE == 0 and d % LANES == 0
    assert pool_w.shape[1] == len(POOL_WINDOWS) and n_groups % SSM_PACK == 0
    assert all(w & (w - 1) == 0 and w <= POOL_HALO for w in POOL_WINDOWS)
    assert moe_w1.shape[3] % DEINT == 0

    x2 = x.reshape(m, d)
    for l in range(depth):
        wi = w_in[l].astype(BF16)
        u_pool, u_ssm, gates = _inproj(
            x2, norm_mix_g[l][None], wi[:, :pw], wi[:, pw:pw + sw], wi[:, pw + sw:], b_gate[l][None])
        mats = _ssm_matrices(ssm_lambda_re[l], ssm_lambda_im[l], ssm_log_dt[l], ssm_b_re[l],
                             ssm_b_im[l], ssm_c_re[l], ssm_c_im[l], ssm_d[l])
        y_ssm = _ssm(u_ssm.reshape(bsz, seq, sw), mats).reshape(m, sw)
        x1, h3, idx_t, gw_t, rank_t, cnt = _mix(
            x2, u_pool, y_ssm, gates, pool_w[l].astype(BF16), pool_scale[l][None],
            ssm_glu_w[l].astype(BF16), ssm_glu_b[l][None], w_pool_up[l].astype(BF16),
            w_ssm_up[l].astype(BF16), w_out[l].astype(BF16), norm_ffn_g[l][None],
            router_w[l].T, router_b[l][:, None], seq)

        n_assign = m * TOP_K
        n_blocks = (n_assign + N_EXPERTS * (MOE_ROWS - 1) + MOE_ROWS - 1) // MOE_ROWS
        n_slots = n_blocks * MOE_ROWS
        counts = cnt[:, 0].astype(jnp.int32)
        padded = (counts + MOE_ROWS - 1) // MOE_ROWS * MOE_ROWS
        pad_end = jnp.cumsum(padded)
        pad_start = pad_end - padded
        n_used = (pad_end[-1] // MOE_ROWS).astype(jnp.int32)[None]
        blk_row = jnp.arange(n_blocks, dtype=jnp.int32) * MOE_ROWS
        blk_e = jnp.minimum(jnp.sum(pad_end[None, :] <= blk_row[:, None], axis=1),
                            N_EXPERTS - 1).astype(jnp.int32)
        first = jnp.concatenate([jnp.ones((1,), jnp.int32),
                                 (blk_e[1:] != blk_e[:-1]).astype(jnp.int32)])
        e_ids = jnp.arange(N_EXPERTS, dtype=jnp.int32)[:, None, None]
        zpos = jnp.concatenate([pad_start + counts, n_used]).astype(jnp.int32)
        ct = COMBINE_TILE
        n_ct = m // ct
        cnt_te = jnp.sum((idx_t.reshape(TOP_K, n_ct, ct)[None] == e_ids[..., None]).astype(jnp.int32),
                         axis=(1, 3))
        carry_te = jnp.cumsum(cnt_te, axis=1) - cnt_te
        offs_te = jnp.cumsum(cnt_te, axis=0) - cnt_te
        tab = jnp.stack([cnt_te, offs_te, pad_start[:, None] + carry_te, jnp.zeros_like(cnt_te)],
                        axis=0)
        tab = tab.transpose(2, 0, 1).reshape(-1).astype(jnp.int32)
        delta = jnp.repeat(offs_te - carry_te, ct, axis=1)
        lpos = rank_t + jnp.sum(jnp.where(idx_t[None] == e_ids, delta[:, None, :], 0), axis=0)
        tile_major = lambda a: a.reshape(TOP_K, n_ct, ct).transpose(1, 0, 2).reshape(-1)
        lpos_tiles = tile_major((lpos * (d // LANES)).astype(jnp.int32))

        xs = _dispatch(tab, lpos_tiles, zpos, h3, n_slots, d // LANES)
        f2 = moe_w1.shape[3]
        b1p = (moe_b1[l].reshape(N_EXPERTS, f2 // DEINT, LANES, 2)
               .transpose(0, 1, 3, 2).reshape(N_EXPERTS, 1, f2))
        ys = _experts(blk_e, first, n_used, xs, moe_w1[l], b1p, moe_w2[l], moe_b2[l][:, None, :],
                      n_blocks)
        x2 = _combine(tab, lpos_tiles, tile_major(gw_t), x1, norm_final_g[None], ys)
    return x2.reshape(bsz, seq, d)
```

```python
import functools
import math

import jax
import jax.numpy as jnp
from jax import lax
from jax.experimental import pallas as pl
from jax.experimental.pallas import tpu as pltpu

F32 = jnp.float32
BF16 = jnp.bfloat16

RMS_EPS = 1e-6
POOL_WINDOWS = (2, 4, 8, 16)
POOL_HALO = 16
SSM_GROUP_DIM = 16
SSM_STATE = 64
N_EXPERTS = 32
TOP_K = 4
SWIGLU_ALPHA = 1.702
SWIGLU_LIMIT = 7.0

LANES = 128
SUBLANES = 8
SSM_CHUNK = SUBLANES
SSM_PACK = LANES // SSM_GROUP_DIM
SSM_TILE = 512
DEINT = 2 * LANES
ROW_TILE = 512
COMBINE_TILE = 512
COMBINE_UNROLL = 8
RUN_TAB = 4 * N_EXPERTS
MOE_ROWS = 512
VMEM_LIMIT = 56 * 1024 * 1024


def _cparams(n_axes):
    return pltpu.CompilerParams(
        dimension_semantics=("arbitrary",) * n_axes, vmem_limit_bytes=VMEM_LIMIT)


def _inproj_kernel(x_ref, g_ref, wp_ref, ws_ref, wg_ref, bg_ref, up_ref, us_ref, gate_ref):
    x = x_ref[...]
    ms = jnp.mean(x * x, axis=-1, keepdims=True)
    h = (x * lax.rsqrt(ms + RMS_EPS) * g_ref[...]).astype(BF16)
    up_ref[...] = jnp.dot(h, wp_ref[...], preferred_element_type=F32)
    us_ref[...] = jnp.dot(h, ws_ref[...], preferred_element_type=F32)
    gl = jnp.dot(h, wg_ref[...], preferred_element_type=F32) + bg_ref[...]
    gate_ref[...] = jax.nn.sigmoid(gl).astype(BF16)


def _inproj(x2, g, wp, ws, wg, bg):
    m, d = x2.shape
    pw, sw, gw = wp.shape[1], ws.shape[1], wg.shape[1]
    tm = ROW_TILE
    const = lambda i: (0, 0)
    row = lambda i: (i, 0)
    return pl.pallas_call(
        _inproj_kernel,
        grid=(m // tm,),
        in_specs=[
            pl.BlockSpec((tm, d), row),
            pl.BlockSpec((1, d), const),
            pl.BlockSpec((d, pw), const),
            pl.BlockSpec((d, sw), const),
            pl.BlockSpec((d, gw), const),
            pl.BlockSpec((1, gw), const),
        ],
        out_specs=[
            pl.BlockSpec((tm, pw), row),
            pl.BlockSpec((tm, sw), row),
            pl.BlockSpec((tm, gw), row),
        ],
        out_shape=[
            jax.ShapeDtypeStruct((m, pw), F32),
            jax.ShapeDtypeStruct((m, sw), F32),
            jax.ShapeDtypeStruct((m, gw), BF16),
        ],
        compiler_params=_cparams(1),
        name="inproj",
    )(x2, g, wp, ws, wg, bg)


def _ssm_matrices(lam_re, lam_im, log_dt, b_re, b_im, c_re, c_im, d_skip):
    hi = lax.Precision.HIGHEST
    L, P, N, GP = SSM_CHUNK, SSM_GROUP_DIM, SSM_STATE, SSM_PACK
    G = lam_re.shape[0]
    K = G // GP
    lr, li = lam_re.astype(F32), lam_im.astype(F32)
    dt = jnp.exp(log_dt.astype(F32))[:, None]
    mag = jnp.exp(lr * dt)
    lb_re, lb_im = mag * jnp.cos(li * dt), mag * jnp.sin(li * dt)
    den = lr * lr + li * li
    xr, xi = lb_re - 1.0, lb_im
    f_re = (xr * lr + xi * li) / den
    f_im = (xi * lr - xr * li) / den
    br, bi = b_re.astype(F32), b_im.astype(F32)
    bb_re = f_re[..., None] * br - f_im[..., None] * bi
    bb_im = f_re[..., None] * bi + f_im[..., None] * br
    tau = jnp.arange(L + 1, dtype=F32)[None, :, None]
    pmag = jnp.exp(lr[:, None, :] * dt[:, None, :] * tau)
    pang = li[:, None, :] * dt[:, None, :] * tau
    pw_re, pw_im = pmag * jnp.cos(pang), pmag * jnp.sin(pang)
    ab_re = pw_re[:, :L, :, None] * bb_re[:, None] - pw_im[:, :L, :, None] * bb_im[:, None]
    ab_im = pw_re[:, :L, :, None] * bb_im[:, None] + pw_im[:, :L, :, None] * bb_re[:, None]
    cr, ci = c_re.astype(F32), c_im.astype(F32)
    kern = (jnp.einsum('gqn,glnp->glqp', cr, ab_re, precision=hi)
            - jnp.einsum('gqn,glnp->glqp', ci, ab_im, precision=hi))
    eye = jnp.eye(GP, dtype=F32)
    kern_p = jnp.einsum('kilqp,ih->kiplhq', kern.reshape(K, GP, L, P, P), eye, precision=hi)
    kern_p = kern_p.reshape(K, GP * P, L, GP * P)
    kern_p = jnp.concatenate([jnp.zeros_like(kern_p), kern_p], axis=2)
    toep = jnp.stack([kern_p[:, :, L - j:2 * L - j, :].reshape(K, GP * P, L * GP * P)
                      for j in range(L)], axis=1)
    toep = toep.reshape(K, L * GP * P, L * GP * P)
    w_re = jnp.einsum('kijnp,ih->kjiphn', ab_re[:, ::-1].reshape(K, GP, L, N, P), eye, precision=hi)
    w_im = jnp.einsum('kijnp,ih->kjiphn', ab_im[:, ::-1].reshape(K, GP, L, N, P), eye, precision=hi)
    w = jnp.concatenate([w_re.reshape(K, L * GP * P, GP * N), w_im.reshape(K, L * GP * P, GP * N)], axis=2)
    p1_re, p1_im = pw_re[:, 1:], pw_im[:, 1:]
    v_re = cr[:, None] * p1_re[:, :, None, :] - ci[:, None] * p1_im[:, :, None, :]
    v_im = -(cr[:, None] * p1_im[:, :, None, :] + ci[:, None] * p1_re[:, :, None, :])
    v_re = jnp.einsum('kitqn,ih->kinthq', v_re.reshape(K, GP, L, P, N), eye, precision=hi)
    v_im = jnp.einsum('kitqn,ih->kinthq', v_im.reshape(K, GP, L, P, N), eye, precision=hi)
    v = jnp.concatenate([v_re.reshape(K, GP * N, L * GP * P), v_im.reshape(K, GP * N, L * GP * P)], axis=1)
    al = jnp.stack([pw_re[:, L].reshape(K, GP * N), pw_im[:, L].reshape(K, GP * N)], axis=1)
    dvec = jnp.tile(d_skip.astype(F32).reshape(K, 1, GP * P), (1, L, 1)).reshape(K, 1, L * GP * P)
    return toep.astype(BF16), w.astype(BF16), v.astype(BF16), al, dvec


def _ssm_kernel(u_ref, toep_ref, w_ref, v_ref, al_ref, d_ref, y_ref, lhs_s, e_s, s_s, carry_s):
    bsz, tt, lanes = u_ref.shape
    L = SSM_CHUNK
    n_chunks = tt // L
    ns = al_ref.shape[2]

    @pl.when(pl.program_id(1) == 0)
    def _():
        carry_s[...] = jnp.zeros_like(carry_s)

    for b in range(bsz):
        for t in range(L):
            lhs_s[t, pl.ds(b, n_chunks, stride=bsz), :] = u_ref[b, pl.ds(t, n_chunks, stride=L), :]
    lhs = jnp.concatenate([lhs_s[t] for t in range(L)], axis=1)
    lb = lhs.astype(BF16)
    e_s[...] = jnp.dot(lb, w_ref[0], preferred_element_type=F32)
    ar = jnp.broadcast_to(al_ref[0, 0:1, :], (bsz, ns))
    ai = jnp.broadcast_to(al_ref[0, 1:2, :], (bsz, ns))
    sr = carry_s[0]
    si = carry_s[1]
    for c in range(n_chunks):
        rows = slice(c * bsz, (c + 1) * bsz)
        s_s[rows, :ns] = sr
        s_s[rows, ns:] = si
        er = e_s[rows, :ns]
        ei = e_s[rows, ns:]
        sr, si = ar * sr - ai * si + er, ar * si + ai * sr + ei
    carry_s[0] = sr
    carry_s[1] = si
    y = (jnp.dot(lb, toep_ref[0], preferred_element_type=F32)
         + jnp.dot(s_s[...].astype(BF16), v_ref[0], preferred_element_type=F32)
         + d_ref[0] * lhs)
    yg = jax.nn.gelu(y)
    for t in range(L):
        lhs_s[t] = yg[:, t * lanes:(t + 1) * lanes]
    for b in range(bsz):
        for t in range(L):
            y_ref[b, pl.ds(t, n_chunks, stride=L), :] = lhs_s[t, pl.ds(b, n_chunks, stride=bsz), :]


def _ssm(u3, mats):
    toep, w, v, al, dvec = mats
    bsz, seq, sw = u3.shape
    k = toep.shape[0]
    lanes = sw // k
    tt = SSM_TILE
    rows = bsz * tt // SSM_CHUNK
    cl = SSM_CHUNK * lanes
    ns = al.shape[2]
    pack = lambda p, t: (p, 0, 0)
    tile = lambda p, t: (0, t, p)
    return pl.pallas_call(
        _ssm_kernel,
        grid=(k, seq // tt),
        in_specs=[
            pl.BlockSpec((bsz, tt, lanes), tile),
            pl.BlockSpec((1, cl, cl), pack),
            pl.BlockSpec((1, cl, 2 * ns), pack),
            pl.BlockSpec((1, 2 * ns, cl), pack),
            pl.BlockSpec((1, 2, ns), pack),
            pl.BlockSpec((1, 1, cl), pack),
        ],
        out_specs=pl.BlockSpec((bsz, tt, lanes), tile),
        out_shape=jax.ShapeDtypeStruct((bsz, seq, sw), F32),
        scratch_shapes=[
            pltpu.VMEM((SSM_CHUNK, rows, lanes), F32),
            pltpu.VMEM((rows, 2 * ns), F32),
            pltpu.VMEM((rows, 2 * ns), F32),
            pltpu.VMEM((2, bsz, ns), F32),
        ],
        compiler_params=_cparams(2),
        name="ssm",
    )(u3, toep, w, v, al, dvec)


def _mix_kernel(tiles_per_seq, x_ref, up_ref, halo_ref, ys_ref, gate_ref, poolw_ref, pscale_ref,
                gluw_ref, glub_ref, wpu_ref, wsu_ref, wout_ref, gffn_ref, rwh_ref, rwl_ref, rb_ref,
                tri_ref,
                x1_ref, h3_ref, idx_ref, gw_ref, rank_ref, cnt_ref, ext_s, carry_s):
    i = pl.program_id(0)
    j = i % tiles_per_seq
    tm = x_ref.shape[0]
    d_model = x_ref.shape[1]
    gdim = poolw_ref.shape[1]

    @pl.when(i == 0)
    def _():
        carry_s[...] = jnp.zeros_like(carry_s)

    ext_s[0:POOL_HALO, :] = jnp.where(j == 0, 0.0, halo_ref[...])
    ext_s[POOL_HALO:, :] = up_ref[...]
    pos = (j * tm + 1 + lax.broadcasted_iota(jnp.int32, (tm, 1), 0)).astype(F32)
    parts = []
    for g, w in enumerate(POOL_WINDOWS):
        cols = slice(g * gdim, (g + 1) * gdim)
        s = ext_s[:, cols]
        k = 1
        while k < w:
            s = s + pltpu.roll(s, k, 0)
            k *= 2
        s = s[POOL_HALO:POOL_HALO + tm]
        cur = ext_s[POOL_HALO:POOL_HALO + tm, cols]
        dlt = (s / jnp.minimum(pos, float(w)) - cur).astype(BF16)
        parts.append(jnp.dot(dlt, poolw_ref[g], preferred_element_type=F32))
    yp = jnp.concatenate(parts, axis=1) * pscale_ref[...]
    y_pool = jnp.dot(yp.astype(BF16), wpu_ref[...], preferred_element_type=F32)

    yg = ys_ref[...].astype(BF16)
    lin = jnp.dot(yg, gluw_ref[...], preferred_element_type=F32) + glub_ref[...]
    glu = yg * jax.nn.sigmoid(lin.astype(BF16))
    y_ssm = jnp.dot(glu, wsu_ref[...], preferred_element_type=F32)

    z = (gate_ref[:, :d_model] * y_pool.astype(BF16) + gate_ref[:, d_model:] * y_ssm.astype(BF16))
    x1 = x_ref[...] + jnp.dot(z, wout_ref[...], preferred_element_type=F32)
    x1_ref[...] = x1

    ms = jnp.mean(x1 * x1, axis=-1, keepdims=True)
    h2 = x1 * lax.rsqrt(ms + RMS_EPS) * gffn_ref[...]
    n_s = d_model // LANES
    for s in range(n_s):
        h3_ref[pl.ds(s, tm, stride=n_s), :] = h2[:, s * LANES:(s + 1) * LANES]

    h_hi = h2.astype(BF16)
    h_lo = (h2 - h_hi.astype(F32)).astype(BF16)
    nt = (((1,), (1,)), ((), ()))
    logits = (lax.dot_general(rwh_ref[...], h_hi, nt, preferred_element_type=F32)
              + lax.dot_general(rwh_ref[...], h_lo, nt, preferred_element_type=F32)
              + lax.dot_general(rwl_ref[...], h_hi, nt, preferred_element_type=F32)
              + rb_ref[...])
    n_e = logits.shape[0]
    iota_e = lax.broadcasted_iota(jnp.int32, (n_e, tm), 0)
    l = logits
    tops, hots = [], []
    for k in range(TOP_K):
        m = jnp.max(l, axis=0, keepdims=True)
        idx = jnp.min(jnp.where(l == m, iota_e, n_e), axis=0, keepdims=True)
        hot = iota_e == idx
        l = jnp.where(hot, -jnp.inf, l)
        tops.append(m)
        hots.append(hot)
        idx_ref[k:k + 1, :] = idx
    exps = [jnp.exp(m - tops[0]) for m in tops]
    den = exps[0] + exps[1] + exps[2] + exps[3]
    for k in range(TOP_K):
        gw_ref[k:k + 1, :] = exps[k] / den

    multi = sum(h.astype(F32) for h in hots)
    cum = jnp.dot(multi.astype(BF16), tri_ref[...], preferred_element_type=F32) + carry_s[:, 0:1]
    for k in range(TOP_K):
        rk = jnp.sum(jnp.where(hots[k], cum, 0.0), axis=0, keepdims=True)
        rank_ref[k:k + 1, :] = rk.astype(jnp.int32)
    carry_s[...] = carry_s[...] + jnp.sum(multi, axis=1, keepdims=True)
    cnt_ref[...] = carry_s[...]


def _mix(x2, u_pool, y_ssm, gates, pool_w, pool_scale, glu_w, glu_b, w_pool_up, w_ssm_up, w_out,
         g_ffn, rw_t, rb, seq_len):
    m, d = x2.shape
    pw = u_pool.shape[1]
    sw = y_ssm.shape[1]
    tm = ROW_TILE
    n_e = rw_t.shape[0]
    rw_hi = rw_t.astype(BF16)
    rw_lo = (rw_t - rw_hi.astype(F32)).astype(BF16)
    tiles_per_seq = seq_len // tm
    tri = (jnp.arange(tm)[:, None] < jnp.arange(tm)[None, :]).astype(BF16)
    row = lambda i: (i, 0)
    const2 = lambda i: (0, 0)
    const3 = lambda i: (0, 0, 0)
    col = lambda i: (0, i)
    halo = lambda i: (jnp.maximum(i * (tm // POOL_HALO) - 1, 0), 0)
    return pl.pallas_call(
        functools.partial(_mix_kernel, tiles_per_seq),
        grid=(m // tm,),
        in_specs=[
            pl.BlockSpec((tm, d), row),
            pl.BlockSpec((tm, pw), row),
            pl.BlockSpec((POOL_HALO, pw), halo),
            pl.BlockSpec((tm, sw), row),
            pl.BlockSpec((tm, 2 * d), row),
            pl.BlockSpec(pool_w.shape, const3),
            pl.BlockSpec((1, pw), const2),
            pl.BlockSpec((sw, sw), const2),
            pl.BlockSpec((1, sw), const2),
            pl.BlockSpec((pw, d), const2),
            pl.BlockSpec((sw, d), const2),
            pl.BlockSpec((d, d), const2),
            pl.BlockSpec((1, d), const2),
            pl.BlockSpec((n_e, d), const2),
            pl.BlockSpec((n_e, d), const2),
            pl.BlockSpec((n_e, 1), const2),
            pl.BlockSpec((tm, tm), const2),
        ],
        out_specs=[
            pl.BlockSpec((tm, d), row),
            pl.BlockSpec((tm * (d // LANES), LANES), row),
            pl.BlockSpec((TOP_K, tm), col),
            pl.BlockSpec((TOP_K, tm), col),
            pl.BlockSpec((TOP_K, tm), col),
            pl.BlockSpec((n_e, LANES), const2),
        ],
        out_shape=[
            jax.ShapeDtypeStruct((m, d), F32),
            jax.ShapeDtypeStruct((m * (d // LANES), LANES), F32),
            jax.ShapeDtypeStruct((TOP_K, m), jnp.int32),
            jax.ShapeDtypeStruct((TOP_K, m), F32),
            jax.ShapeDtypeStruct((TOP_K, m), jnp.int32),
            jax.ShapeDtypeStruct((n_e, LANES), F32),
        ],
        scratch_shapes=[
            pltpu.VMEM((POOL_HALO + tm, pw), F32),
            pltpu.VMEM((n_e, LANES), F32),
        ],
        compiler_params=_cparams(1),
        name="mix_route",
    )(x2, u_pool, u_pool, y_ssm, gates, pool_w, pool_scale, glu_w, glu_b, w_pool_up, w_ssm_up,
      w_out, g_ffn, rw_hi, rw_lo, rb, tri)


def _row_copy(src, dst, sem):
    return pltpu.make_async_copy(src, dst, sem)


def _for_each_run_piece(tab, tb, max_rows, n_s, fn):
    def per_expert(e, carry):
        n = tab[tb + e]
        off = tab[tb + N_EXPERTS + e]
        slot = tab[tb + 2 * N_EXPERTS + e]
        def piece(b):
            size = 1 << b

            @pl.when(((n >> b) & 1) == 1)
            def _():
                done = n & (size - 1)
                fn(pl.multiple_of((slot + done) * n_s, n_s), pl.multiple_of((off + done) * n_s, n_s),
                   size * n_s)

        n_bits = max_rows.bit_length()
        low_bits = min(n_bits, (2 * TOP_K * max_rows // N_EXPERTS - 1).bit_length())
        for b in range(low_bits):
            piece(b)

        @pl.when((n >> low_bits) != 0)
        def _():
            for b in range(low_bits, n_bits):
                piece(b)
        return carry

    lax.fori_loop(0, N_EXPERTS, per_expert, 0)


def _dispatch_kernel(n_s, tab_ref, lpos_ref, zpos_ref, h3_ref, xs_ref, st0_s, st1_s, sem):
    j = pl.program_id(0)
    tm = h3_ref.shape[0] // (2 * n_s)
    per_tile = TOP_K * tm
    zrows = MOE_ROWS * n_s

    @pl.when(j == 0)
    def _():
        st0_s[0:zrows, :] = jnp.zeros((zrows, st0_s.shape[1]), F32)

        def fill(start):
            cp = _row_copy(st0_s.at[pl.ds(0, zrows)],
                           xs_ref.at[pl.ds(pl.multiple_of(start * n_s, n_s), zrows)], sem.at[0])
            cp.start()
            cp.wait()

        for e in range(N_EXPERTS):
            fill(zpos_ref[e])

        def tail(t, carry):
            fill(t * MOE_ROWS)
            return carry

        lax.fori_loop(zpos_ref[N_EXPERTS], xs_ref.shape[0] // zrows, tail, 0)

    def drain(stage, slot):
        _row_copy(stage, xs_ref.at[pl.ds(0, per_tile * n_s)], sem.at[slot]).wait()

    def scatter(stage, slot):
        @pl.when(j > 0)
        def _():
            drain(stage, slot)

        base = slot * per_tile

        def tokens(g, carry):
            for u in range(COMBINE_UNROLL):
                t = g * COMBINE_UNROLL + u
                rows = h3_ref[pl.ds(pl.multiple_of((slot * tm + t) * n_s, n_s), n_s), :]
                for k in range(TOP_K):
                    row = pl.multiple_of(lpos_ref[base + k * tm + t], n_s)
                    stage[pl.ds(row, n_s), :] = rows
            return carry

        lax.fori_loop(0, tm // COMBINE_UNROLL, tokens, 0)
        _for_each_run_piece(
            tab_ref, slot * RUN_TAB, tm, n_s,
            lambda s, b, r: _row_copy(stage.at[pl.ds(b, r)], xs_ref.at[pl.ds(s, r)],
                                      sem.at[slot]).start())

    scatter(st0_s, 0)
    scatter(st1_s, 1)

    @pl.when(j + 1 == pl.num_programs(0))
    def _():
        drain(st0_s, 0)
        drain(st1_s, 1)


def _dispatch(tab, lpos_tiles, zpos, h3, n_slots, n_s):
    rows, lanes = h3.shape
    tm = COMBINE_TILE
    smem = pltpu.SMEM
    return pl.pallas_call(
        functools.partial(_dispatch_kernel, n_s),
        grid=(rows // (2 * tm * n_s),),
        in_specs=[
            pl.BlockSpec((2 * RUN_TAB,), lambda j: (j,), memory_space=smem),
            pl.BlockSpec((2 * TOP_K * tm,), lambda j: (j,), memory_space=smem),
            pl.BlockSpec(memory_space=smem),
            pl.BlockSpec((2 * tm * n_s, lanes), lambda j: (j, 0)),
        ],
        out_specs=pl.BlockSpec(memory_space=pl.ANY),
        out_shape=jax.ShapeDtypeStruct(((n_slots + MOE_ROWS) * n_s, lanes), F32),
        scratch_shapes=[pltpu.VMEM((TOP_K * tm * n_s, lanes), F32),
                        pltpu.VMEM((TOP_K * tm * n_s, lanes), F32),
                        pltpu.SemaphoreType.DMA((2,))],
        compiler_params=_cparams(1),
        name="dispatch",
    )(tab, lpos_tiles, zpos, h3)


def _expert_kernel(blk_e_ref, first_ref, nused_ref, xs_ref, w1_ref, b1_ref, w2_ref, b2_ref, perm_ref,
                   ys_ref, w1p_s, w2b_s):
    i = pl.program_id(0)
    n_s = w1_ref.shape[1] // LANES
    rows = xs_ref.shape[0] // n_s
    f2 = w1_ref.shape[2]

    @pl.when(first_ref[i] == 1)
    def _():
        for c in range(f2 // DEINT):
            cols = slice(c * DEINT, (c + 1) * DEINT)
            strip = w1_ref[0, :, cols].astype(BF16)
            w1p_s[:, cols] = jnp.dot(strip, perm_ref[...], preferred_element_type=F32).astype(BF16)
        w2b_s[...] = w2_ref[0].astype(BF16)

    @pl.when(i < nused_ref[0])
    def _():
        x = jnp.concatenate([xs_ref[pl.ds(s, rows, stride=n_s), :] for s in range(n_s)],
                            axis=1).astype(BF16)
        h = jnp.dot(x, w1p_s[...], preferred_element_type=F32) + b1_ref[0]
        acts = []
        for c in range(f2 // DEINT):
            xg = jnp.minimum(h[:, c * DEINT:c * DEINT + LANES], SWIGLU_LIMIT)
            xl = jnp.clip(h[:, c * DEINT + LANES:(c + 1) * DEINT], -SWIGLU_LIMIT, SWIGLU_LIMIT)
            acts.append((xg * jax.nn.sigmoid(SWIGLU_ALPHA * xg) * (xl + 1.0)).astype(BF16))
        act = jnp.concatenate(acts, axis=1)
        y = jnp.dot(act, w2b_s[...], preferred_element_type=F32) + b2_ref[0]
        for s in range(n_s):
            ys_ref[pl.ds(s, rows, stride=n_s), :] = y[:, s * LANES:(s + 1) * LANES]

    @pl.when(i >= nused_ref[0])
    def _():
        ys_ref[...] = jnp.zeros_like(ys_ref)


def _experts(blk_e, first, n_used, xs, w1, b1p, w2, b2, n_blocks):
    _, lanes = xs.shape
    _, d, f2 = w1.shape
    blk = MOE_ROWS * (d // lanes)
    f = w2.shape[1]
    half = jnp.arange(DEINT) // 2 + (jnp.arange(DEINT) % 2) * LANES
    perm = (half[:, None] == jnp.arange(DEINT)[None, :]).astype(BF16)
    xmap = lambda i, be, fi, nu: (jnp.minimum(i, nu[0] - 1), 0)
    emap = lambda i, be, fi, nu: (be[i], 0, 0)
    grid_spec = pltpu.PrefetchScalarGridSpec(
        num_scalar_prefetch=3,
        grid=(n_blocks,),
        in_specs=[
            pl.BlockSpec((blk, lanes), xmap),
            pl.BlockSpec((1, d, f2), emap),
            pl.BlockSpec((1, 1, f2), emap),
            pl.BlockSpec((1, f, d), emap),
            pl.BlockSpec((1, 1, d), emap),
            pl.BlockSpec((DEINT, DEINT), lambda i, be, fi, nu: (0, 0)),
        ],
        out_specs=pl.BlockSpec((blk, lanes), lambda i, be, fi, nu: (i, 0)),
        scratch_shapes=[pltpu.VMEM((d, f2), BF16), pltpu.VMEM((f, d), BF16)],
    )
    return pl.pallas_call(
        _expert_kernel,
        grid_spec=grid_spec,
        out_shape=jax.ShapeDtypeStruct((n_blocks * blk, lanes), F32),
        compiler_params=_cparams(1),
        name="experts",
    )(blk_e, first, n_used, xs, w1, b1p, w2, b2, perm)


def _combine_kernel(tab_ref, nxt_ref, lpos_ref, gw_ref, x1_ref, gfin_ref, ys_ref, out_ref,
                    buf0_s, buf1_s, acc_s, sem):
    j = pl.program_id(0)
    tm = x1_ref.shape[0] // 2
    n_s = x1_ref.shape[1] // LANES
    per_tile = TOP_K * tm
    tab_w = RUN_TAB

    def gather(tab, tb, buf, slot):
        _for_each_run_piece(
            tab, tb, tm, n_s,
            lambda s, b, r: _row_copy(ys_ref.at[pl.ds(s, r)], buf.at[pl.ds(b, r)],
                                      sem.at[slot]).start())

    def reduce(buf, slot, t0):
        _row_copy(ys_ref.at[pl.ds(0, per_tile * n_s)], buf, sem.at[slot]).wait()
        base = (t0 // tm) * per_tile

        def tokens(g, carry):
            for u in range(COMBINE_UNROLL):
                t = g * COMBINE_UNROLL + u
                acc = None
                for k in range(TOP_K):
                    row = pl.multiple_of(lpos_ref[base + k * tm + t], n_s)
                    term = gw_ref[base + k * tm + t] * buf[pl.ds(row, n_s), :]
                    acc = term if acc is None else acc + term
                acc_s[pl.ds(pl.multiple_of(t * n_s, n_s), n_s), :] = acc
            return carry

        lax.fori_loop(0, tm // COMBINE_UNROLL, tokens, 0)
        y = jnp.concatenate([acc_s[pl.ds(s, tm, stride=n_s), :] for s in range(n_s)], axis=1)
        acc = x1_ref[t0:t0 + tm, :] + y
        ms = jnp.mean(acc * acc, axis=-1, keepdims=True)
        out_ref[t0:t0 + tm, :] = acc * lax.rsqrt(ms + RMS_EPS) * gfin_ref[...]

    @pl.when(j == 0)
    def _():
        gather(tab_ref, 0, buf0_s, 0)

    gather(tab_ref, tab_w, buf1_s, 1)
    reduce(buf0_s, 0, 0)

    @pl.when(j + 1 < pl.num_programs(0))
    def _():
        gather(nxt_ref, 0, buf0_s, 0)

    reduce(buf1_s, 1, tm)


def _combine(tab, lpos_tiles, gw_tiles, x1, g_final, ys):
    m, d = x1.shape
    _, lanes = ys.shape
    s = d // lanes
    tm = COMBINE_TILE
    n_pairs = m // (2 * tm)
    tab_w = RUN_TAB
    smem = pltpu.SMEM
    return pl.pallas_call(
        _combine_kernel,
        grid=(n_pairs,),
        in_specs=[
            pl.BlockSpec((2 * tab_w,), lambda j: (j,), memory_space=smem),
            pl.BlockSpec((tab_w,), lambda j: (jnp.minimum(2 * j + 2, 2 * n_pairs - 1),),
                         memory_space=smem),
            pl.BlockSpec((2 * TOP_K * tm,), lambda j: (j,), memory_space=smem),
            pl.BlockSpec((2 * TOP_K * tm,), lambda j: (j,), memory_space=smem),
            pl.BlockSpec((2 * tm, d), lambda j: (j, 0)),
            pl.BlockSpec((1, d), lambda j: (0, 0)),
            pl.BlockSpec(memory_space=pl.ANY),
        ],
        out_specs=pl.BlockSpec((2 * tm, d), lambda j: (j, 0)),
        out_shape=jax.ShapeDtypeStruct((m, d), F32),
        scratch_shapes=[pltpu.VMEM((TOP_K * tm * s, lanes), F32),
                        pltpu.VMEM((TOP_K * tm * s, lanes), F32),
                        pltpu.VMEM((tm * s, lanes), F32),
                        pltpu.SemaphoreType.DMA((2,))],
        compiler_params=_cparams(1),
        name="combine",
    )(tab, tab, lpos_tiles, gw_tiles, x1, g_final, ys)


def kernel(x, norm_mix_g, w_in, b_gate, pool_w, pool_scale, ssm_lambda_re, ssm_lambda_im, ssm_log_dt, ssm_b_re, ssm_b_im, ssm_c_re, ssm_c_im, ssm_d, ssm_glu_w, ssm_glu_b, w_pool_up, w_ssm_up, w_out, norm_ffn_g, router_w, router_b, moe_w1, moe_b1, moe_w2, moe_b2, norm_final_g):
    bsz, seq, d = x.shape
    depth = w_in.shape[0]
    pw = pool_w.shape[1] * pool_w.shape[2]
    n_groups = ssm_lambda_re.shape[1]
    sw = n_groups * SSM_GROUP_DIM
    m = bsz * seq
    assert depth == 1
    assert bsz == SUBLANES and seq % ROW_TILE == 0 and seq % SSM_TILE == 0 and d % LANES == 0
    assert pool_w.shape[1] == len(POOL_WINDOWS) and n_groups % SSM_PACK == 0
    assert all(w & (w - 1) == 0 and w <= POOL_HALO for w in POOL_WINDOWS)
    assert moe_w1.shape[3] % DEINT == 0

    x2 = x.reshape(m, d)
    for l in range(depth):
        wi = w_in[l].astype(BF16)
        u_pool, u_ssm, gates = _inproj(
            x2, norm_mix_g[l][None], wi[:, :pw], wi[:, pw:pw + sw], wi[:, pw + sw:], b_gate[l][None])
        mats = _ssm_matrices(ssm_lambda_re[l], ssm_lambda_im[l], ssm_log_dt[l], ssm_b_re[l],
                             ssm_b_im[l], ssm_c_re[l], ssm_c_im[l], ssm_d[l])
        y_ssm = _ssm(u_ssm.reshape(bsz, seq, sw), mats).reshape(m, sw)
        x1, h3, idx_t, gw_t, rank_t, cnt = _mix(
            x2, u_pool, y_ssm, gates, pool_w[l].astype(BF16), pool_scale[l][None],
            ssm_glu_w[l].astype(BF16), ssm_glu_b[l][None], w_pool_up[l].astype(BF16),
            w_ssm_up[l].astype(BF16), w_out[l].astype(BF16), norm_ffn_g[l][None],
            router_w[l].T, router_b[l][:, None], seq)

        n_assign = m * TOP_K
        n_blocks = (n_assign + N_EXPERTS * (MOE_ROWS - 1) + MOE_ROWS - 1) // MOE_ROWS
        n_slots = n_blocks * MOE_ROWS
        counts = cnt[:, 0].astype(jnp.int32)
        padded = (counts + MOE_ROWS - 1) // MOE_ROWS * MOE_ROWS
        pad_end = jnp.cumsum(padded)
        pad_start = pad_end - padded
        n_used = (pad_end[-1] // MOE_ROWS).astype(jnp.int32)[None]
        blk_row = jnp.arange(n_blocks, dtype=jnp.int32) * MOE_ROWS
        blk_e = jnp.minimum(jnp.sum(pad_end[None, :] <= blk_row[:, None], axis=1),
                            N_EXPERTS - 1).astype(jnp.int32)
        first = jnp.concatenate([jnp.ones((1,), jnp.int32),
                                 (blk_e[1:] != blk_e[:-1]).astype(jnp.int32)])
        e_ids = jnp.arange(N_EXPERTS, dtype=jnp.int32)[:, None, None]
        zpos = jnp.concatenate([pad_start + counts, n_used]).astype(jnp.int32)
        ct = COMBINE_TILE
        n_ct = m // ct
        cnt_te = jnp.sum((idx_t.reshape(TOP_K, n_ct, ct)[None] == e_ids[..., None]).astype(jnp.int32),
                         axis=(1, 3))
        carry_te = jnp.cumsum(cnt_te, axis=1) - cnt_te
        offs_te = jnp.cumsum(cnt_te, axis=0) - cnt_te
        tab = jnp.stack([cnt_te, offs_te, pad_start[:, None] + carry_te, jnp.zeros_like(cnt_te)],
                        axis=0)
        tab = tab.transpose(2, 0, 1).reshape(-1).astype(jnp.int32)
        delta = jnp.repeat(offs_te - carry_te, ct, axis=1)
        lpos = rank_t + jnp.sum(jnp.where(idx_t[None] == e_ids, delta[:, None, :], 0), axis=0)
        tile_major = lambda a: a.reshape(TOP_K, n_ct, ct).transpose(1, 0, 2).reshape(-1)
        lpos_tiles = tile_major((lpos * (d // LANES)).astype(jnp.int32))

        xs = _dispatch(tab, lpos_tiles, zpos, h3, n_slots, d // LANES)
        f2 = moe_w1.shape[3]
        b1p = (moe_b1[l].reshape(N_EXPERTS, f2 // DEINT, LANES, 2)
               .transpose(0, 1, 3, 2).reshape(N_EXPERTS, 1, f2))
        ys = _experts(blk_e, first, n_used, xs, moe_w1[l], b1p, moe_w2[l], moe_b2[l][:, None, :],
                      n_blocks)
        x2 = _combine(tab, lpos_tiles, tile_major(gw_t), x1, norm_final_g[None], ys)
    return x2.reshape(bsz, seq, d)
```

```python
import functools
import math

import jax
import jax.numpy as jnp
from jax import lax
from jax.experimental import pallas as pl
from jax.experimental.pallas import tpu as pltpu

F32 = jnp.float32
BF16 = jnp.bfloat16

RMS_EPS = 1e-6
POOL_WINDOWS = (2, 4, 8, 16)
POOL_HALO = 16
SSM_GROUP_DIM = 16
SSM_STATE = 64
N_EXPERTS = 32
TOP_K = 4
SWIGLU_ALPHA = 1.702
SWIGLU_LIMIT = 7.0

LANES = 128
SUBLANES = 8
SSM_CHUNK = SUBLANES
SSM_PACK = LANES // SSM_GROUP_DIM
SSM_TILE = 512
DEINT = 2 * LANES
ROW_TILE = 512
INPROJ_TILE = 1024
COMBINE_TILE = 512
COMBINE_UNROLL = 8
RUN_TAB = 4 * N_EXPERTS
MOE_ROWS = 512
VMEM_LIMIT = 56 * 1024 * 1024


def _cparams(n_axes):
    return pltpu.CompilerParams(
        dimension_semantics=("arbitrary",) * n_axes, vmem_limit_bytes=VMEM_LIMIT)


def _inproj_kernel(x_ref, g_ref, wp_ref, ws_ref, wg_ref, bg_ref, up_ref, us_ref, gate_ref):
    x = x_ref[...]
    ms = jnp.mean(x * x, axis=-1, keepdims=True)
    h = (x * lax.rsqrt(ms + RMS_EPS) * g_ref[...]).astype(BF16)
    up_ref[...] = jnp.dot(h, wp_ref[...], preferred_element_type=F32)
    us_ref[...] = jnp.dot(h, ws_ref[...], preferred_element_type=F32)
    gl = jnp.dot(h, wg_ref[...], preferred_element_type=F32) + bg_ref[...]
    gate_ref[...] = jax.nn.sigmoid(gl).astype(BF16)


def _inproj(x2, g, wp, ws, wg, bg):
    m, d = x2.shape
    pw, sw, gw = wp.shape[1], ws.shape[1], wg.shape[1]
    tm = INPROJ_TILE
    const = lambda i: (0, 0)
    row = lambda i: (i, 0)
    return pl.pallas_call(
        _inproj_kernel,
        grid=(m // tm,),
        in_specs=[
            pl.BlockSpec((tm, d), row),
            pl.BlockSpec((1, d), const),
            pl.BlockSpec((d, pw), const),
            pl.BlockSpec((d, sw), const),
            pl.BlockSpec((d, gw), const),
            pl.BlockSpec((1, gw), const),
        ],
        out_specs=[
            pl.BlockSpec((tm, pw), row),
            pl.BlockSpec((tm, sw), row),
            pl.BlockSpec((tm, gw), row),
        ],
        out_shape=[
            jax.ShapeDtypeStruct((m, pw), F32),
            jax.ShapeDtypeStruct((m, sw), F32),
            jax.ShapeDtypeStruct((m, gw), BF16),
        ],
        compiler_params=_cparams(1),
        name="inproj",
    )(x2, g, wp, ws, wg, bg)


def _ssm_matrices(lam_re, lam_im, log_dt, b_re, b_im, c_re, c_im, d_skip):
    hi = lax.Precision.HIGHEST
    L, P, N, GP = SSM_CHUNK, SSM_GROUP_DIM, SSM_STATE, SSM_PACK
    G = lam_re.shape[0]
    K = G // GP
    lr, li = lam_re.astype(F32), lam_im.astype(F32)
    dt = jnp.exp(log_dt.astype(F32))[:, None]
    mag = jnp.exp(lr * dt)
    lb_re, lb_im = mag * jnp.cos(li * dt), mag * jnp.sin(li * dt)
    den = lr * lr + li * li
    xr, xi = lb_re - 1.0, lb_im
    f_re = (xr * lr + xi * li) / den
    f_im = (xi * lr - xr * li) / den
    br, bi = b_re.astype(F32), b_im.astype(F32)
    bb_re = f_re[..., None] * br - f_im[..., None] * bi
    bb_im = f_re[..., None] * bi + f_im[..., None] * br
    n_c, n_s = GP * P, GP * N
    tau = jnp.arange(L + 1, dtype=F32)[None, :, None]
    pmag = jnp.exp((lr * dt).reshape(K, 1, n_s) * tau)
    pang = (li * dt).reshape(K, 1, n_s) * tau
    pw_re, pw_im = pmag * jnp.cos(pang), pmag * jnp.sin(pang)
    same_group = (jnp.arange(n_c)[:, None] // P) == (jnp.arange(n_s)[None, :] // N)

    def block_diag(m, mask):
        rows = m.transpose(0, 2, 1).reshape(K, GP * m.shape[2], m.shape[1])
        return jnp.where(mask, jnp.tile(rows, (1, 1, GP)), 0.0)

    bbp_re, bbp_im = block_diag(bb_re, same_group), block_diag(bb_im, same_group)
    cp_re = block_diag(c_re.astype(F32), same_group.T)
    cp_im = block_diag(c_im.astype(F32), same_group.T)
    al_re, al_im = pw_re[:, :L, None, :], pw_im[:, :L, None, :]
    wl_re = al_re * bbp_re[:, None] - al_im * bbp_im[:, None]
    wl_im = al_re * bbp_im[:, None] + al_im * bbp_re[:, None]
    kern = (jnp.einsum('klcs,ksd->klcd', wl_re, cp_re, precision=hi)
            - jnp.einsum('klcs,ksd->klcd', wl_im, cp_im, precision=hi))
    kern = jnp.concatenate([jnp.zeros_like(kern), kern], axis=1)
    toep = jnp.stack([kern[:, L - j:2 * L - j].transpose(0, 2, 1, 3).reshape(K, n_c, L * n_c)
                      for j in range(L)], axis=1).reshape(K, L * n_c, L * n_c)
    w = jnp.concatenate([wl_re[:, ::-1].reshape(K, L * n_c, n_s),
                         wl_im[:, ::-1].reshape(K, L * n_c, n_s)], axis=2)
    p1_re, p1_im = pw_re[:, 1:, :, None], pw_im[:, 1:, :, None]
    v_re = cp_re[:, None] * p1_re - cp_im[:, None] * p1_im
    v_im = -(cp_re[:, None] * p1_im + cp_im[:, None] * p1_re)
    v = (jnp.concatenate([v_re, v_im], axis=2).transpose(0, 2, 1, 3)
         .reshape(K, 2 * n_s, L * n_c))
    al = jnp.stack([pw_re[:, L], pw_im[:, L]], axis=1)
    dvec = jnp.tile(d_skip.astype(F32).reshape(K, 1, n_c), (1, L, 1)).reshape(K, 1, L * n_c)
    return toep.astype(BF16), w.astype(BF16), v.astype(BF16), al, dvec


def _ssm_kernel(u_ref, toep_ref, w_ref, v_ref, al_ref, d_ref, y_ref, lhs_s, e_s, s_s, carry_s):
    bsz, tt, lanes = u_ref.shape
    L = SSM_CHUNK
    n_chunks = tt // L
    ns = al_ref.shape[2]

    @pl.when(pl.program_id(1) == 0)
    def _():
        carry_s[...] = jnp.zeros_like(carry_s)

    for b in range(bsz):
        for t in range(L):
            lhs_s[t, pl.ds(b, n_chunks, stride=bsz), :] = u_ref[b, pl.ds(t, n_chunks, stride=L), :]
    lhs = jnp.concatenate([lhs_s[t] for t in range(L)], axis=1)
    lb = lhs.astype(BF16)
    e_s[...] = jnp.dot(lb, w_ref[0], preferred_element_type=F32)
    ar = jnp.broadcast_to(al_ref[0, 0:1, :], (bsz, ns))
    ai = jnp.broadcast_to(al_ref[0, 1:2, :], (bsz, ns))
    sr = carry_s[0]
    si = carry_s[1]
    for c in range(n_chunks):
        rows = slice(c * bsz, (c + 1) * bsz)
        s_s[rows, :ns] = sr
        s_s[rows, ns:] = si
        er = e_s[rows, :ns]
        ei = e_s[rows, ns:]
        sr, si = ar * sr - ai * si + er, ar * si + ai * sr + ei
    carry_s[0] = sr
    carry_s[1] = si
    y = (jnp.dot(lb, toep_ref[0], preferred_element_type=F32)
         + jnp.dot(s_s[...].astype(BF16), v_ref[0], preferred_element_type=F32)
         + d_ref[0] * lhs)
    yg = jax.nn.gelu(y)
    for t in range(L):
        lhs_s[t] = yg[:, t * lanes:(t + 1) * lanes]
    for b in range(bsz):
        for t in range(L):
            y_ref[b, pl.ds(t, n_chunks, stride=L), :] = lhs_s[t, pl.ds(b, n_chunks, stride=bsz), :]


def _ssm(u3, mats):
    toep, w, v, al, dvec = mats
    bsz, seq, sw = u3.shape
    k = toep.shape[0]
    lanes = sw // k
    tt = SSM_TILE
    rows = bsz * tt // SSM_CHUNK
    cl = SSM_CHUNK * lanes
    ns = al.shape[2]
    pack = lambda p, t: (p, 0, 0)
    tile = lambda p, t: (0, t, p)
    return pl.pallas_call(
        _ssm_kernel,
        grid=(k, seq // tt),
        in_specs=[
            pl.BlockSpec((bsz, tt, lanes), tile),
            pl.BlockSpec((1, cl, cl), pack),
            pl.BlockSpec((1, cl, 2 * ns), pack),
            pl.BlockSpec((1, 2 * ns, cl), pack),
            pl.BlockSpec((1, 2, ns), pack),
            pl.BlockSpec((1, 1, cl), pack),
        ],
        out_specs=pl.BlockSpec((bsz, tt, lanes), tile),
        out_shape=jax.ShapeDtypeStruct((bsz, seq, sw), F32),
        scratch_shapes=[
            pltpu.VMEM((SSM_CHUNK, rows, lanes), F32),
            pltpu.VMEM((rows, 2 * ns), F32),
            pltpu.VMEM((rows, 2 * ns), F32),
            pltpu.VMEM((2, bsz, ns), F32),
        ],
        compiler_params=_cparams(2),
        name="ssm",
    )(u3, toep, w, v, al, dvec)


def _mix_kernel(tiles_per_seq, x_ref, up_ref, halo_ref, ys_ref, gate_ref, poolw_ref, pscale_ref,
                gluw_ref, glub_ref, wpu_ref, wsu_ref, wout_ref, gffn_ref, rwh_ref, rwl_ref, rb_ref,
                tri_ref,
                x1_ref, h3_ref, idx_ref, gw_ref, rank_ref, cnt_ref, ext_s, carry_s):
    i = pl.program_id(0)
    j = i % tiles_per_seq
    tm = x_ref.shape[0]
    d_model = x_ref.shape[1]
    gdim = poolw_ref.shape[1]

    @pl.when(i == 0)
    def _():
        carry_s[...] = jnp.zeros_like(carry_s)

    ext_s[0:POOL_HALO, :] = jnp.where(j == 0, 0.0, halo_ref[...])
    ext_s[POOL_HALO:, :] = up_ref[...]
    pos = (j * tm + 1 + lax.broadcasted_iota(jnp.int32, (tm, 1), 0)).astype(F32)
    parts = []
    for g, w in enumerate(POOL_WINDOWS):
        cols = slice(g * gdim, (g + 1) * gdim)
        s = ext_s[:, cols]
        k = 1
        while k < w:
            s = s + pltpu.roll(s, k, 0)
            k *= 2
        s = s[POOL_HALO:POOL_HALO + tm]
        cur = ext_s[POOL_HALO:POOL_HALO + tm, cols]
        dlt = (s / jnp.minimum(pos, float(w)) - cur).astype(BF16)
        parts.append(jnp.dot(dlt, poolw_ref[g], preferred_element_type=F32))
    yp = jnp.concatenate(parts, axis=1) * pscale_ref[...]
    y_pool = jnp.dot(yp.astype(BF16), wpu_ref[...], preferred_element_type=F32)

    yg = ys_ref[...].astype(BF16)
    lin = jnp.dot(yg, gluw_ref[...], preferred_element_type=F32) + glub_ref[...]
    glu = yg * jax.nn.sigmoid(lin.astype(BF16))
    y_ssm = jnp.dot(glu, wsu_ref[...], preferred_element_type=F32)

    z = (gate_ref[:, :d_model] * y_pool.astype(BF16) + gate_ref[:, d_model:] * y_ssm.astype(BF16))
    x1 = x_ref[...] + jnp.dot(z, wout_ref[...], preferred_element_type=F32)
    x1_ref[...] = x1

    ms = jnp.mean(x1 * x1, axis=-1, keepdims=True)
    h2 = x1 * lax.rsqrt(ms + RMS_EPS) * gffn_ref[...]
    n_s = d_model // LANES
    for s in range(n_s):
        h3_ref[pl.ds(s, tm, stride=n_s), :] = h2[:, s * LANES:(s + 1) * LANES]

    h_hi = h2.astype(BF16)
    h_lo = (h2 - h_hi.astype(F32)).astype(BF16)
    nt = (((1,), (1,)), ((), ()))
    logits = (lax.dot_general(rwh_ref[...], h_hi, nt, preferred_element_type=F32)
              + lax.dot_general(rwh_ref[...], h_lo, nt, preferred_element_type=F32)
              + lax.dot_general(rwl_ref[...], h_hi, nt, preferred_element_type=F32)
              + rb_ref[...])
    n_e = logits.shape[0]
    iota_e = lax.broadcasted_iota(jnp.int32, (n_e, tm), 0)
    l = logits
    tops, hots = [], []
    for k in range(TOP_K):
        m = jnp.max(l, axis=0, keepdims=True)
        idx = jnp.min(jnp.where(l == m, iota_e, n_e), axis=0, keepdims=True)
        hot = iota_e == idx
        l = jnp.where(hot, -jnp.inf, l)
        tops.append(m)
        hots.append(hot)
        idx_ref[k:k + 1, :] = idx
    exps = [jnp.exp(m - tops[0]) for m in tops]
    den = exps[0] + exps[1] + exps[2] + exps[3]
    for k in range(TOP_K):
        gw_ref[k:k + 1, :] = exps[k] / den

    multi = sum(h.astype(F32) for h in hots)
    cum = jnp.dot(multi.astype(BF16), tri_ref[...], preferred_element_type=F32) + carry_s[:, 0:1]
    for k in range(TOP_K):
        rk = jnp.sum(jnp.where(hots[k], cum, 0.0), axis=0, keepdims=True)
        rank_ref[k:k + 1, :] = rk.astype(jnp.int32)
    carry_s[...] = carry_s[...] + jnp.sum(multi, axis=1, keepdims=True)
    cnt_ref[...] = carry_s[...]


def _mix(x2, u_pool, y_ssm, gates, pool_w, pool_scale, glu_w, glu_b, w_pool_up, w_ssm_up, w_out,
         g_ffn, rw_t, rb, seq_len):
    m, d = x2.shape
    pw = u_pool.shape[1]
    sw = y_ssm.shape[1]
    tm = ROW_TILE
    n_e = rw_t.shape[0]
    rw_hi = rw_t.astype(BF16)
    rw_lo = (rw_t - rw_hi.astype(F32)).astype(BF16)
    tiles_per_seq = seq_len // tm
    tri = (jnp.arange(tm)[:, None] < jnp.arange(tm)[None, :]).astype(BF16)
    row = lambda i: (i, 0)
    const2 = lambda i: (0, 0)
    const3 = lambda i: (0, 0, 0)
    col = lambda i: (0, i)
    halo = lambda i: (jnp.maximum(i * (tm // POOL_HALO) - 1, 0), 0)
    return pl.pallas_call(
        functools.partial(_mix_kernel, tiles_per_seq),
        grid=(m // tm,),
        in_specs=[
            pl.BlockSpec((tm, d), row),
            pl.BlockSpec((tm, pw), row),
            pl.BlockSpec((POOL_HALO, pw), halo),
            pl.BlockSpec((tm, sw), row),
            pl.BlockSpec((tm, 2 * d), row),
            pl.BlockSpec(pool_w.shape, const3),
            pl.BlockSpec((1, pw), const2),
            pl.BlockSpec((sw, sw), const2),
            pl.BlockSpec((1, sw), const2),
            pl.BlockSpec((pw, d), const2),
            pl.BlockSpec((sw, d), const2),
            pl.BlockSpec((d, d), const2),
            pl.BlockSpec((1, d), const2),
            pl.BlockSpec((n_e, d), const2),
            pl.BlockSpec((n_e, d), const2),
            pl.BlockSpec((n_e, 1), const2),
            pl.BlockSpec((tm, tm), const2),
        ],
        out_specs=[
            pl.BlockSpec((tm, d), row),
            pl.BlockSpec((tm * (d // LANES), LANES), row),
            pl.BlockSpec((TOP_K, tm), col),
            pl.BlockSpec((TOP_K, tm), col),
            pl.BlockSpec((TOP_K, tm), col),
            pl.BlockSpec((n_e, LANES), const2),
        ],
        out_shape=[
            jax.ShapeDtypeStruct((m, d), F32),
            jax.ShapeDtypeStruct((m * (d // LANES), LANES), F32),
            jax.ShapeDtypeStruct((TOP_K, m), jnp.int32),
            jax.ShapeDtypeStruct((TOP_K, m), F32),
            jax.ShapeDtypeStruct((TOP_K, m), jnp.int32),
            jax.ShapeDtypeStruct((n_e, LANES), F32),
        ],
        scratch_shapes=[
            pltpu.VMEM((POOL_HALO + tm, pw), F32),
            pltpu.VMEM((n_e, LANES), F32),
        ],
        compiler_params=_cparams(1),
        name="mix_route",
    )(x2, u_pool, u_pool, y_ssm, gates, pool_w, pool_scale, glu_w, glu_b, w_pool_up, w_ssm_up,
      w_out, g_ffn, rw_hi, rw_lo, rb, tri)


def _row_copy(src, dst, sem):
    return pltpu.make_async_copy(src, dst, sem)


def _for_each_run_piece(tab, tb, max_rows, n_s, fn):
    def per_expert(e, carry):
        n = tab[tb + e]
        off = tab[tb + N_EXPERTS + e]
        slot = tab[tb + 2 * N_EXPERTS + e]
        def piece(b):
            size = 1 << b

            @pl.when(((n >> b) & 1) == 1)
            def _():
                done = n & (size - 1)
                fn(pl.multiple_of((slot + done) * n_s, n_s), pl.multiple_of((off + done) * n_s, n_s),
                   size * n_s)

        n_bits = max_rows.bit_length()
        low_bits = min(n_bits, (2 * TOP_K * max_rows // N_EXPERTS - 1).bit_length())
        for b in range(low_bits):
            piece(b)

        @pl.when((n >> low_bits) != 0)
        def _():
            for b in range(low_bits, n_bits):
                piece(b)
        return carry

    lax.fori_loop(0, N_EXPERTS, per_expert, 0)


def _dispatch_kernel(n_s, tab_ref, lpos_ref, zpos_ref, h3_ref, xs_ref, st0_s, st1_s, sem):
    j = pl.program_id(0)
    tm = h3_ref.shape[0] // (2 * n_s)
    per_tile = TOP_K * tm
    zrows = MOE_ROWS * n_s

    @pl.when(j == 0)
    def _():
        st0_s[0:zrows, :] = jnp.zeros((zrows, st0_s.shape[1]), F32)

        def fill(start):
            cp = _row_copy(st0_s.at[pl.ds(0, zrows)],
                           xs_ref.at[pl.ds(pl.multiple_of(start * n_s, n_s), zrows)], sem.at[0])
            cp.start()
            cp.wait()

        for e in range(N_EXPERTS):
            fill(zpos_ref[e])

        def tail(t, carry):
            fill(t * MOE_ROWS)
            return carry

        lax.fori_loop(zpos_ref[N_EXPERTS], xs_ref.shape[0] // zrows, tail, 0)

    def drain(stage, slot):
        _row_copy(stage, xs_ref.at[pl.ds(0, per_tile * n_s)], sem.at[slot]).wait()

    def scatter(stage, slot):
        @pl.when(j > 0)
        def _():
            drain(stage, slot)

        base = slot * per_tile

        def tokens(g, carry):
            for u in range(COMBINE_UNROLL):
                t = g * COMBINE_UNROLL + u
                rows = h3_ref[pl.ds(pl.multiple_of((slot * tm + t) * n_s, n_s), n_s), :]
                for k in range(TOP_K):
                    row = pl.multiple_of(lpos_ref[base + k * tm + t], n_s)
                    stage[pl.ds(row, n_s), :] = rows
            return carry

        lax.fori_loop(0, tm // COMBINE_UNROLL, tokens, 0)
        _for_each_run_piece(
            tab_ref, slot * RUN_TAB, tm, n_s,
            lambda s, b, r: _row_copy(stage.at[pl.ds(b, r)], xs_ref.at[pl.ds(s, r)],
                                      sem.at[slot]).start())

    scatter(st0_s, 0)
    scatter(st1_s, 1)

    @pl.when(j + 1 == pl.num_programs(0))
    def _():
        drain(st0_s, 0)
        drain(st1_s, 1)


def _dispatch(tab, lpos_tiles, zpos, h3, n_slots, n_s):
    rows, lanes = h3.shape
    tm = COMBINE_TILE
    smem = pltpu.SMEM
    return pl.pallas_call(
        functools.partial(_dispatch_kernel, n_s),
        grid=(rows // (2 * tm * n_s),),
        in_specs=[
            pl.BlockSpec((2 * RUN_TAB,), lambda j: (j,), memory_space=smem),
            pl.BlockSpec((2 * TOP_K * tm,), lambda j: (j,), memory_space=smem),
            pl.BlockSpec(memory_space=smem),
            pl.BlockSpec((2 * tm * n_s, lanes), lambda j: (j, 0)),
        ],
        out_specs=pl.BlockSpec(memory_space=pl.ANY),
        out_shape=jax.ShapeDtypeStruct(((n_slots + MOE_ROWS) * n_s, lanes), F32),
        scratch_shapes=[pltpu.VMEM((TOP_K * tm * n_s, lanes), F32),
                        pltpu.VMEM((TOP_K * tm * n_s, lanes), F32),
                        pltpu.SemaphoreType.DMA((2,))],
        compiler_params=_cparams(1),
        name="dispatch",
    )(tab, lpos_tiles, zpos, h3)


def _expert_kernel(blk_e_ref, first_ref, nused_ref, xs_ref, w1_ref, b1_ref, w2_ref, b2_ref, perm_ref,
                   ys_ref, w1p_s, w2b_s):
    i = pl.program_id(0)
    n_s = w1_ref.shape[1] // LANES
    rows = xs_ref.shape[0] // n_s
    f2 = w1_ref.shape[2]

    @pl.when(first_ref[i] == 1)
    def _():
        for c in range(f2 // DEINT):
            cols = slice(c * DEINT, (c + 1) * DEINT)
            strip = w1_ref[0, :, cols].astype(BF16)
            w1p_s[:, cols] = jnp.dot(strip, perm_ref[...], preferred_element_type=F32).astype(BF16)
        w2b_s[...] = w2_ref[0].astype(BF16)

    @pl.when(i < nused_ref[0])
    def _():
        x = jnp.concatenate([xs_ref[pl.ds(s, rows, stride=n_s), :] for s in range(n_s)],
                            axis=1).astype(BF16)
        h = jnp.dot(x, w1p_s[...], preferred_element_type=F32) + b1_ref[0]
        acts = []
        for c in range(f2 // DEINT):
            xg = jnp.minimum(h[:, c * DEINT:c * DEINT + LANES], SWIGLU_LIMIT)
            xl = jnp.clip(h[:, c * DEINT + LANES:(c + 1) * DEINT], -SWIGLU_LIMIT, SWIGLU_LIMIT)
            acts.append((xg * jax.nn.sigmoid(SWIGLU_ALPHA * xg) * (xl + 1.0)).astype(BF16))
        act = jnp.concatenate(acts, axis=1)
        y = jnp.dot(act, w2b_s[...], preferred_element_type=F32) + b2_ref[0]
        for s in range(n_s):
            ys_ref[pl.ds(s, rows, stride=n_s), :] = y[:, s * LANES:(s + 1) * LANES]

    @pl.when(i >= nused_ref[0])
    def _():
        ys_ref[...] = jnp.zeros_like(ys_ref)


def _experts(blk_e, first, n_used, xs, w1, b1p, w2, b2, n_blocks):
    _, lanes = xs.shape
    _, d, f2 = w1.shape
    blk = MOE_ROWS * (d // lanes)
    f = w2.shape[1]
    half = jnp.arange(DEINT) // 2 + (jnp.arange(DEINT) % 2) * LANES
    perm = (half[:, None] == jnp.arange(DEINT)[None, :]).astype(BF16)
    xmap = lambda i, be, fi, nu: (jnp.minimum(i, nu[0] - 1), 0)
    emap = lambda i, be, fi, nu: (be[i], 0, 0)
    grid_spec = pltpu.PrefetchScalarGridSpec(
        num_scalar_prefetch=3,
        grid=(n_blocks,),
        in_specs=[
            pl.BlockSpec((blk, lanes), xmap),
            pl.BlockSpec((1, d, f2), emap),
            pl.BlockSpec((1, 1, f2), emap),
            pl.BlockSpec((1, f, d), emap),
            pl.BlockSpec((1, 1, d), emap),
            pl.BlockSpec((DEINT, DEINT), lambda i, be, fi, nu: (0, 0)),
        ],
        out_specs=pl.BlockSpec((blk, lanes), lambda i, be, fi, nu: (i, 0)),
        scratch_shapes=[pltpu.VMEM((d, f2), BF16), pltpu.VMEM((f, d), BF16)],
    )
    return pl.pallas_call(
        _expert_kernel,
        grid_spec=grid_spec,
        out_shape=jax.ShapeDtypeStruct((n_blocks * blk, lanes), F32),
        compiler_params=_cparams(1),
        name="experts",
    )(blk_e, first, n_used, xs, w1, b1p, w2, b2, perm)


def _combine_kernel(tab_ref, nxt_ref, lpos_ref, gw_ref, x1_ref, gfin_ref, ys_ref, out_ref,
                    buf0_s, buf1_s, acc_s, sem):
    j = pl.program_id(0)
    tm = x1_ref.shape[0] // 2
    n_s = x1_ref.shape[1] // LANES
    per_tile = TOP_K * tm
    tab_w = RUN_TAB

    def gather(tab, tb, buf, slot):
        _for_each_run_piece(
            tab, tb, tm, n_s,
            lambda s, b, r: _row_copy(ys_ref.at[pl.ds(s, r)], buf.at[pl.ds(b, r)],
                                      sem.at[slot]).start())

    def reduce(buf, slot, t0):
        _row_copy(ys_ref.at[pl.ds(0, per_tile * n_s)], buf, sem.at[slot]).wait()
        base = (t0 // tm) * per_tile

        def tokens(g, carry):
            for u in range(COMBINE_UNROLL):
                t = g * COMBINE_UNROLL + u
                acc = None
                for k in range(TOP_K):
                    row = pl.multiple_of(lpos_ref[base + k * tm + t], n_s)
                    term = gw_ref[base + k * tm + t] * buf[pl.ds(row, n_s), :]
                    acc = term if acc is None else acc + term
                acc_s[pl.ds(pl.multiple_of(t * n_s, n_s), n_s), :] = acc
            return carry

        lax.fori_loop(0, tm // COMBINE_UNROLL, tokens, 0)
        y = jnp.concatenate([acc_s[pl.ds(s, tm, stride=n_s), :] for s in range(n_s)], axis=1)
        acc = x1_ref[t0:t0 + tm, :] + y
        ms = jnp.mean(acc * acc, axis=-1, keepdims=True)
        out_ref[t0:t0 + tm, :] = acc * lax.rsqrt(ms + RMS_EPS) * gfin_ref[...]

    @pl.when(j == 0)
    def _():
        gather(tab_ref, 0, buf0_s, 0)

    gather(tab_ref, tab_w, buf1_s, 1)
    reduce(buf0_s, 0, 0)

    @pl.when(j + 1 < pl.num_programs(0))
    def _():
        gather(nxt_ref, 0, buf0_s, 0)

    reduce(buf1_s, 1, tm)


def _combine(tab, lpos_tiles, gw_tiles, x1, g_final, ys):
    m, d = x1.shape
    _, lanes = ys.shape
    s = d // lanes
    tm = COMBINE_TILE
    n_pairs = m // (2 * tm)
    tab_w = RUN_TAB
    smem = pltpu.SMEM
    return pl.pallas_call(
        _combine_kernel,
        grid=(n_pairs,),
        in_specs=[
            pl.BlockSpec((2 * tab_w,), lambda j: (j,), memory_space=smem),
            pl.BlockSpec((tab_w,), lambda j: (jnp.minimum(2 * j + 2, 2 * n_pairs - 1),),
                         memory_space=smem),
            pl.BlockSpec((2 * TOP_K * tm,), lambda j: (j,), memory_space=smem),
            pl.BlockSpec((2 * TOP_K * tm,), lambda j: (j,), memory_space=smem),
            pl.BlockSpec((2 * tm, d), lambda j: (j, 0)),
            pl.BlockSpec((1, d), lambda j: (0, 0)),
            pl.BlockSpec(memory_space=pl.ANY),
        ],
        out_specs=pl.BlockSpec((2 * tm, d), lambda j: (j, 0)),
        out_shape=jax.ShapeDtypeStruct((m, d), F32),
        scratch_shapes=[pltpu.VMEM((TOP_K * tm * s, lanes), F32),
                        pltpu.VMEM((TOP_K * tm * s, lanes), F32),
                        pltpu.VMEM((tm * s, lanes), F32),
                        pltpu.SemaphoreType.DMA((2,))],
        compiler_params=_cparams(1),
        name="combine",
    )(tab, tab, lpos_tiles, gw_tiles, x1, g_final, ys)


def kernel(x, norm_mix_g, w_in, b_gate, pool_w, pool_scale, ssm_lambda_re, ssm_lambda_im, ssm_log_dt, ssm_b_re, ssm_b_im, ssm_c_re, ssm_c_im, ssm_d, ssm_glu_w, ssm_glu_b, w_pool_up, w_ssm_up, w_out, norm_ffn_g, router_w, router_b, moe_w1, moe_b1, moe_w2, moe_b2, norm_final_g):
    bsz, seq, d = x.shape
    depth = w_in.shape[0]
    pw = pool_w.shape[1] * pool_w.shape[2]
    n_groups = ssm_lambda_re.shape[1]
    sw = n_groups * SSM_GROUP_DIM
    m = bsz * seq
    assert depth == 1
    assert bsz == SUBLANES and seq % ROW_TILE == 0 and seq % SSM_TILE == 0 and d % LANES == 0
    assert pool_w.shape[1] == len(POOL_WINDOWS) and n_groups % SSM_PACK == 0
    assert all(w & (w - 1) == 0 and w <= POOL_HALO for w in POOL_WINDOWS)
    assert moe_w1.shape[3] % DEINT == 0

    x2 = x.reshape(m, d)
    for l in range(depth):
        wi = w_in[l].astype(BF16)
        u_pool, u_ssm, gates = _inproj(
            x2, norm_mix_g[l][None], wi[:, :pw], wi[:, pw:pw + sw], wi[:, pw + sw:], b_gate[l][None])
        mats = _ssm_matrices(ssm_lambda_re[l], ssm_lambda_im[l], ssm_log_dt[l], ssm_b_re[l],
                             ssm_b_im[l], ssm_c_re[l], ssm_c_im[l], ssm_d[l])
        y_ssm = _ssm(u_ssm.reshape(bsz, seq, sw), mats).reshape(m, sw)
        x1, h3, idx_t, gw_t, rank_t, cnt = _mix(
            x2, u_pool, y_ssm, gates, pool_w[l].astype(BF16), pool_scale[l][None],
            ssm_glu_w[l].astype(BF16), ssm_glu_b[l][None], w_pool_up[l].astype(BF16),
            w_ssm_up[l].astype(BF16), w_out[l].astype(BF16), norm_ffn_g[l][None],
            router_w[l].T, router_b[l][:, None], seq)

        n_assign = m * TOP_K
        n_blocks = (n_assign + N_EXPERTS * (MOE_ROWS - 1) + MOE_ROWS - 1) // MOE_ROWS
        n_slots = n_blocks * MOE_ROWS
        counts = cnt[:, 0].astype(jnp.int32)
        padded = (counts + MOE_ROWS - 1) // MOE_ROWS * MOE_ROWS
        pad_end = jnp.cumsum(padded)
        pad_start = pad_end - padded
        n_used = (pad_end[-1] // MOE_ROWS).astype(jnp.int32)[None]
        blk_row = jnp.arange(n_blocks, dtype=jnp.int32) * MOE_ROWS
        blk_e = jnp.minimum(jnp.sum(pad_end[None, :] <= blk_row[:, None], axis=1),
                            N_EXPERTS - 1).astype(jnp.int32)
        first = jnp.concatenate([jnp.ones((1,), jnp.int32),
                                 (blk_e[1:] != blk_e[:-1]).astype(jnp.int32)])
        e_ids = jnp.arange(N_EXPERTS, dtype=jnp.int32)[:, None, None]
        zpos = jnp.concatenate([pad_start + counts, n_used]).astype(jnp.int32)
        ct = COMBINE_TILE
        n_ct = m // ct
        cnt_te = jnp.sum((idx_t.reshape(TOP_K, n_ct, ct)[None] == e_ids[..., None]).astype(jnp.int32),
                         axis=(1, 3))
        carry_te = jnp.cumsum(cnt_te, axis=1) - cnt_te
        offs_te = jnp.cumsum(cnt_te, axis=0) - cnt_te
        tab = jnp.stack([cnt_te, offs_te, pad_start[:, None] + carry_te, jnp.zeros_like(cnt_te)],
                        axis=0)
        tab = tab.transpose(2, 0, 1).reshape(-1).astype(jnp.int32)
        delta = jnp.repeat(offs_te - carry_te, ct, axis=1)
        lpos = rank_t + jnp.sum(jnp.where(idx_t[None] == e_ids, delta[:, None, :], 0), axis=0)
        tile_major = lambda a: a.reshape(TOP_K, n_ct, ct).transpose(1, 0, 2).reshape(-1)
        lpos_tiles = tile_major((lpos * (d // LANES)).astype(jnp.int32))

        xs = _dispatch(tab, lpos_tiles, zpos, h3, n_slots, d // LANES)
        f2 = moe_w1.shape[3]
        b1p = (moe_b1[l].reshape(N_EXPERTS, f2 // DEINT, LANES, 2)
               .transpose(0, 1, 3, 2).reshape(N_EXPERTS, 1, f2))
        ys = _experts(blk_e, first, n_used, xs, moe_w1[l], b1p, moe_w2[l], moe_b2[l][:, None, :],
                      n_blocks)
        x2 = _combine(tab, lpos_tiles, tile_major(gw_t), x1, norm_final_g[None], ys)
    return x2.reshape(bsz, seq, d)
```

```python
import functools
import math

import jax
import jax.numpy as jnp
from jax import lax
from jax.experimental import pallas as pl
from jax.experimental.pallas import tpu as pltpu

F32 = jnp.float32
BF16 = jnp.bfloat16

RMS_EPS = 1e-6
POOL_WINDOWS = (2, 4, 8, 16)
POOL_HALO = 16
SSM_GROUP_DIM = 16
SSM_STATE = 64
N_EXPERTS = 32
TOP_K = 4
SWIGLU_ALPHA = 1.702
SWIGLU_LIMIT = 7.0

LANES = 128
SUBLANES = 8
SSM_CHUNK = SUBLANES
SSM_PACK = LANES // SSM_GROUP_DIM
SSM_TILE = 512
DEINT = 2 * LANES
ROW_TILE = 512
INPROJ_TILE = 1024
COMBINE_TILE = 512
COMBINE_UNROLL = 8
RUN_TAB = 4 * N_EXPERTS
MOE_ROWS = 512
VMEM_LIMIT = 56 * 1024 * 1024


def _cparams(n_axes):
    return pltpu.CompilerParams(
        dimension_semantics=("arbitrary",) * n_axes, vmem_limit_bytes=VMEM_LIMIT)


def _inproj_kernel(x_ref, g_ref, wp_ref, ws_ref, wg_ref, bg_ref, up_ref, us_ref, gate_ref):
    x = x_ref[...]
    ms = jnp.mean(x * x, axis=-1, keepdims=True)
    h = (x * lax.rsqrt(ms + RMS_EPS) * g_ref[...]).astype(BF16)
    up_ref[...] = jnp.dot(h, wp_ref[...], preferred_element_type=F32)
    us_ref[...] = jnp.dot(h, ws_ref[...], preferred_element_type=F32)
    gl = jnp.dot(h, wg_ref[...], preferred_element_type=F32) + bg_ref[...]
    gate_ref[...] = jax.nn.sigmoid(gl).astype(BF16)


def _inproj(x2, g, wp, ws, wg, bg):
    m, d = x2.shape
    pw, sw, gw = wp.shape[1], ws.shape[1], wg.shape[1]
    tm = INPROJ_TILE
    const = lambda i: (0, 0)
    row = lambda i: (i, 0)
    return pl.pallas_call(
        _inproj_kernel,
        grid=(m // tm,),
        in_specs=[
            pl.BlockSpec((tm, d), row),
            pl.BlockSpec((1, d), const),
            pl.BlockSpec((d, pw), const),
            pl.BlockSpec((d, sw), const),
            pl.BlockSpec((d, gw), const),
            pl.BlockSpec((1, gw), const),
        ],
        out_specs=[
            pl.BlockSpec((tm, pw), row),
            pl.BlockSpec((tm, sw), row),
            pl.BlockSpec((tm, gw), row),
        ],
        out_shape=[
            jax.ShapeDtypeStruct((m, pw), F32),
            jax.ShapeDtypeStruct((m, sw), F32),
            jax.ShapeDtypeStruct((m, gw), BF16),
        ],
        compiler_params=_cparams(1),
        name="inproj",
    )(x2, g, wp, ws, wg, bg)


def _ssm_matrices(lam_re, lam_im, log_dt, b_re, b_im, c_re, c_im, d_skip):
    hi = lax.Precision.HIGHEST
    L, P, N, GP = SSM_CHUNK, SSM_GROUP_DIM, SSM_STATE, SSM_PACK
    G = lam_re.shape[0]
    K = G // GP
    lr, li = lam_re.astype(F32), lam_im.astype(F32)
    dt = jnp.exp(log_dt.astype(F32))[:, None]
    mag = jnp.exp(lr * dt)
    lb_re, lb_im = mag * jnp.cos(li * dt), mag * jnp.sin(li * dt)
    den = lr * lr + li * li
    xr, xi = lb_re - 1.0, lb_im
    f_re = (xr * lr + xi * li) / den
    f_im = (xi * lr - xr * li) / den
    br, bi = b_re.astype(F32), b_im.astype(F32)
    bb_re = f_re[..., None] * br - f_im[..., None] * bi
    bb_im = f_re[..., None] * bi + f_im[..., None] * br
    n_c, n_s = GP * P, GP * N
    tau = jnp.arange(L + 1, dtype=F32)[None, :, None]
    pmag = jnp.exp((lr * dt).reshape(K, 1, n_s) * tau)
    pang = (li * dt).reshape(K, 1, n_s) * tau
    pw_re, pw_im = pmag * jnp.cos(pang), pmag * jnp.sin(pang)
    same_group = (jnp.arange(n_c)[:, None] // P) == (jnp.arange(n_s)[None, :] // N)

    def block_diag(m, mask):
        rows = m.transpose(0, 2, 1).reshape(K, GP * m.shape[2], m.shape[1])
        return jnp.where(mask, jnp.tile(rows, (1, 1, GP)), 0.0)

    bbp_re, bbp_im = block_diag(bb_re, same_group), block_diag(bb_im, same_group)
    cp_re = block_diag(c_re.astype(F32), same_group.T)
    cp_im = block_diag(c_im.astype(F32), same_group.T)
    al_re, al_im = pw_re[:, :L, None, :], pw_im[:, :L, None, :]
    wl_re = al_re * bbp_re[:, None] - al_im * bbp_im[:, None]
    wl_im = al_re * bbp_im[:, None] + al_im * bbp_re[:, None]
    kern = (jnp.einsum('klcs,ksd->klcd', wl_re, cp_re, precision=hi)
            - jnp.einsum('klcs,ksd->klcd', wl_im, cp_im, precision=hi))
    kern = jnp.concatenate([jnp.zeros_like(kern), kern], axis=1)
    toep = jnp.stack([kern[:, L - j:2 * L - j].transpose(0, 2, 1, 3).reshape(K, n_c, L * n_c)
                      for j in range(L)], axis=1).reshape(K, L * n_c, L * n_c)
    w = jnp.concatenate([wl_re[:, ::-1].reshape(K, L * n_c, n_s),
                         wl_im[:, ::-1].reshape(K, L * n_c, n_s)], axis=2)
    p1_re, p1_im = pw_re[:, 1:, :, None], pw_im[:, 1:, :, None]
    v_re = cp_re[:, None] * p1_re - cp_im[:, None] * p1_im
    v_im = -(cp_re[:, None] * p1_im + cp_im[:, None] * p1_re)
    v = (jnp.concatenate([v_re, v_im], axis=2).transpose(0, 2, 1, 3)
         .reshape(K, 2 * n_s, L * n_c))
    al = jnp.stack([pw_re[:, L], pw_im[:, L]], axis=1)
    dvec = jnp.tile(d_skip.astype(F32).reshape(K, 1, n_c), (1, L, 1)).reshape(K, 1, L * n_c)
    return toep.astype(BF16), w.astype(BF16), v.astype(BF16), al, dvec


def _ssm_kernel(u_ref, toep_ref, w_ref, v_ref, al_ref, d_ref, y_ref, lhs_s, e_s, s_s, carry_s):
    bsz, tt, lanes = u_ref.shape
    L = SSM_CHUNK
    n_chunks = tt // L
    ns = al_ref.shape[2]

    @pl.when(pl.program_id(1) == 0)
    def _():
        carry_s[...] = jnp.zeros_like(carry_s)

    for b in range(bsz):
        for t in range(L):
            lhs_s[t, pl.ds(b, n_chunks, stride=bsz), :] = u_ref[b, pl.ds(t, n_chunks, stride=L), :]
    lhs = jnp.concatenate([lhs_s[t] for t in range(L)], axis=1)
    lb = lhs.astype(BF16)
    e_s[...] = jnp.dot(lb, w_ref[0], preferred_element_type=F32)
    ar = jnp.broadcast_to(al_ref[0, 0:1, :], (bsz, ns))
    ai = jnp.broadcast_to(al_ref[0, 1:2, :], (bsz, ns))
    sr = carry_s[0]
    si = carry_s[1]
    for c in range(n_chunks):
        rows = slice(c * bsz, (c + 1) * bsz)
        s_s[rows, :ns] = sr
        s_s[rows, ns:] = si
        er = e_s[rows, :ns]
        ei = e_s[rows, ns:]
        sr, si = ar * sr - ai * si + er, ar * si + ai * sr + ei
    carry_s[0] = sr
    carry_s[1] = si
    y = (jnp.dot(lb, toep_ref[0], preferred_element_type=F32)
         + jnp.dot(s_s[...].astype(BF16), v_ref[0], preferred_element_type=F32)
         + d_ref[0] * lhs)
    yg = jax.nn.gelu(y)
    for t in range(L):
        lhs_s[t] = yg[:, t * lanes:(t + 1) * lanes]
    for b in range(bsz):
        for t in range(L):
            y_ref[b, pl.ds(t, n_chunks, stride=L), :] = lhs_s[t, pl.ds(b, n_chunks, stride=bsz), :]


def _ssm(u3, mats):
    toep, w, v, al, dvec = mats
    bsz, seq, sw = u3.shape
    k = toep.shape[0]
    lanes = sw // k
    tt = SSM_TILE
    rows = bsz * tt // SSM_CHUNK
    cl = SSM_CHUNK * lanes
    ns = al.shape[2]
    pack = lambda p, t: (p, 0, 0)
    tile = lambda p, t: (0, t, p)
    return pl.pallas_call(
        _ssm_kernel,
        grid=(k, seq // tt),
        in_specs=[
            pl.BlockSpec((bsz, tt, lanes), tile),
            pl.BlockSpec((1, cl, cl), pack),
            pl.BlockSpec((1, cl, 2 * ns), pack),
            pl.BlockSpec((1, 2 * ns, cl), pack),
            pl.BlockSpec((1, 2, ns), pack),
            pl.BlockSpec((1, 1, cl), pack),
        ],
        out_specs=pl.BlockSpec((bsz, tt, lanes), tile),
        out_shape=jax.ShapeDtypeStruct((bsz, seq, sw), F32),
        scratch_shapes=[
            pltpu.VMEM((SSM_CHUNK, rows, lanes), F32),
            pltpu.VMEM((rows, 2 * ns), F32),
            pltpu.VMEM((rows, 2 * ns), F32),
            pltpu.VMEM((2, bsz, ns), F32),
        ],
        compiler_params=_cparams(2),
        name="ssm",
    )(u3, toep, w, v, al, dvec)


def _mix_kernel(tiles_per_seq, x_ref, up_ref, halo_ref, ys_ref, gate_ref, poolw_ref, pscale_ref,
                gluw_ref, glub_ref, wpu_ref, wsu_ref, wout_ref, gffn_ref, rwh_ref, rwl_ref, rb_ref,
                tri_ref,
                x1_ref, h3_ref, idx_ref, gw_ref, rank_ref, cnt_ref, ext_s, carry_s):
    i = pl.program_id(0)
    j = i % tiles_per_seq
    tm = x_ref.shape[0]
    d_model = x_ref.shape[1]
    gdim = poolw_ref.shape[1]

    @pl.when(i == 0)
    def _():
        carry_s[...] = jnp.zeros_like(carry_s)

    ext_s[0:POOL_HALO, :] = jnp.where(j == 0, 0.0, halo_ref[...])
    ext_s[POOL_HALO:, :] = up_ref[...]
    pos = (j * tm + 1 + lax.broadcasted_iota(jnp.int32, (tm, 1), 0)).astype(F32)
    parts = []
    for g, w in enumerate(POOL_WINDOWS):
        cols = slice(g * gdim, (g + 1) * gdim)
        s = ext_s[:, cols]
        k = 1
        while k < w:
            s = s + pltpu.roll(s, k, 0)
            k *= 2
        s = s[POOL_HALO:POOL_HALO + tm]
        cur = ext_s[POOL_HALO:POOL_HALO + tm, cols]
        dlt = (s / jnp.minimum(pos, float(w)) - cur).astype(BF16)
        parts.append(jnp.dot(dlt, poolw_ref[g], preferred_element_type=F32))
    yp = jnp.concatenate(parts, axis=1) * pscale_ref[...]
    y_pool = jnp.dot(yp.astype(BF16), wpu_ref[...], preferred_element_type=F32)

    yg = ys_ref[...].astype(BF16)
    lin = jnp.dot(yg, gluw_ref[...], preferred_element_type=F32) + glub_ref[...]
    glu = yg * jax.nn.sigmoid(lin.astype(BF16))
    y_ssm = jnp.dot(glu, wsu_ref[...], preferred_element_type=F32)

    z = (gate_ref[:, :d_model] * y_pool.astype(BF16) + gate_ref[:, d_model:] * y_ssm.astype(BF16))
    x1 = x_ref[...] + jnp.dot(z, wout_ref[...], preferred_element_type=F32)
    x1_ref[...] = x1

    ms = jnp.mean(x1 * x1, axis=-1, keepdims=True)
    h2 = x1 * lax.rsqrt(ms + RMS_EPS) * gffn_ref[...]
    n_s = d_model // LANES
    for s in range(n_s):
        h3_ref[pl.ds(s, tm, stride=n_s), :] = h2[:, s * LANES:(s + 1) * LANES]

    h_hi = h2.astype(BF16)
    h_lo = (h2 - h_hi.astype(F32)).astype(BF16)
    nt = (((1,), (1,)), ((), ()))
    logits = (lax.dot_general(rwh_ref[...], h_hi, nt, preferred_element_type=F32)
              + lax.dot_general(rwh_ref[...], h_lo, nt, preferred_element_type=F32)
              + lax.dot_general(rwl_ref[...], h_hi, nt, preferred_element_type=F32)
              + rb_ref[...])
    n_e = logits.shape[0]
    iota_e = lax.broadcasted_iota(jnp.int32, (n_e, tm), 0)
    l = logits
    tops, hots = [], []
    for k in range(TOP_K):
        m = jnp.max(l, axis=0, keepdims=True)
        idx = jnp.min(jnp.where(l == m, iota_e, n_e), axis=0, keepdims=True)
        hot = iota_e == idx
        l = jnp.where(hot, -jnp.inf, l)
        tops.append(m)
        hots.append(hot)
        idx_ref[k:k + 1, :] = idx
    exps = [jnp.exp(m - tops[0]) for m in tops]
    den = exps[0] + exps[1] + exps[2] + exps[3]
    for k in range(TOP_K):
        gw_ref[k:k + 1, :] = exps[k] / den

    multi = sum(h.astype(F32) for h in hots)
    cum = jnp.dot(multi.astype(BF16), tri_ref[...], preferred_element_type=F32) + carry_s[:, 0:1]
    for k in range(TOP_K):
        rk = jnp.sum(jnp.where(hots[k], cum, 0.0), axis=0, keepdims=True)
        rank_ref[k:k + 1, :] = rk.astype(jnp.int32)
    carry_s[...] = carry_s[...] + jnp.sum(multi, axis=1, keepdims=True)
    cnt_ref[...] = carry_s[...]


def _mix(x2, u_pool, y_ssm, gates, pool_w, pool_scale, glu_w, glu_b, w_pool_up, w_ssm_up, w_out,
         g_ffn, rw_t, rb, seq_len):
    m, d = x2.shape
    pw = u_pool.shape[1]
    sw = y_ssm.shape[1]
    tm = ROW_TILE
    n_e = rw_t.shape[0]
    rw_hi = rw_t.astype(BF16)
    rw_lo = (rw_t - rw_hi.astype(F32)).astype(BF16)
    tiles_per_seq = seq_len // tm
    tri = (jnp.arange(tm)[:, None] < jnp.arange(tm)[None, :]).astype(BF16)
    row = lambda i: (i, 0)
    const2 = lambda i: (0, 0)
    const3 = lambda i: (0, 0, 0)
    col = lambda i: (0, i)
    halo = lambda i: (jnp.maximum(i * (tm // POOL_HALO) - 1, 0), 0)
    return pl.pallas_call(
        functools.partial(_mix_kernel, tiles_per_seq),
        grid=(m // tm,),
        in_specs=[
            pl.BlockSpec((tm, d), row),
            pl.BlockSpec((tm, pw), row),
            pl.BlockSpec((POOL_HALO, pw), halo),
            pl.BlockSpec((tm, sw), row),
            pl.BlockSpec((tm, 2 * d), row),
            pl.BlockSpec(pool_w.shape, const3),
            pl.BlockSpec((1, pw), const2),
            pl.BlockSpec((sw, sw), const2),
            pl.BlockSpec((1, sw), const2),
            pl.BlockSpec((pw, d), const2),
            pl.BlockSpec((sw, d), const2),
            pl.BlockSpec((d, d), const2),
            pl.BlockSpec((1, d), const2),
            pl.BlockSpec((n_e, d), const2),
            pl.BlockSpec((n_e, d), const2),
            pl.BlockSpec((n_e, 1), const2),
            pl.BlockSpec((tm, tm), const2),
        ],
        out_specs=[
            pl.BlockSpec((tm, d), row),
            pl.BlockSpec((tm * (d // LANES), LANES), row),
            pl.BlockSpec((TOP_K, tm), col),
            pl.BlockSpec((TOP_K, tm), col),
            pl.BlockSpec((TOP_K, tm), col),
            pl.BlockSpec((n_e, LANES), const2),
        ],
        out_shape=[
            jax.ShapeDtypeStruct((m, d), F32),
            jax.ShapeDtypeStruct((m * (d // LANES), LANES), F32),
            jax.ShapeDtypeStruct((TOP_K, m), jnp.int32),
            jax.ShapeDtypeStruct((TOP_K, m), F32),
            jax.ShapeDtypeStruct((TOP_K, m), jnp.int32),
            jax.ShapeDtypeStruct((n_e, LANES), F32),
        ],
        scratch_shapes=[
            pltpu.VMEM((POOL_HALO + tm, pw), F32),
            pltpu.VMEM((n_e, LANES), F32),
        ],
        compiler_params=_cparams(1),
        name="mix_route",
    )(x2, u_pool, u_pool, y_ssm, gates, pool_w, pool_scale, glu_w, glu_b, w_pool_up, w_ssm_up,
      w_out, g_ffn, rw_hi, rw_lo, rb, tri)


def _row_copy(src, dst, sem):
    return pltpu.make_async_copy(src, dst, sem)


def _for_each_run_piece(tab, tb, max_rows, n_s, fn):
    def per_expert(e, carry):
        n = tab[tb + e]
        off = tab[tb + N_EXPERTS + e]
        slot = tab[tb + 2 * N_EXPERTS + e]
        def piece(b):
            size = 1 << b

            @pl.when(((n >> b) & 1) == 1)
            def _():
                done = n & (size - 1)
                fn(pl.multiple_of((slot + done) * n_s, n_s), pl.multiple_of((off + done) * n_s, n_s),
                   size * n_s)

        n_bits = max_rows.bit_length()
        low_bits = min(n_bits, (2 * TOP_K * max_rows // N_EXPERTS - 1).bit_length())
        for b in range(low_bits):
            piece(b)

        @pl.when((n >> low_bits) != 0)
        def _():
            for b in range(low_bits, n_bits):
                piece(b)
        return carry

    lax.fori_loop(0, N_EXPERTS, per_expert, 0)


def _dispatch_kernel(n_s, tab_ref, lpos_ref, zpos_ref, h3_ref, xs_ref, st0_s, st1_s, sem):
    j = pl.program_id(0)
    tm = h3_ref.shape[0] // (2 * n_s)
    per_tile = TOP_K * tm
    zrows = MOE_ROWS * n_s

    @pl.when(j == 0)
    def _():
        st0_s[0:zrows, :] = jnp.zeros((zrows, st0_s.shape[1]), F32)

        def fill(start):
            cp = _row_copy(st0_s.at[pl.ds(0, zrows)],
                           xs_ref.at[pl.ds(pl.multiple_of(start * n_s, n_s), zrows)], sem.at[0])
            cp.start()
            cp.wait()

        for e in range(N_EXPERTS):
            fill(zpos_ref[e])

        def tail(t, carry):
            fill(t * MOE_ROWS)
            return carry

        lax.fori_loop(zpos_ref[N_EXPERTS], xs_ref.shape[0] // zrows, tail, 0)

    def drain(stage, slot):
        _row_copy(stage, xs_ref.at[pl.ds(0, per_tile * n_s)], sem.at[slot]).wait()

    def scatter(stage, slot):
        @pl.when(j > 0)
        def _():
            drain(stage, slot)

        base = slot * per_tile

        def tokens(g, carry):
            first = base + g * (COMBINE_UNROLL * TOP_K)
            for u in range(COMBINE_UNROLL):
                t = g * COMBINE_UNROLL + u
                rows = h3_ref[pl.ds(pl.multiple_of((slot * tm + t) * n_s, n_s), n_s), :]
                for k in range(TOP_K):
                    row = pl.multiple_of(lpos_ref[first + u * TOP_K + k], n_s)
                    stage[pl.ds(row, n_s), :] = rows
            return carry

        lax.fori_loop(0, tm // COMBINE_UNROLL, tokens, 0)
        _for_each_run_piece(
            tab_ref, slot * RUN_TAB, tm, n_s,
            lambda s, b, r: _row_copy(stage.at[pl.ds(b, r)], xs_ref.at[pl.ds(s, r)],
                                      sem.at[slot]).start())

    scatter(st0_s, 0)
    scatter(st1_s, 1)

    @pl.when(j + 1 == pl.num_programs(0))
    def _():
        drain(st0_s, 0)
        drain(st1_s, 1)


def _dispatch(tab, lpos_tiles, zpos, h3, n_slots, n_s):
    rows, lanes = h3.shape
    tm = COMBINE_TILE
    smem = pltpu.SMEM
    return pl.pallas_call(
        functools.partial(_dispatch_kernel, n_s),
        grid=(rows // (2 * tm * n_s),),
        in_specs=[
            pl.BlockSpec((2 * RUN_TAB,), lambda j: (j,), memory_space=smem),
            pl.BlockSpec((2 * TOP_K * tm,), lambda j: (j,), memory_space=smem),
            pl.BlockSpec(memory_space=smem),
            pl.BlockSpec((2 * tm * n_s, lanes), lambda j: (j, 0)),
        ],
        out_specs=pl.BlockSpec(memory_space=pl.ANY),
        out_shape=jax.ShapeDtypeStruct(((n_slots + MOE_ROWS) * n_s, lanes), F32),
        scratch_shapes=[pltpu.VMEM((TOP_K * tm * n_s, lanes), F32),
                        pltpu.VMEM((TOP_K * tm * n_s, lanes), F32),
                        pltpu.SemaphoreType.DMA((2,))],
        compiler_params=_cparams(1),
        name="dispatch",
    )(tab, lpos_tiles, zpos, h3)


def _expert_kernel(blk_e_ref, first_ref, next_ref, nused_ref, xs_ref, w1_hbm, b1_ref, w2_hbm, b2_ref,
                   perm_ref, ys_ref, w1f_s, w2f_s, w1p_s, w2b_s, sem):
    i = pl.program_id(0)
    n_s = w1f_s.shape[0] // LANES
    rows = xs_ref.shape[0] // n_s
    f2 = w1f_s.shape[1]

    def fetch(e):
        return (_row_copy(w1_hbm.at[e], w1f_s, sem.at[0]), _row_copy(w2_hbm.at[e], w2f_s, sem.at[1]))

    @pl.when(i == 0)
    def _():
        for cp in fetch(blk_e_ref[0]):
            cp.start()

    @pl.when(first_ref[i] == 1)
    def _():
        for cp in fetch(blk_e_ref[i]):
            cp.wait()
        for c in range(f2 // DEINT):
            cols = slice(c * DEINT, (c + 1) * DEINT)
            strip = w1f_s[:, cols].astype(BF16)
            w1p_s[:, cols] = jnp.dot(strip, perm_ref[...], preferred_element_type=F32).astype(BF16)
        w2b_s[...] = w2f_s[...].astype(BF16)

        @pl.when(next_ref[i] >= 0)
        def _():
            for cp in fetch(next_ref[i]):
                cp.start()

    @pl.when(i < nused_ref[0])
    def _():
        x = jnp.concatenate([xs_ref[pl.ds(s, rows, stride=n_s), :] for s in range(n_s)],
                            axis=1).astype(BF16)
        h = jnp.dot(x, w1p_s[...], preferred_element_type=F32) + b1_ref[0]
        acts = []
        for c in range(f2 // DEINT):
            xg = jnp.minimum(h[:, c * DEINT:c * DEINT + LANES], SWIGLU_LIMIT)
            xl = jnp.clip(h[:, c * DEINT + LANES:(c + 1) * DEINT], -SWIGLU_LIMIT, SWIGLU_LIMIT)
            acts.append((xg * jax.nn.sigmoid(SWIGLU_ALPHA * xg) * (xl + 1.0)).astype(BF16))
        act = jnp.concatenate(acts, axis=1)
        y = jnp.dot(act, w2b_s[...], preferred_element_type=F32) + b2_ref[0]
        for s in range(n_s):
            ys_ref[pl.ds(s, rows, stride=n_s), :] = y[:, s * LANES:(s + 1) * LANES]

    @pl.when(i >= nused_ref[0])
    def _():
        ys_ref[...] = jnp.zeros_like(ys_ref)


def _experts(blk_e, first, nxt, n_used, xs, w1, b1p, w2, b2, n_blocks):
    _, lanes = xs.shape
    _, d, f2 = w1.shape
    blk = MOE_ROWS * (d // lanes)
    f = w2.shape[1]
    half = jnp.arange(DEINT) // 2 + (jnp.arange(DEINT) % 2) * LANES
    perm = (half[:, None] == jnp.arange(DEINT)[None, :]).astype(BF16)
    xmap = lambda i, be, fi, nx, nu: (jnp.minimum(i, nu[0] - 1), 0)
    emap = lambda i, be, fi, nx, nu: (be[i], 0, 0)
    grid_spec = pltpu.PrefetchScalarGridSpec(
        num_scalar_prefetch=4,
        grid=(n_blocks,),
        in_specs=[
            pl.BlockSpec((blk, lanes), xmap),
            pl.BlockSpec(memory_space=pl.ANY),
            pl.BlockSpec((1, 1, f2), emap),
            pl.BlockSpec(memory_space=pl.ANY),
            pl.BlockSpec((1, 1, d), emap),
            pl.BlockSpec((DEINT, DEINT), lambda i, be, fi, nx, nu: (0, 0)),
        ],
        out_specs=pl.BlockSpec((blk, lanes), lambda i, be, fi, nx, nu: (i, 0)),
        scratch_shapes=[pltpu.VMEM((d, f2), F32), pltpu.VMEM((f, d), F32),
                        pltpu.VMEM((d, f2), BF16), pltpu.VMEM((f, d), BF16),
                        pltpu.SemaphoreType.DMA((2,))],
    )
    return pl.pallas_call(
        _expert_kernel,
        grid_spec=grid_spec,
        out_shape=jax.ShapeDtypeStruct((n_blocks * blk, lanes), F32),
        compiler_params=_cparams(1),
        name="experts",
    )(blk_e, first, nxt, n_used, xs, w1, b1p, w2, b2, perm)


def _combine_kernel(tab_ref, nxt_ref, lpos_ref, gw_ref, x1_ref, gfin_ref, ys_ref, out_ref,
                    buf0_s, buf1_s, acc_s, sem):
    j = pl.program_id(0)
    tm = x1_ref.shape[0] // 2
    n_s = x1_ref.shape[1] // LANES
    per_tile = TOP_K * tm
    tab_w = RUN_TAB

    def gather(tab, tb, buf, slot):
        _for_each_run_piece(
            tab, tb, tm, n_s,
            lambda s, b, r: _row_copy(ys_ref.at[pl.ds(s, r)], buf.at[pl.ds(b, r)],
                                      sem.at[slot]).start())

    def reduce(buf, slot, t0):
        _row_copy(ys_ref.at[pl.ds(0, per_tile * n_s)], buf, sem.at[slot]).wait()
        base = (t0 // tm) * per_tile

        def tokens(g, carry):
            first = base + g * (COMBINE_UNROLL * TOP_K)
            for u in range(COMBINE_UNROLL):
                t = g * COMBINE_UNROLL + u
                acc = None
                for k in range(TOP_K):
                    row = pl.multiple_of(lpos_ref[first + u * TOP_K + k], n_s)
                    term = gw_ref[first + u * TOP_K + k] * buf[pl.ds(row, n_s), :]
                    acc = term if acc is None else acc + term
                acc_s[pl.ds(pl.multiple_of(t * n_s, n_s), n_s), :] = acc
            return carry

        lax.fori_loop(0, tm // COMBINE_UNROLL, tokens, 0)
        y = jnp.concatenate([acc_s[pl.ds(s, tm, stride=n_s), :] for s in range(n_s)], axis=1)
        acc = x1_ref[t0:t0 + tm, :] + y
        ms = jnp.mean(acc * acc, axis=-1, keepdims=True)
        out_ref[t0:t0 + tm, :] = acc * lax.rsqrt(ms + RMS_EPS) * gfin_ref[...]

    @pl.when(j == 0)
    def _():
        gather(tab_ref, 0, buf0_s, 0)

    gather(tab_ref, tab_w, buf1_s, 1)
    reduce(buf0_s, 0, 0)

    @pl.when(j + 1 < pl.num_programs(0))
    def _():
        gather(nxt_ref, 0, buf0_s, 0)

    reduce(buf1_s, 1, tm)


def _combine(tab, lpos_tiles, gw_tiles, x1, g_final, ys):
    m, d = x1.shape
    _, lanes = ys.shape
    s = d // lanes
    tm = COMBINE_TILE
    n_pairs = m // (2 * tm)
    tab_w = RUN_TAB
    smem = pltpu.SMEM
    return pl.pallas_call(
        _combine_kernel,
        grid=(n_pairs,),
        in_specs=[
            pl.BlockSpec((2 * tab_w,), lambda j: (j,), memory_space=smem),
            pl.BlockSpec((tab_w,), lambda j: (jnp.minimum(2 * j + 2, 2 * n_pairs - 1),),
                         memory_space=smem),
            pl.BlockSpec((2 * TOP_K * tm,), lambda j: (j,), memory_space=smem),
            pl.BlockSpec((2 * TOP_K * tm,), lambda j: (j,), memory_space=smem),
            pl.BlockSpec((2 * tm, d), lambda j: (j, 0)),
            pl.BlockSpec((1, d), lambda j: (0, 0)),
            pl.BlockSpec(memory_space=pl.ANY),
        ],
        out_specs=pl.BlockSpec((2 * tm, d), lambda j: (j, 0)),
        out_shape=jax.ShapeDtypeStruct((m, d), F32),
        scratch_shapes=[pltpu.VMEM((TOP_K * tm * s, lanes), F32),
                        pltpu.VMEM((TOP_K * tm * s, lanes), F32),
                        pltpu.VMEM((tm * s, lanes), F32),
                        pltpu.SemaphoreType.DMA((2,))],
        compiler_params=_cparams(1),
        name="combine",
    )(tab, tab, lpos_tiles, gw_tiles, x1, g_final, ys)


def kernel(x, norm_mix_g, w_in, b_gate, pool_w, pool_scale, ssm_lambda_re, ssm_lambda_im, ssm_log_dt, ssm_b_re, ssm_b_im, ssm_c_re, ssm_c_im, ssm_d, ssm_glu_w, ssm_glu_b, w_pool_up, w_ssm_up, w_out, norm_ffn_g, router_w, router_b, moe_w1, moe_b1, moe_w2, moe_b2, norm_final_g):
    bsz, seq, d = x.shape
    depth = w_in.shape[0]
    pw = pool_w.shape[1] * pool_w.shape[2]
    n_groups = ssm_lambda_re.shape[1]
    sw = n_groups * SSM_GROUP_DIM
    m = bsz * seq
    assert depth == 1
    assert bsz == SUBLANES and seq % ROW_TILE == 0 and seq % SSM_TILE == 0 and d % LANES == 0
    assert pool_w.shape[1] == len(POOL_WINDOWS) and n_groups % SSM_PACK == 0
    assert all(w & (w - 1) == 0 and w <= POOL_HALO for w in POOL_WINDOWS)
    assert moe_w1.shape[3] % DEINT == 0

    x2 = x.reshape(m, d)
    for l in range(depth):
        wi = w_in[l].astype(BF16)
        u_pool, u_ssm, gates = _inproj(
            x2, norm_mix_g[l][None], wi[:, :pw], wi[:, pw:pw + sw], wi[:, pw + sw:], b_gate[l][None])
        mats = _ssm_matrices(ssm_lambda_re[l], ssm_lambda_im[l], ssm_log_dt[l], ssm_b_re[l],
                             ssm_b_im[l], ssm_c_re[l], ssm_c_im[l], ssm_d[l])
        y_ssm = _ssm(u_ssm.reshape(bsz, seq, sw), mats).reshape(m, sw)
        x1, h3, idx_t, gw_t, rank_t, cnt = _mix(
            x2, u_pool, y_ssm, gates, pool_w[l].astype(BF16), pool_scale[l][None],
            ssm_glu_w[l].astype(BF16), ssm_glu_b[l][None], w_pool_up[l].astype(BF16),
            w_ssm_up[l].astype(BF16), w_out[l].astype(BF16), norm_ffn_g[l][None],
            router_w[l].T, router_b[l][:, None], seq)

        n_assign = m * TOP_K
        n_blocks = (n_assign + N_EXPERTS * (MOE_ROWS - 1) + MOE_ROWS - 1) // MOE_ROWS
        n_slots = n_blocks * MOE_ROWS
        counts = cnt[:, 0].astype(jnp.int32)
        padded = (counts + MOE_ROWS - 1) // MOE_ROWS * MOE_ROWS
        pad_end = jnp.cumsum(padded)
        pad_start = pad_end - padded
        n_used = (pad_end[-1] // MOE_ROWS).astype(jnp.int32)[None]
        blk_row = jnp.arange(n_blocks, dtype=jnp.int32) * MOE_ROWS
        blk_e = jnp.minimum(jnp.sum(pad_end[None, :] <= blk_row[:, None], axis=1),
                            N_EXPERTS - 1).astype(jnp.int32)
        blk_ids = jnp.arange(n_blocks, dtype=jnp.int32)
        first = jnp.concatenate([jnp.ones((1,), jnp.int32),
                                 (blk_e[1:] != blk_e[:-1]).astype(jnp.int32)])
        first = jnp.where(blk_ids < n_used[0], first, 0)
        ex = jnp.arange(N_EXPERTS, dtype=jnp.int32)
        later = (ex[None, :] > ex[:, None]) & (counts[None, :] > 0)
        next_e = jnp.min(jnp.where(later, ex[None, :], N_EXPERTS), axis=1)
        next_e = jnp.where(next_e < N_EXPERTS, next_e, -1).astype(jnp.int32)
        nxt = jnp.sum(jnp.where(blk_e[:, None] == ex[None, :], next_e[None, :], 0), axis=1)
        e_ids = jnp.arange(N_EXPERTS, dtype=jnp.int32)[:, None, None]
        zpos = jnp.concatenate([pad_start + counts, n_used]).astype(jnp.int32)
        ct = COMBINE_TILE
        n_ct = m // ct
        cnt_te = jnp.sum((idx_t.reshape(TOP_K, n_ct, ct)[None] == e_ids[..., None]).astype(jnp.int32),
                         axis=(1, 3))
        carry_te = jnp.cumsum(cnt_te, axis=1) - cnt_te
        offs_te = jnp.cumsum(cnt_te, axis=0) - cnt_te
        tab = jnp.stack([cnt_te, offs_te, pad_start[:, None] + carry_te, jnp.zeros_like(cnt_te)],
                        axis=0)
        tab = tab.transpose(2, 0, 1).reshape(-1).astype(jnp.int32)
        delta = jnp.repeat(offs_te - carry_te, ct, axis=1)
        lpos = rank_t + jnp.sum(jnp.where(idx_t[None] == e_ids, delta[:, None, :], 0), axis=0)
        token_major = lambda a: a.T.reshape(-1)
        lpos_tiles = token_major((lpos * (d // LANES)).astype(jnp.int32))

        xs = _dispatch(tab, lpos_tiles, zpos, h3, n_slots, d // LANES)
        f2 = moe_w1.shape[3]
        b1p = (moe_b1[l].reshape(N_EXPERTS, f2 // DEINT, LANES, 2)
               .transpose(0, 1, 3, 2).reshape(N_EXPERTS, 1, f2))
        ys = _experts(blk_e, first, nxt, n_used, xs, moe_w1[l], b1p, moe_w2[l], moe_b2[l][:, None, :],
                      n_blocks)
        x2 = _combine(tab, lpos_tiles, token_major(gw_t), x1, norm_final_g[None], ys)
    return x2.reshape(bsz, seq, d)
```

```python
import functools
import math

import jax
import jax.numpy as jnp
from jax import lax
from jax.experimental import pallas as pl
from jax.experimental.pallas import tpu as pltpu

F32 = jnp.float32
BF16 = jnp.bfloat16

RMS_EPS = 1e-6
POOL_WINDOWS = (2, 4, 8, 16)
POOL_HALO = 16
SSM_GROUP_DIM = 16
SSM_STATE = 64
N_EXPERTS = 32
TOP_K = 4
SWIGLU_ALPHA = 1.702
SWIGLU_LIMIT = 7.0

LANES = 128
SUBLANES = 8
SSM_CHUNK = SUBLANES
SSM_PACK = LANES // SSM_GROUP_DIM
SSM_TILE = 512
DEINT = 2 * LANES
ROW_TILE = 512
INPROJ_TILE = 1024
COMBINE_TILE = 512
COMBINE_UNROLL = 8
RUN_TAB = 4 * N_EXPERTS
MOE_ROWS = 512
VMEM_LIMIT = 56 * 1024 * 1024


def _cparams(n_axes):
    return pltpu.CompilerParams(
        dimension_semantics=("arbitrary",) * n_axes, vmem_limit_bytes=VMEM_LIMIT)


def _inproj_kernel(x_ref, g_ref, wp_ref, ws_ref, wg_ref, bg_ref, up_ref, us_ref, gate_ref):
    x = x_ref[...]
    ms = jnp.mean(x * x, axis=-1, keepdims=True)
    h = (x * lax.rsqrt(ms + RMS_EPS) * g_ref[...]).astype(BF16)
    up_ref[...] = jnp.dot(h, wp_ref[...], preferred_element_type=F32)
    us_ref[...] = jnp.dot(h, ws_ref[...], preferred_element_type=F32)
    gl = jnp.dot(h, wg_ref[...], preferred_element_type=F32) + bg_ref[...]
    gate_ref[...] = jax.nn.sigmoid(gl).astype(BF16)


def _inproj(x2, g, wp, ws, wg, bg):
    m, d = x2.shape
    pw, sw, gw = wp.shape[1], ws.shape[1], wg.shape[1]
    tm = INPROJ_TILE
    const = lambda i: (0, 0)
    row = lambda i: (i, 0)
    return pl.pallas_call(
        _inproj_kernel,
        grid=(m // tm,),
        in_specs=[
            pl.BlockSpec((tm, d), row),
            pl.BlockSpec((1, d), const),
            pl.BlockSpec((d, pw), const),
            pl.BlockSpec((d, sw), const),
            pl.BlockSpec((d, gw), const),
            pl.BlockSpec((1, gw), const),
        ],
        out_specs=[
            pl.BlockSpec((tm, pw), row),
            pl.BlockSpec((tm, sw), row),
            pl.BlockSpec((tm, gw), row),
        ],
        out_shape=[
            jax.ShapeDtypeStruct((m, pw), F32),
            jax.ShapeDtypeStruct((m, sw), F32),
            jax.ShapeDtypeStruct((m, gw), BF16),
        ],
        compiler_params=_cparams(1),
        name="inproj",
    )(x2, g, wp, ws, wg, bg)


def _ssm_matrices(lam_re, lam_im, log_dt, b_re, b_im, c_re, c_im, d_skip):
    hi = lax.Precision.HIGHEST
    L, P, N, GP = SSM_CHUNK, SSM_GROUP_DIM, SSM_STATE, SSM_PACK
    G = lam_re.shape[0]
    K = G // GP
    lr, li = lam_re.astype(F32), lam_im.astype(F32)
    dt = jnp.exp(log_dt.astype(F32))[:, None]
    mag = jnp.exp(lr * dt)
    lb_re, lb_im = mag * jnp.cos(li * dt), mag * jnp.sin(li * dt)
    den = lr * lr + li * li
    xr, xi = lb_re - 1.0, lb_im
    f_re = (xr * lr + xi * li) / den
    f_im = (xi * lr - xr * li) / den
    br, bi = b_re.astype(F32), b_im.astype(F32)
    bb_re = f_re[..., None] * br - f_im[..., None] * bi
    bb_im = f_re[..., None] * bi + f_im[..., None] * br
    n_c, n_s = GP * P, GP * N
    tau = jnp.arange(L + 1, dtype=F32)[None, :, None]
    pmag = jnp.exp((lr * dt).reshape(K, 1, n_s) * tau)
    pang = (li * dt).reshape(K, 1, n_s) * tau
    pw_re, pw_im = pmag * jnp.cos(pang), pmag * jnp.sin(pang)
    same_group = (jnp.arange(n_c)[:, None] // P) == (jnp.arange(n_s)[None, :] // N)

    def block_diag(m, mask):
        rows = m.transpose(0, 2, 1).reshape(K, GP * m.shape[2], m.shape[1])
        return jnp.where(mask, jnp.tile(rows, (1, 1, GP)), 0.0)

    bbp_re, bbp_im = block_diag(bb_re, same_group), block_diag(bb_im, same_group)
    cp_re = block_diag(c_re.astype(F32), same_group.T)
    cp_im = block_diag(c_im.astype(F32), same_group.T)
    al_re, al_im = pw_re[:, :L, None, :], pw_im[:, :L, None, :]
    wl_re = al_re * bbp_re[:, None] - al_im * bbp_im[:, None]
    wl_im = al_re * bbp_im[:, None] + al_im * bbp_re[:, None]
    kern = (jnp.einsum('klcs,ksd->klcd', wl_re, cp_re, precision=hi)
            - jnp.einsum('klcs,ksd->klcd', wl_im, cp_im, precision=hi))
    kern = jnp.concatenate([jnp.zeros_like(kern), kern], axis=1)
    toep = jnp.stack([kern[:, L - j:2 * L - j].transpose(0, 2, 1, 3).reshape(K, n_c, L * n_c)
                      for j in range(L)], axis=1).reshape(K, L * n_c, L * n_c)
    w = jnp.concatenate([wl_re[:, ::-1].reshape(K, L * n_c, n_s),
                         wl_im[:, ::-1].reshape(K, L * n_c, n_s)], axis=2)
    p1_re, p1_im = pw_re[:, 1:, :, None], pw_im[:, 1:, :, None]
    v_re = cp_re[:, None] * p1_re - cp_im[:, None] * p1_im
    v_im = -(cp_re[:, None] * p1_im + cp_im[:, None] * p1_re)
    v = (jnp.concatenate([v_re, v_im], axis=2).transpose(0, 2, 1, 3)
         .reshape(K, 2 * n_s, L * n_c))
    al = jnp.stack([pw_re[:, L], pw_im[:, L]], axis=1)
    dvec = jnp.tile(d_skip.astype(F32).reshape(K, 1, n_c), (1, L, 1)).reshape(K, 1, L * n_c)
    return toep.astype(BF16), w.astype(BF16), v.astype(BF16), al, dvec


def _ssm_kernel(u_ref, toep_ref, w_ref, v_ref, al_ref, d_ref, y_ref, lhs_s, e_s, s_s, carry_s):
    bsz, tt, lanes = u_ref.shape
    L = SSM_CHUNK
    n_chunks = tt // L
    ns = al_ref.shape[2]

    @pl.when(pl.program_id(1) == 0)
    def _():
        carry_s[...] = jnp.zeros_like(carry_s)

    for b in range(bsz):
        for t in range(L):
            lhs_s[t, pl.ds(b, n_chunks, stride=bsz), :] = u_ref[b, pl.ds(t, n_chunks, stride=L), :]
    lhs = jnp.concatenate([lhs_s[t] for t in range(L)], axis=1)
    lb = lhs.astype(BF16)
    e_s[...] = jnp.dot(lb, w_ref[0], preferred_element_type=F32)
    ar = jnp.broadcast_to(al_ref[0, 0:1, :], (bsz, ns))
    ai = jnp.broadcast_to(al_ref[0, 1:2, :], (bsz, ns))
    sr = carry_s[0]
    si = carry_s[1]
    for c in range(n_chunks):
        rows = slice(c * bsz, (c + 1) * bsz)
        s_s[rows, :ns] = sr
        s_s[rows, ns:] = si
        er = e_s[rows, :ns]
        ei = e_s[rows, ns:]
        sr, si = ar * sr - ai * si + er, ar * si + ai * sr + ei
    carry_s[0] = sr
    carry_s[1] = si
    y = (jnp.dot(lb, toep_ref[0], preferred_element_type=F32)
         + jnp.dot(s_s[...].astype(BF16), v_ref[0], preferred_element_type=F32)
         + d_ref[0] * lhs)
    yg = jax.nn.gelu(y)
    for t in range(L):
        lhs_s[t] = yg[:, t * lanes:(t + 1) * lanes]
    for b in range(bsz):
        for t in range(L):
            y_ref[b, pl.ds(t, n_chunks, stride=L), :] = lhs_s[t, pl.ds(b, n_chunks, stride=bsz), :]


def _ssm(u3, mats):
    toep, w, v, al, dvec = mats
    bsz, seq, sw = u3.shape
    k = toep.shape[0]
    lanes = sw // k
    tt = SSM_TILE
    rows = bsz * tt // SSM_CHUNK
    cl = SSM_CHUNK * lanes
    ns = al.shape[2]
    pack = lambda p, t: (p, 0, 0)
    tile = lambda p, t: (0, t, p)
    return pl.pallas_call(
        _ssm_kernel,
        grid=(k, seq // tt),
        in_specs=[
            pl.BlockSpec((bsz, tt, lanes), tile),
            pl.BlockSpec((1, cl, cl), pack),
            pl.BlockSpec((1, cl, 2 * ns), pack),
            pl.BlockSpec((1, 2 * ns, cl), pack),
            pl.BlockSpec((1, 2, ns), pack),
            pl.BlockSpec((1, 1, cl), pack),
        ],
        out_specs=pl.BlockSpec((bsz, tt, lanes), tile),
        out_shape=jax.ShapeDtypeStruct((bsz, seq, sw), F32),
        scratch_shapes=[
            pltpu.VMEM((SSM_CHUNK, rows, lanes), F32),
            pltpu.VMEM((rows, 2 * ns), F32),
            pltpu.VMEM((rows, 2 * ns), F32),
            pltpu.VMEM((2, bsz, ns), F32),
        ],
        compiler_params=_cparams(2),
        name="ssm",
    )(u3, toep, w, v, al, dvec)


def _mix_kernel(tiles_per_seq, x_ref, up_ref, halo_ref, ys_ref, gate_ref, poolw_ref, pscale_ref,
                gluw_ref, glub_ref, wpu_ref, wsu_ref, wout_ref, gffn_ref, rwh_ref, rwl_ref, rb_ref,
                tri_ref,
                x1_ref, h3_ref, idx_ref, gw_ref, rank_ref, cnt_ref, ext_s, carry_s):
    i = pl.program_id(0)
    j = i % tiles_per_seq
    tm = x_ref.shape[0]
    d_model = x_ref.shape[1]
    gdim = poolw_ref.shape[1]

    @pl.when(i == 0)
    def _():
        carry_s[...] = jnp.zeros_like(carry_s)

    ext_s[0:POOL_HALO, :] = jnp.where(j == 0, 0.0, halo_ref[...])
    ext_s[POOL_HALO:, :] = up_ref[...]
    pos = (j * tm + 1 + lax.broadcasted_iota(jnp.int32, (tm, 1), 0)).astype(F32)
    parts = []
    for g, w in enumerate(POOL_WINDOWS):
        cols = slice(g * gdim, (g + 1) * gdim)
        s = ext_s[:, cols]
        k = 1
        while k < w:
            s = s + pltpu.roll(s, k, 0)
            k *= 2
        s = s[POOL_HALO:POOL_HALO + tm]
        cur = ext_s[POOL_HALO:POOL_HALO + tm, cols]
        dlt = (s / jnp.minimum(pos, float(w)) - cur).astype(BF16)
        parts.append(jnp.dot(dlt, poolw_ref[g], preferred_element_type=F32))
    yp = jnp.concatenate(parts, axis=1) * pscale_ref[...]
    y_pool = jnp.dot(yp.astype(BF16), wpu_ref[...], preferred_element_type=F32)

    yg = ys_ref[...].astype(BF16)
    lin = jnp.dot(yg, gluw_ref[...], preferred_element_type=F32) + glub_ref[...]
    glu = yg * jax.nn.sigmoid(lin.astype(BF16))
    y_ssm = jnp.dot(glu, wsu_ref[...], preferred_element_type=F32)

    z = (gate_ref[:, :d_model] * y_pool.astype(BF16) + gate_ref[:, d_model:] * y_ssm.astype(BF16))
    x1 = x_ref[...] + jnp.dot(z, wout_ref[...], preferred_element_type=F32)
    x1_ref[...] = x1

    ms = jnp.mean(x1 * x1, axis=-1, keepdims=True)
    h2 = x1 * lax.rsqrt(ms + RMS_EPS) * gffn_ref[...]
    n_s = d_model // LANES
    for s in range(n_s):
        h3_ref[pl.ds(s, tm, stride=n_s), :] = h2[:, s * LANES:(s + 1) * LANES]

    h_hi = h2.astype(BF16)
    h_lo = (h2 - h_hi.astype(F32)).astype(BF16)
    nt = (((1,), (1,)), ((), ()))
    logits = (lax.dot_general(rwh_ref[...], h_hi, nt, preferred_element_type=F32)
              + lax.dot_general(rwh_ref[...], h_lo, nt, preferred_element_type=F32)
              + lax.dot_general(rwl_ref[...], h_hi, nt, preferred_element_type=F32)
              + rb_ref[...])
    n_e = logits.shape[0]
    iota_e = lax.broadcasted_iota(jnp.int32, (n_e, tm), 0)
    l = logits
    tops, hots = [], []
    for k in range(TOP_K):
        m = jnp.max(l, axis=0, keepdims=True)
        idx = jnp.min(jnp.where(l == m, iota_e, n_e), axis=0, keepdims=True)
        hot = iota_e == idx
        l = jnp.where(hot, -jnp.inf, l)
        tops.append(m)
        hots.append(hot)
        idx_ref[k:k + 1, :] = idx
    exps = [jnp.exp(m - tops[0]) for m in tops]
    den = exps[0] + exps[1] + exps[2] + exps[3]
    for k in range(TOP_K):
        gw_ref[k:k + 1, :] = exps[k] / den

    multi = sum(h.astype(F32) for h in hots)
    cum = jnp.dot(multi.astype(BF16), tri_ref[...], preferred_element_type=F32) + carry_s[:, 0:1]
    for k in range(TOP_K):
        rk = jnp.sum(jnp.where(hots[k], cum, 0.0), axis=0, keepdims=True)
        rank_ref[k:k + 1, :] = rk.astype(jnp.int32)
    carry_s[...] = carry_s[...] + jnp.sum(multi, axis=1, keepdims=True)
    cnt_ref[...] = carry_s[...]


def _mix(x2, u_pool, y_ssm, gates, pool_w, pool_scale, glu_w, glu_b, w_pool_up, w_ssm_up, w_out,
         g_ffn, rw_t, rb, seq_len):
    m, d = x2.shape
    pw = u_pool.shape[1]
    sw = y_ssm.shape[1]
    tm = ROW_TILE
    n_e = rw_t.shape[0]
    rw_hi = rw_t.astype(BF16)
    rw_lo = (rw_t - rw_hi.astype(F32)).astype(BF16)
    tiles_per_seq = seq_len // tm
    tri = (jnp.arange(tm)[:, None] < jnp.arange(tm)[None, :]).astype(BF16)
    row = lambda i: (i, 0)
    const2 = lambda i: (0, 0)
    const3 = lambda i: (0, 0, 0)
    col = lambda i: (0, i)
    halo = lambda i: (jnp.maximum(i * (tm // POOL_HALO) - 1, 0), 0)
    return pl.pallas_call(
        functools.partial(_mix_kernel, tiles_per_seq),
        grid=(m // tm,),
        in_specs=[
            pl.BlockSpec((tm, d), row),
            pl.BlockSpec((tm, pw), row),
            pl.BlockSpec((POOL_HALO, pw), halo),
            pl.BlockSpec((tm, sw), row),
            pl.BlockSpec((tm, 2 * d), row),
            pl.BlockSpec(pool_w.shape, const3),
            pl.BlockSpec((1, pw), const2),
            pl.BlockSpec((sw, sw), const2),
            pl.BlockSpec((1, sw), const2),
            pl.BlockSpec((pw, d), const2),
            pl.BlockSpec((sw, d), const2),
            pl.BlockSpec((d, d), const2),
            pl.BlockSpec((1, d), const2),
            pl.BlockSpec((n_e, d), const2),
            pl.BlockSpec((n_e, d), const2),
            pl.BlockSpec((n_e, 1), const2),
            pl.BlockSpec((tm, tm), const2),
        ],
        out_specs=[
            pl.BlockSpec((tm, d), row),
            pl.BlockSpec((tm * (d // LANES), LANES), row),
            pl.BlockSpec((TOP_K, tm), col),
            pl.BlockSpec((TOP_K, tm), col),
            pl.BlockSpec((TOP_K, tm), col),
            pl.BlockSpec((n_e, LANES), const2),
        ],
        out_shape=[
            jax.ShapeDtypeStruct((m, d), F32),
            jax.ShapeDtypeStruct((m * (d // LANES), LANES), F32),
            jax.ShapeDtypeStruct((TOP_K, m), jnp.int32),
            jax.ShapeDtypeStruct((TOP_K, m), F32),
            jax.ShapeDtypeStruct((TOP_K, m), jnp.int32),
            jax.ShapeDtypeStruct((n_e, LANES), F32),
        ],
        scratch_shapes=[
            pltpu.VMEM((POOL_HALO + tm, pw), F32),
            pltpu.VMEM((n_e, LANES), F32),
        ],
        compiler_params=_cparams(1),
        name="mix_route",
    )(x2, u_pool, u_pool, y_ssm, gates, pool_w, pool_scale, glu_w, glu_b, w_pool_up, w_ssm_up,
      w_out, g_ffn, rw_hi, rw_lo, rb, tri)


def _row_copy(src, dst, sem):
    return pltpu.make_async_copy(src, dst, sem)


def _for_each_run_piece(tab, tb, max_rows, n_s, fn):
    def per_expert(e, carry):
        n = tab[tb + e]
        off = tab[tb + N_EXPERTS + e]
        slot = tab[tb + 2 * N_EXPERTS + e]
        def piece(b):
            size = 1 << b

            @pl.when(((n >> b) & 1) == 1)
            def _():
                done = n & (size - 1)
                fn(pl.multiple_of((slot + done) * n_s, n_s), pl.multiple_of((off + done) * n_s, n_s),
                   size * n_s)

        n_bits = max_rows.bit_length()
        low_bits = min(n_bits, (2 * TOP_K * max_rows // N_EXPERTS - 1).bit_length())
        for b in range(low_bits):
            piece(b)

        @pl.when((n >> low_bits) != 0)
        def _():
            for b in range(low_bits, n_bits):
                piece(b)
        return carry

    lax.fori_loop(0, N_EXPERTS, per_expert, 0)


def _dispatch_kernel(n_s, tab_ref, lpos_ref, zpos_ref, h3_ref, xs_ref, st0_s, st1_s, sem):
    j = pl.program_id(0)
    tm = h3_ref.shape[0] // (2 * n_s)
    per_tile = TOP_K * tm
    zrows = MOE_ROWS * n_s

    @pl.when(j == 0)
    def _():
        st0_s[0:zrows, :] = jnp.zeros((zrows, st0_s.shape[1]), F32)

        def fill(start):
            cp = _row_copy(st0_s.at[pl.ds(0, zrows)],
                           xs_ref.at[pl.ds(pl.multiple_of(start * n_s, n_s), zrows)], sem.at[0])
            cp.start()
            cp.wait()

        for e in range(N_EXPERTS):
            fill(zpos_ref[e])

        def tail(t, carry):
            fill(t * MOE_ROWS)
            return carry

        lax.fori_loop(zpos_ref[N_EXPERTS], xs_ref.shape[0] // zrows, tail, 0)

    def drain(stage, slot):
        _row_copy(stage, xs_ref.at[pl.ds(0, per_tile * n_s)], sem.at[slot]).wait()

    def scatter(stage, slot):
        @pl.when(j > 0)
        def _():
            drain(stage, slot)

        base = slot * per_tile

        def tokens(g, carry):
            for u in range(COMBINE_UNROLL):
                t = g * COMBINE_UNROLL + u
                rows = h3_ref[pl.ds(pl.multiple_of((slot * tm + t) * n_s, n_s), n_s), :]
                for k in range(TOP_K):
                    row = pl.multiple_of(lpos_ref[base + k * tm + t], n_s)
                    stage[pl.ds(row, n_s), :] = rows
            return carry

        lax.fori_loop(0, tm // COMBINE_UNROLL, tokens, 0)
        _for_each_run_piece(
            tab_ref, slot * RUN_TAB, tm, n_s,
            lambda s, b, r: _row_copy(stage.at[pl.ds(b, r)], xs_ref.at[pl.ds(s, r)],
                                      sem.at[slot]).start())

    scatter(st0_s, 0)
    scatter(st1_s, 1)

    @pl.when(j + 1 == pl.num_programs(0))
    def _():
        drain(st0_s, 0)
        drain(st1_s, 1)


def _dispatch(tab, lpos_tiles, zpos, h3, n_slots, n_s):
    rows, lanes = h3.shape
    tm = COMBINE_TILE
    smem = pltpu.SMEM
    return pl.pallas_call(
        functools.partial(_dispatch_kernel, n_s),
        grid=(rows // (2 * tm * n_s),),
        in_specs=[
            pl.BlockSpec((2 * RUN_TAB,), lambda j: (j,), memory_space=smem),
            pl.BlockSpec((2 * TOP_K * tm,), lambda j: (j,), memory_space=smem),
            pl.BlockSpec(memory_space=smem),
            pl.BlockSpec((2 * tm * n_s, lanes), lambda j: (j, 0)),
        ],
        out_specs=pl.BlockSpec(memory_space=pl.ANY),
        out_shape=jax.ShapeDtypeStruct(((n_slots + MOE_ROWS) * n_s, lanes), F32),
        scratch_shapes=[pltpu.VMEM((TOP_K * tm * n_s, lanes), F32),
                        pltpu.VMEM((TOP_K * tm * n_s, lanes), F32),
                        pltpu.SemaphoreType.DMA((2,))],
        compiler_params=_cparams(1),
        name="dispatch",
    )(tab, lpos_tiles, zpos, h3)


def _expert_kernel(blk_e_ref, first_ref, next_ref, nused_ref, xs_ref, w1_hbm, b1_ref, w2_hbm, b2_ref,
                   perm_ref, ys_ref, w1f_s, w2f_s, w1p_s, w2b_s, sem):
    i = pl.program_id(0)
    n_s = w1f_s.shape[0] // LANES
    rows = xs_ref.shape[0] // n_s
    f2 = w1f_s.shape[1]

    def fetch(e):
        return (_row_copy(w1_hbm.at[e], w1f_s, sem.at[0]), _row_copy(w2_hbm.at[e], w2f_s, sem.at[1]))

    @pl.when(i == 0)
    def _():
        for cp in fetch(blk_e_ref[0]):
            cp.start()

    @pl.when(first_ref[i] == 1)
    def _():
        for cp in fetch(blk_e_ref[i]):
            cp.wait()
        for c in range(f2 // DEINT):
            cols = slice(c * DEINT, (c + 1) * DEINT)
            strip = w1f_s[:, cols].astype(BF16)
            w1p_s[:, cols] = jnp.dot(strip, perm_ref[...], preferred_element_type=F32).astype(BF16)
        w2b_s[...] = w2f_s[...].astype(BF16)

        @pl.when(next_ref[i] >= 0)
        def _():
            for cp in fetch(next_ref[i]):
                cp.start()

    @pl.when(i < nused_ref[0])
    def _():
        x = jnp.concatenate([xs_ref[pl.ds(s, rows, stride=n_s), :] for s in range(n_s)],
                            axis=1).astype(BF16)
        h = jnp.dot(x, w1p_s[...], preferred_element_type=F32) + b1_ref[0]
        acts = []
        for c in range(f2 // DEINT):
            xg = jnp.minimum(h[:, c * DEINT:c * DEINT + LANES], SWIGLU_LIMIT)
            xl = jnp.clip(h[:, c * DEINT + LANES:(c + 1) * DEINT], -SWIGLU_LIMIT, SWIGLU_LIMIT)
            acts.append((xg * jax.nn.sigmoid(SWIGLU_ALPHA * xg) * (xl + 1.0)).astype(BF16))
        act = jnp.concatenate(acts, axis=1)
        y = jnp.dot(act, w2b_s[...], preferred_element_type=F32) + b2_ref[0]
        for s in range(n_s):
            ys_ref[pl.ds(s, rows, stride=n_s), :] = y[:, s * LANES:(s + 1) * LANES]

    @pl.when(i >= nused_ref[0])
    def _():
        ys_ref[...] = jnp.zeros_like(ys_ref)


def _experts(blk_e, first, nxt, n_used, xs, w1, b1p, w2, b2, n_blocks):
    _, lanes = xs.shape
    _, d, f2 = w1.shape
    blk = MOE_ROWS * (d // lanes)
    f = w2.shape[1]
    half = jnp.arange(DEINT) // 2 + (jnp.arange(DEINT) % 2) * LANES
    perm = (half[:, None] == jnp.arange(DEINT)[None, :]).astype(BF16)
    xmap = lambda i, be, fi, nx, nu: (jnp.minimum(i, nu[0] - 1), 0)
    emap = lambda i, be, fi, nx, nu: (be[i], 0, 0)
    grid_spec = pltpu.PrefetchScalarGridSpec(
        num_scalar_prefetch=4,
        grid=(n_blocks,),
        in_specs=[
            pl.BlockSpec((blk, lanes), xmap),
            pl.BlockSpec(memory_space=pl.ANY),
            pl.BlockSpec((1, 1, f2), emap),
            pl.BlockSpec(memory_space=pl.ANY),
            pl.BlockSpec((1, 1, d), emap),
            pl.BlockSpec((DEINT, DEINT), lambda i, be, fi, nx, nu: (0, 0)),
        ],
        out_specs=pl.BlockSpec((blk, lanes), lambda i, be, fi, nx, nu: (i, 0)),
        scratch_shapes=[pltpu.VMEM((d, f2), F32), pltpu.VMEM((f, d), F32),
                        pltpu.VMEM((d, f2), BF16), pltpu.VMEM((f, d), BF16),
                        pltpu.SemaphoreType.DMA((2,))],
    )
    return pl.pallas_call(
        _expert_kernel,
        grid_spec=grid_spec,
        out_shape=jax.ShapeDtypeStruct((n_blocks * blk, lanes), F32),
        compiler_params=_cparams(1),
        name="experts",
    )(blk_e, first, nxt, n_used, xs, w1, b1p, w2, b2, perm)


def _combine_kernel(tab_ref, nxt_ref, lpos_ref, gw_ref, x1_ref, gfin_ref, ys_ref, out_ref,
                    buf0_s, buf1_s, acc_s, sem):
    j = pl.program_id(0)
    tm = x1_ref.shape[0] // 2
    n_s = x1_ref.shape[1] // LANES
    per_tile = TOP_K * tm
    tab_w = RUN_TAB

    def gather(tab, tb, buf, slot):
        _for_each_run_piece(
            tab, tb, tm, n_s,
            lambda s, b, r: _row_copy(ys_ref.at[pl.ds(s, r)], buf.at[pl.ds(b, r)],
                                      sem.at[slot]).start())

    def reduce(buf, slot, t0):
        _row_copy(ys_ref.at[pl.ds(0, per_tile * n_s)], buf, sem.at[slot]).wait()
        base = (t0 // tm) * per_tile

        def tokens(g, carry):
            for u in range(COMBINE_UNROLL):
                t = g * COMBINE_UNROLL + u
                acc = None
                for k in range(TOP_K):
                    row = pl.multiple_of(lpos_ref[base + k * tm + t], n_s)
                    term = gw_ref[base + k * tm + t] * buf[pl.ds(row, n_s), :]
                    acc = term if acc is None else acc + term
                acc_s[pl.ds(pl.multiple_of(t * n_s, n_s), n_s), :] = acc
            return carry

        lax.fori_loop(0, tm // COMBINE_UNROLL, tokens, 0)
        y = jnp.concatenate([acc_s[pl.ds(s, tm, stride=n_s), :] for s in range(n_s)], axis=1)
        acc = x1_ref[t0:t0 + tm, :] + y
        ms = jnp.mean(acc * acc, axis=-1, keepdims=True)
        out_ref[t0:t0 + tm, :] = acc * lax.rsqrt(ms + RMS_EPS) * gfin_ref[...]

    @pl.when(j == 0)
    def _():
        gather(tab_ref, 0, buf0_s, 0)

    gather(tab_ref, tab_w, buf1_s, 1)
    reduce(buf0_s, 0, 0)

    @pl.when(j + 1 < pl.num_programs(0))
    def _():
        gather(nxt_ref, 0, buf0_s, 0)

    reduce(buf1_s, 1, tm)


def _combine(tab, lpos_tiles, gw_tiles, x1, g_final, ys):
    m, d = x1.shape
    _, lanes = ys.shape
    s = d // lanes
    tm = COMBINE_TILE
    n_pairs = m // (2 * tm)
    tab_w = RUN_TAB
    smem = pltpu.SMEM
    return pl.pallas_call(
        _combine_kernel,
        grid=(n_pairs,),
        in_specs=[
            pl.BlockSpec((2 * tab_w,), lambda j: (j,), memory_space=smem),
            pl.BlockSpec((tab_w,), lambda j: (jnp.minimum(2 * j + 2, 2 * n_pairs - 1),),
                         memory_space=smem),
            pl.BlockSpec((2 * TOP_K * tm,), lambda j: (j,), memory_space=smem),
            pl.BlockSpec((2 * TOP_K * tm,), lambda j: (j,), memory_space=smem),
            pl.BlockSpec((2 * tm, d), lambda j: (j, 0)),
            pl.BlockSpec((1, d), lambda j: (0, 0)),
            pl.BlockSpec(memory_space=pl.ANY),
        ],
        out_specs=pl.BlockSpec((2 * tm, d), lambda j: (j, 0)),
        out_shape=jax.ShapeDtypeStruct((m, d), F32),
        scratch_shapes=[pltpu.VMEM((TOP_K * tm * s, lanes), F32),
                        pltpu.VMEM((TOP_K * tm * s, lanes), F32),
                        pltpu.VMEM((tm * s, lanes), F32),
                        pltpu.SemaphoreType.DMA((2,))],
        compiler_params=_cparams(1),
        name="combine",
    )(tab, tab, lpos_tiles, gw_tiles, x1, g_final, ys)


def kernel(x, norm_mix_g, w_in, b_gate, pool_w, pool_scale, ssm_lambda_re, ssm_lambda_im, ssm_log_dt, ssm_b_re, ssm_b_im, ssm_c_re, ssm_c_im, ssm_d, ssm_glu_w, ssm_glu_b, w_pool_up, w_ssm_up, w_out, norm_ffn_g, router_w, router_b, moe_w1, moe_b1, moe_w2, moe_b2, norm_final_g):
    bsz, seq, d = x.shape
    depth = w_in.shape[0]
    pw = pool_w.shape[1] * pool_w.shape[2]
    n_groups = ssm_lambda_re.shape[1]
    sw = n_groups * SSM_GROUP_DIM
    m = bsz * seq
    assert depth == 1
    assert bsz == SUBLANES and seq % ROW_TILE == 0 and seq % SSM_TILE == 0 and d % LANES == 0
    assert pool_w.shape[1] == len(POOL_WINDOWS) and n_groups % SSM_PACK == 0
    assert all(w & (w - 1) == 0 and w <= POOL_HALO for w in POOL_WINDOWS)
    assert moe_w1.shape[3] % DEINT == 0

    x2 = x.reshape(m, d)
    for l in range(depth):
        wi = w_in[l].astype(BF16)
        u_pool, u_ssm, gates = _inproj(
            x2, norm_mix_g[l][None], wi[:, :pw], wi[:, pw:pw + sw], wi[:, pw + sw:], b_gate[l][None])
        mats = _ssm_matrices(ssm_lambda_re[l], ssm_lambda_im[l], ssm_log_dt[l], ssm_b_re[l],
                             ssm_b_im[l], ssm_c_re[l], ssm_c_im[l], ssm_d[l])
        y_ssm = _ssm(u_ssm.reshape(bsz, seq, sw), mats).reshape(m, sw)
        x1, h3, idx_t, gw_t, rank_t, cnt = _mix(
            x2, u_pool, y_ssm, gates, pool_w[l].astype(BF16), pool_scale[l][None],
            ssm_glu_w[l].astype(BF16), ssm_glu_b[l][None], w_pool_up[l].astype(BF16),
            w_ssm_up[l].astype(BF16), w_out[l].astype(BF16), norm_ffn_g[l][None],
            router_w[l].T, router_b[l][:, None], seq)

        n_assign = m * TOP_K
        n_blocks = (n_assign + N_EXPERTS * (MOE_ROWS - 1) + MOE_ROWS - 1) // MOE_ROWS
        n_slots = n_blocks * MOE_ROWS
        counts = cnt[:, 0].astype(jnp.int32)
        padded = (counts + MOE_ROWS - 1) // MOE_ROWS * MOE_ROWS
        pad_end = jnp.cumsum(padded)
        pad_start = pad_end - padded
        n_used = (pad_end[-1] // MOE_ROWS).astype(jnp.int32)[None]
        blk_row = jnp.arange(n_blocks, dtype=jnp.int32) * MOE_ROWS
        blk_e = jnp.minimum(jnp.sum(pad_end[None, :] <= blk_row[:, None], axis=1),
                            N_EXPERTS - 1).astype(jnp.int32)
        blk_ids = jnp.arange(n_blocks, dtype=jnp.int32)
        first = jnp.concatenate([jnp.ones((1,), jnp.int32),
                                 (blk_e[1:] != blk_e[:-1]).astype(jnp.int32)])
        first = jnp.where(blk_ids < n_used[0], first, 0)
        ex = jnp.arange(N_EXPERTS, dtype=jnp.int32)
        later = (ex[None, :] > ex[:, None]) & (counts[None, :] > 0)
        next_e = jnp.min(jnp.where(later, ex[None, :], N_EXPERTS), axis=1)
        next_e = jnp.where(next_e < N_EXPERTS, next_e, -1).astype(jnp.int32)
        nxt = jnp.sum(jnp.where(blk_e[:, None] == ex[None, :], next_e[None, :], 0), axis=1)
        e_ids = jnp.arange(N_EXPERTS, dtype=jnp.int32)[:, None, None]
        zpos = jnp.concatenate([pad_start + counts, n_used]).astype(jnp.int32)
        ct = COMBINE_TILE
        n_ct = m // ct
        cnt_te = jnp.sum((idx_t.reshape(TOP_K, n_ct, ct)[None] == e_ids[..., None]).astype(jnp.int32),
                         axis=(1, 3))
        carry_te = jnp.cumsum(cnt_te, axis=1) - cnt_te
        offs_te = jnp.cumsum(cnt_te, axis=0) - cnt_te
        tab = jnp.stack([cnt_te, offs_te, pad_start[:, None] + carry_te, jnp.zeros_like(cnt_te)],
                        axis=0)
        tab = tab.transpose(2, 0, 1).reshape(-1).astype(jnp.int32)
        delta = jnp.repeat(offs_te - carry_te, ct, axis=1)
        lpos = rank_t + jnp.sum(jnp.where(idx_t[None] == e_ids, delta[:, None, :], 0), axis=0)
        tile_major = lambda a: a.reshape(TOP_K, n_ct, ct).transpose(1, 0, 2).reshape(-1)
        lpos_tiles = tile_major((lpos * (d // LANES)).astype(jnp.int32))

        xs = _dispatch(tab, lpos_tiles, zpos, h3, n_slots, d // LANES)
        f2 = moe_w1.shape[3]
        b1p = (moe_b1[l].reshape(N_EXPERTS, f2 // DEINT, LANES, 2)
               .transpose(0, 1, 3, 2).reshape(N_EXPERTS, 1, f2))
        ys = _experts(blk_e, first, nxt, n_used, xs, moe_w1[l], b1p, moe_w2[l], moe_b2[l][:, None, :],
                      n_blocks)
        x2 = _combine(tab, lpos_tiles, tile_major(gw_t), x1, norm_final_g[None], ys)
    return x2.reshape(bsz, seq, d)
```

```python
import functools
import math

import jax
import jax.numpy as jnp
from jax import lax
from jax.experimental import pallas as pl
from jax.experimental.pallas import tpu as pltpu

F32 = jnp.float32
BF16 = jnp.bfloat16

RMS_EPS = 1e-6
POOL_WINDOWS = (2, 4, 8, 16)
POOL_HALO = 16
SSM_GROUP_DIM = 16
SSM_STATE = 64
N_EXPERTS = 32
TOP_K = 4
SWIGLU_ALPHA = 1.702
SWIGLU_LIMIT = 7.0

LANES = 128
SUBLANES = 8
SSM_CHUNK = SUBLANES
SSM_PACK = LANES // SSM_GROUP_DIM
SSM_TILE = 1024
DEINT = 2 * LANES
ROW_TILE = 512
INPROJ_TILE = 1024
COMBINE_TILE = 512
COMBINE_UNROLL = 16
RUN_TAB = 4 * N_EXPERTS
MOE_ROWS = 512
VMEM_LIMIT = 56 * 1024 * 1024


def _cparams(n_axes):
    return pltpu.CompilerParams(
        dimension_semantics=("arbitrary",) * n_axes, vmem_limit_bytes=VMEM_LIMIT)


def _inproj_kernel(x_ref, g_ref, wp_ref, ws_ref, wg_ref, bg_ref, up_ref, us_ref, gate_ref):
    x = x_ref[...]
    ms = jnp.mean(x * x, axis=-1, keepdims=True)
    h = (x * lax.rsqrt(ms + RMS_EPS) * g_ref[...]).astype(BF16)
    up_ref[...] = jnp.dot(h, wp_ref[...], preferred_element_type=F32)
    us_ref[...] = jnp.dot(h, ws_ref[...], preferred_element_type=F32)
    gl = jnp.dot(h, wg_ref[...], preferred_element_type=F32) + bg_ref[...]
    gate_ref[...] = jax.nn.sigmoid(gl).astype(BF16)


def _inproj(x2, g, wp, ws, wg, bg):
    m, d = x2.shape
    pw, sw, gw = wp.shape[1], ws.shape[1], wg.shape[1]
    tm = INPROJ_TILE
    const = lambda i: (0, 0)
    row = lambda i: (i, 0)
    return pl.pallas_call(
        _inproj_kernel,
        grid=(m // tm,),
        in_specs=[
            pl.BlockSpec((tm, d), row),
            pl.BlockSpec((1, d), const),
            pl.BlockSpec((d, pw), const),
            pl.BlockSpec((d, sw), const),
            pl.BlockSpec((d, gw), const),
            pl.BlockSpec((1, gw), const),
        ],
        out_specs=[
            pl.BlockSpec((tm, pw), row),
            pl.BlockSpec((tm, sw), row),
            pl.BlockSpec((tm, gw), row),
        ],
        out_shape=[
            jax.ShapeDtypeStruct((m, pw), F32),
            jax.ShapeDtypeStruct((m, sw), F32),
            jax.ShapeDtypeStruct((m, gw), BF16),
        ],
        compiler_params=_cparams(1),
        name="inproj",
    )(x2, g, wp, ws, wg, bg)


def _ssm_matrices(lam_re, lam_im, log_dt, b_re, b_im, c_re, c_im, d_skip):
    hi = lax.Precision.HIGHEST
    L, P, N, GP = SSM_CHUNK, SSM_GROUP_DIM, SSM_STATE, SSM_PACK
    G = lam_re.shape[0]
    K = G // GP
    lr, li = lam_re.astype(F32), lam_im.astype(F32)
    dt = jnp.exp(log_dt.astype(F32))[:, None]
    mag = jnp.exp(lr * dt)
    lb_re, lb_im = mag * jnp.cos(li * dt), mag * jnp.sin(li * dt)
    den = lr * lr + li * li
    xr, xi = lb_re - 1.0, lb_im
    f_re = (xr * lr + xi * li) / den
    f_im = (xi * lr - xr * li) / den
    br, bi = b_re.astype(F32), b_im.astype(F32)
    bb_re = f_re[..., None] * br - f_im[..., None] * bi
    bb_im = f_re[..., None] * bi + f_im[..., None] * br
    n_c, n_s = GP * P, GP * N
    tau = jnp.arange(L + 1, dtype=F32)[None, :, None]
    pmag = jnp.exp((lr * dt).reshape(K, 1, n_s) * tau)
    pang = (li * dt).reshape(K, 1, n_s) * tau
    pw_re, pw_im = pmag * jnp.cos(pang), pmag * jnp.sin(pang)
    same_group = (jnp.arange(n_c)[:, None] // P) == (jnp.arange(n_s)[None, :] // N)

    def block_diag(m, mask):
        rows = m.transpose(0, 2, 1).reshape(K, GP * m.shape[2], m.shape[1])
        return jnp.where(mask, jnp.tile(rows, (1, 1, GP)), 0.0)

    bbp_re, bbp_im = block_diag(bb_re, same_group), block_diag(bb_im, same_group)
    cp_re = block_diag(c_re.astype(F32), same_group.T)
    cp_im = block_diag(c_im.astype(F32), same_group.T)
    al_re, al_im = pw_re[:, :L, None, :], pw_im[:, :L, None, :]
    wl_re = al_re * bbp_re[:, None] - al_im * bbp_im[:, None]
    wl_im = al_re * bbp_im[:, None] + al_im * bbp_re[:, None]
    kern = (jnp.einsum('klcs,ksd->klcd', wl_re, cp_re, precision=hi)
            - jnp.einsum('klcs,ksd->klcd', wl_im, cp_im, precision=hi))
    kern = jnp.concatenate([jnp.zeros_like(kern), kern], axis=1)
    toep = jnp.stack([kern[:, L - j:2 * L - j].transpose(0, 2, 1, 3).reshape(K, n_c, L * n_c)
                      for j in range(L)], axis=1).reshape(K, L * n_c, L * n_c)
    w = jnp.concatenate([wl_re[:, ::-1].reshape(K, L * n_c, n_s),
                         wl_im[:, ::-1].reshape(K, L * n_c, n_s)], axis=2)
    p1_re, p1_im = pw_re[:, 1:, :, None], pw_im[:, 1:, :, None]
    v_re = cp_re[:, None] * p1_re - cp_im[:, None] * p1_im
    v_im = -(cp_re[:, None] * p1_im + cp_im[:, None] * p1_re)
    v = (jnp.concatenate([v_re, v_im], axis=2).transpose(0, 2, 1, 3)
         .reshape(K, 2 * n_s, L * n_c))
    al = jnp.stack([pw_re[:, L], pw_im[:, L]], axis=1)
    dvec = jnp.tile(d_skip.astype(F32).reshape(K, 1, n_c), (1, L, 1)).reshape(K, 1, L * n_c)
    return toep.astype(BF16), w.astype(BF16), v.astype(BF16), al, dvec


def _ssm_kernel(u_ref, toep_ref, w_ref, v_ref, al_ref, d_ref, y_ref, lhs_s, e_s, s_s, carry_s):
    bsz, tt, lanes = u_ref.shape
    L = SSM_CHUNK
    n_chunks = tt // L
    ns = al_ref.shape[2]

    @pl.when(pl.program_id(1) == 0)
    def _():
        carry_s[...] = jnp.zeros_like(carry_s)

    for b in range(bsz):
        for t in range(L):
            lhs_s[t, pl.ds(b, n_chunks, stride=bsz), :] = u_ref[b, pl.ds(t, n_chunks, stride=L), :]
    lhs = jnp.concatenate([lhs_s[t] for t in range(L)], axis=1)
    lb = lhs.astype(BF16)
    e_s[...] = jnp.dot(lb, w_ref[0], preferred_element_type=F32)
    ar = jnp.broadcast_to(al_ref[0, 0:1, :], (bsz, ns))
    ai = jnp.broadcast_to(al_ref[0, 1:2, :], (bsz, ns))
    sr = carry_s[0]
    si = carry_s[1]
    for c in range(n_chunks):
        rows = slice(c * bsz, (c + 1) * bsz)
        s_s[rows, :ns] = sr
        s_s[rows, ns:] = si
        er = e_s[rows, :ns]
        ei = e_s[rows, ns:]
        sr, si = ar * sr - ai * si + er, ar * si + ai * sr + ei
    carry_s[0] = sr
    carry_s[1] = si
    y = (jnp.dot(lb, toep_ref[0], preferred_element_type=F32)
         + jnp.dot(s_s[...].astype(BF16), v_ref[0], preferred_element_type=F32)
         + d_ref[0] * lhs)
    yg = jax.nn.gelu(y)
    for t in range(L):
        lhs_s[t] = yg[:, t * lanes:(t + 1) * lanes]
    for b in range(bsz):
        for t in range(L):
            y_ref[b, pl.ds(t, n_chunks, stride=L), :] = lhs_s[t, pl.ds(b, n_chunks, stride=bsz), :]


def _ssm(u3, mats):
    toep, w, v, al, dvec = mats
    bsz, seq, sw = u3.shape
    k = toep.shape[0]
    lanes = sw // k
    tt = SSM_TILE
    rows = bsz * tt // SSM_CHUNK
    cl = SSM_CHUNK * lanes
    ns = al.shape[2]
    pack = lambda p, t: (p, 0, 0)
    tile = lambda p, t: (0, t, p)
    return pl.pallas_call(
        _ssm_kernel,
        grid=(k, seq // tt),
        in_specs=[
            pl.BlockSpec((bsz, tt, lanes), tile),
            pl.BlockSpec((1, cl, cl), pack),
            pl.BlockSpec((1, cl, 2 * ns), pack),
            pl.BlockSpec((1, 2 * ns, cl), pack),
            pl.BlockSpec((1, 2, ns), pack),
            pl.BlockSpec((1, 1, cl), pack),
        ],
        out_specs=pl.BlockSpec((bsz, tt, lanes), tile),
        out_shape=jax.ShapeDtypeStruct((bsz, seq, sw), F32),
        scratch_shapes=[
            pltpu.VMEM((SSM_CHUNK, rows, lanes), F32),
            pltpu.VMEM((rows, 2 * ns), F32),
            pltpu.VMEM((rows, 2 * ns), F32),
            pltpu.VMEM((2, bsz, ns), F32),
        ],
        compiler_params=_cparams(2),
        name="ssm",
    )(u3, toep, w, v, al, dvec)


def _mix_kernel(tiles_per_seq, x_ref, up_ref, halo_ref, ys_ref, gate_ref, poolw_ref, pscale_ref,
                gluw_ref, glub_ref, wpu_ref, wsu_ref, wout_ref, gffn_ref, rwh_ref, rwl_ref, rb_ref,
                tri_ref,
                x1_ref, h3_ref, idx_ref, gw_ref, rank_ref, cnt_ref, ext_s, carry_s):
    i = pl.program_id(0)
    j = i % tiles_per_seq
    tm = x_ref.shape[0]
    d_model = x_ref.shape[1]
    gdim = poolw_ref.shape[1]

    @pl.when(i == 0)
    def _():
        carry_s[...] = jnp.zeros_like(carry_s)

    ext_s[0:POOL_HALO, :] = jnp.where(j == 0, 0.0, halo_ref[...])
    ext_s[POOL_HALO:, :] = up_ref[...]
    pos = (j * tm + 1 + lax.broadcasted_iota(jnp.int32, (tm, 1), 0)).astype(F32)
    parts = []
    for g, w in enumerate(POOL_WINDOWS):
        cols = slice(g * gdim, (g + 1) * gdim)
        s = ext_s[:, cols]
        k = 1
        while k < w:
            s = s + pltpu.roll(s, k, 0)
            k *= 2
        s = s[POOL_HALO:POOL_HALO + tm]
        cur = ext_s[POOL_HALO:POOL_HALO + tm, cols]
        dlt = (s / jnp.minimum(pos, float(w)) - cur).astype(BF16)
        parts.append(jnp.dot(dlt, poolw_ref[g], preferred_element_type=F32))
    yp = jnp.concatenate(parts, axis=1) * pscale_ref[...]
    y_pool = jnp.dot(yp.astype(BF16), wpu_ref[...], preferred_element_type=F32)

    yg = ys_ref[...].astype(BF16)
    lin = jnp.dot(yg, gluw_ref[...], preferred_element_type=F32) + glub_ref[...]
    glu = yg * jax.nn.sigmoid(lin.astype(BF16))
    y_ssm = jnp.dot(glu, wsu_ref[...], preferred_element_type=F32)

    z = (gate_ref[:, :d_model] * y_pool.astype(BF16) + gate_ref[:, d_model:] * y_ssm.astype(BF16))
    x1 = x_ref[...] + jnp.dot(z, wout_ref[...], preferred_element_type=F32)
    x1_ref[...] = x1

    ms = jnp.mean(x1 * x1, axis=-1, keepdims=True)
    h2 = x1 * lax.rsqrt(ms + RMS_EPS) * gffn_ref[...]
    n_s = d_model // LANES
    for s in range(n_s):
        h3_ref[pl.ds(s, tm, stride=n_s), :] = h2[:, s * LANES:(s + 1) * LANES]

    h_hi = h2.astype(BF16)
    h_lo = (h2 - h_hi.astype(F32)).astype(BF16)
    nt = (((1,), (1,)), ((), ()))
    logits = (lax.dot_general(rwh_ref[...], h_hi, nt, preferred_element_type=F32)
              + lax.dot_general(rwh_ref[...], h_lo, nt, preferred_element_type=F32)
              + lax.dot_general(rwl_ref[...], h_hi, nt, preferred_element_type=F32)
              + rb_ref[...])
    n_e = logits.shape[0]
    iota_e = lax.broadcasted_iota(jnp.int32, (n_e, tm), 0)
    l = logits
    tops, hots = [], []
    for k in range(TOP_K):
        m = jnp.max(l, axis=0, keepdims=True)
        idx = jnp.min(jnp.where(l == m, iota_e, n_e), axis=0, keepdims=True)
        hot = iota_e == idx
        l = jnp.where(hot, -jnp.inf, l)
        tops.append(m)
        hots.append(hot)
        idx_ref[k:k + 1, :] = idx
    exps = [jnp.exp(m - tops[0]) for m in tops]
    den = exps[0] + exps[1] + exps[2] + exps[3]
    for k in range(TOP_K):
        gw_ref[k:k + 1, :] = exps[k] / den

    multi = sum(h.astype(F32) for h in hots)
    cum = jnp.dot(multi.astype(BF16), tri_ref[...], preferred_element_type=F32) + carry_s[:, 0:1]
    for k in range(TOP_K):
        rk = jnp.sum(jnp.where(hots[k], cum, 0.0), axis=0, keepdims=True)
        rank_ref[k:k + 1, :] = rk.astype(jnp.int32)
    carry_s[...] = carry_s[...] + jnp.sum(multi, axis=1, keepdims=True)
    cnt_ref[...] = carry_s[...]


def _mix(x2, u_pool, y_ssm, gates, pool_w, pool_scale, glu_w, glu_b, w_pool_up, w_ssm_up, w_out,
         g_ffn, rw_t, rb, seq_len):
    m, d = x2.shape
    pw = u_pool.shape[1]
    sw = y_ssm.shape[1]
    tm = ROW_TILE
    n_e = rw_t.shape[0]
    rw_hi = rw_t.astype(BF16)
    rw_lo = (rw_t - rw_hi.astype(F32)).astype(BF16)
    tiles_per_seq = seq_len // tm
    tri = (jnp.arange(tm)[:, None] < jnp.arange(tm)[None, :]).astype(BF16)
    row = lambda i: (i, 0)
    const2 = lambda i: (0, 0)
    const3 = lambda i: (0, 0, 0)
    col = lambda i: (0, i)
    halo = lambda i: (jnp.maximum(i * (tm // POOL_HALO) - 1, 0), 0)
    return pl.pallas_call(
        functools.partial(_mix_kernel, tiles_per_seq),
        grid=(m // tm,),
        in_specs=[
            pl.BlockSpec((tm, d), row),
            pl.BlockSpec((tm, pw), row),
            pl.BlockSpec((POOL_HALO, pw), halo),
            pl.BlockSpec((tm, sw), row),
            pl.BlockSpec((tm, 2 * d), row),
            pl.BlockSpec(pool_w.shape, const3),
            pl.BlockSpec((1, pw), const2),
            pl.BlockSpec((sw, sw), const2),
            pl.BlockSpec((1, sw), const2),
            pl.BlockSpec((pw, d), const2),
            pl.BlockSpec((sw, d), const2),
            pl.BlockSpec((d, d), const2),
            pl.BlockSpec((1, d), const2),
            pl.BlockSpec((n_e, d), const2),
            pl.BlockSpec((n_e, d), const2),
            pl.BlockSpec((n_e, 1), const2),
            pl.BlockSpec((tm, tm), const2),
        ],
        out_specs=[
            pl.BlockSpec((tm, d), row),
            pl.BlockSpec((tm * (d // LANES), LANES), row),
            pl.BlockSpec((TOP_K, tm), col),
            pl.BlockSpec((TOP_K, tm), col),
            pl.BlockSpec((TOP_K, tm), col),
            pl.BlockSpec((n_e, LANES), const2),
        ],
        out_shape=[
            jax.ShapeDtypeStruct((m, d), F32),
            jax.ShapeDtypeStruct((m * (d // LANES), LANES), F32),
            jax.ShapeDtypeStruct((TOP_K, m), jnp.int32),
            jax.ShapeDtypeStruct((TOP_K, m), F32),
            jax.ShapeDtypeStruct((TOP_K, m), jnp.int32),
            jax.ShapeDtypeStruct((n_e, LANES), F32),
        ],
        scratch_shapes=[
            pltpu.VMEM((POOL_HALO + tm, pw), F32),
            pltpu.VMEM((n_e, LANES), F32),
        ],
        compiler_params=_cparams(1),
        name="mix_route",
    )(x2, u_pool, u_pool, y_ssm, gates, pool_w, pool_scale, glu_w, glu_b, w_pool_up, w_ssm_up,
      w_out, g_ffn, rw_hi, rw_lo, rb, tri)


def _row_copy(src, dst, sem):
    return pltpu.make_async_copy(src, dst, sem)


def _for_each_run_piece(tab, tb, max_rows, n_s, fn):
    def per_expert(e, carry):
        n = tab[tb + e]
        off = tab[tb + N_EXPERTS + e]
        slot = tab[tb + 2 * N_EXPERTS + e]
        def piece(b):
            size = 1 << b

            @pl.when(((n >> b) & 1) == 1)
            def _():
                done = n & (size - 1)
                fn(pl.multiple_of((slot + done) * n_s, n_s), pl.multiple_of((off + done) * n_s, n_s),
                   size * n_s)

        n_bits = max_rows.bit_length()
        low_bits = min(n_bits, (2 * TOP_K * max_rows // N_EXPERTS - 1).bit_length())
        for b in range(low_bits):
            piece(b)

        @pl.when((n >> low_bits) != 0)
        def _():
            for b in range(low_bits, n_bits):
                piece(b)
        return carry

    lax.fori_loop(0, N_EXPERTS, per_expert, 0)


def _dispatch_kernel(n_s, tab_ref, lpos_ref, zpos_ref, h3_ref, xs_ref, st0_s, st1_s, sem):
    j = pl.program_id(0)
    tm = h3_ref.shape[0] // (2 * n_s)
    per_tile = TOP_K * tm
    zrows = MOE_ROWS * n_s

    @pl.when(j == 0)
    def _():
        st0_s[0:zrows, :] = jnp.zeros((zrows, st0_s.shape[1]), F32)

        def fill(start):
            cp = _row_copy(st0_s.at[pl.ds(0, zrows)],
                           xs_ref.at[pl.ds(pl.multiple_of(start * n_s, n_s), zrows)], sem.at[0])
            cp.start()
            cp.wait()

        for e in range(N_EXPERTS):
            fill(zpos_ref[e])

        def tail(t, carry):
            fill(t * MOE_ROWS)
            return carry

        lax.fori_loop(zpos_ref[N_EXPERTS], xs_ref.shape[0] // zrows, tail, 0)

    def drain(stage, slot):
        _row_copy(stage, xs_ref.at[pl.ds(0, per_tile * n_s)], sem.at[slot]).wait()

    def scatter(stage, slot):
        @pl.when(j > 0)
        def _():
            drain(stage, slot)

        base = slot * per_tile

        def tokens(g, carry):
            for u in range(COMBINE_UNROLL):
                t = g * COMBINE_UNROLL + u
                rows = h3_ref[pl.ds(pl.multiple_of((slot * tm + t) * n_s, n_s), n_s), :]
                for k in range(TOP_K):
                    row = pl.multiple_of(lpos_ref[base + k * tm + t], n_s)
                    stage[pl.ds(row, n_s), :] = rows
            return carry

        lax.fori_loop(0, tm // COMBINE_UNROLL, tokens, 0)
        _for_each_run_piece(
            tab_ref, slot * RUN_TAB, tm, n_s,
            lambda s, b, r: _row_copy(stage.at[pl.ds(b, r)], xs_ref.at[pl.ds(s, r)],
                                      sem.at[slot]).start())

    scatter(st0_s, 0)
    scatter(st1_s, 1)

    @pl.when(j + 1 == pl.num_programs(0))
    def _():
        drain(st0_s, 0)
        drain(st1_s, 1)


def _dispatch(tab, lpos_tiles, zpos, h3, n_slots, n_s):
    rows, lanes = h3.shape
    tm = COMBINE_TILE
    smem = pltpu.SMEM
    return pl.pallas_call(
        functools.partial(_dispatch_kernel, n_s),
        grid=(rows // (2 * tm * n_s),),
        in_specs=[
            pl.BlockSpec((2 * RUN_TAB,), lambda j: (j,), memory_space=smem),
            pl.BlockSpec((2 * TOP_K * tm,), lambda j: (j,), memory_space=smem),
            pl.BlockSpec(memory_space=smem),
            pl.BlockSpec((2 * tm * n_s, lanes), lambda j: (j, 0)),
        ],
        out_specs=pl.BlockSpec(memory_space=pl.ANY),
        out_shape=jax.ShapeDtypeStruct(((n_slots + MOE_ROWS) * n_s, lanes), F32),
        scratch_shapes=[pltpu.VMEM((TOP_K * tm * n_s, lanes), F32),
                        pltpu.VMEM((TOP_K * tm * n_s, lanes), F32),
                        pltpu.SemaphoreType.DMA((2,))],
        compiler_params=_cparams(1),
        name="dispatch",
    )(tab, lpos_tiles, zpos, h3)


def _expert_kernel(blk_e_ref, first_ref, next_ref, nused_ref, xs_ref, w1_hbm, b1_ref, w2_hbm, b2_ref,
                   perm_ref, ys_ref, w1f_s, w2f_s, w1p_s, w2b_s, sem):
    i = pl.program_id(0)
    n_s = w1f_s.shape[0] // LANES
    rows = xs_ref.shape[0] // n_s
    f2 = w1f_s.shape[1]

    def fetch(e):
        return (_row_copy(w1_hbm.at[e], w1f_s, sem.at[0]), _row_copy(w2_hbm.at[e], w2f_s, sem.at[1]))

    @pl.when(i == 0)
    def _():
        for cp in fetch(blk_e_ref[0]):
            cp.start()

    @pl.when(first_ref[i] == 1)
    def _():
        for cp in fetch(blk_e_ref[i]):
            cp.wait()
        for c in range(f2 // DEINT):
            cols = slice(c * DEINT, (c + 1) * DEINT)
            strip = w1f_s[:, cols].astype(BF16)
            w1p_s[:, cols] = jnp.dot(strip, perm_ref[...], preferred_element_type=F32).astype(BF16)
        w2b_s[...] = w2f_s[...].astype(BF16)

        @pl.when(next_ref[i] >= 0)
        def _():
            for cp in fetch(next_ref[i]):
                cp.start()

    @pl.when(i < nused_ref[0])
    def _():
        x = jnp.concatenate([xs_ref[pl.ds(s, rows, stride=n_s), :] for s in range(n_s)],
                            axis=1).astype(BF16)
        h = jnp.dot(x, w1p_s[...], preferred_element_type=F32) + b1_ref[0]
        acts = []
        for c in range(f2 // DEINT):
            xg = jnp.minimum(h[:, c * DEINT:c * DEINT + LANES], SWIGLU_LIMIT)
            xl = jnp.clip(h[:, c * DEINT + LANES:(c + 1) * DEINT], -SWIGLU_LIMIT, SWIGLU_LIMIT)
            acts.append((xg * jax.nn.sigmoid(SWIGLU_ALPHA * xg) * (xl + 1.0)).astype(BF16))
        act = jnp.concatenate(acts, axis=1)
        y = jnp.dot(act, w2b_s[...], preferred_element_type=F32) + b2_ref[0]
        for s in range(n_s):
            ys_ref[pl.ds(s, rows, stride=n_s), :] = y[:, s * LANES:(s + 1) * LANES]

    @pl.when(i >= nused_ref[0])
    def _():
        ys_ref[...] = jnp.zeros_like(ys_ref)


def _experts(blk_e, first, nxt, n_used, xs, w1, b1p, w2, b2, n_blocks):
    _, lanes = xs.shape
    _, d, f2 = w1.shape
    blk = MOE_ROWS * (d // lanes)
    f = w2.shape[1]
    half = jnp.arange(DEINT) // 2 + (jnp.arange(DEINT) % 2) * LANES
    perm = (half[:, None] == jnp.arange(DEINT)[None, :]).astype(BF16)
    xmap = lambda i, be, fi, nx, nu: (jnp.minimum(i, nu[0] - 1), 0)
    emap = lambda i, be, fi, nx, nu: (be[i], 0, 0)
    grid_spec = pltpu.PrefetchScalarGridSpec(
        num_scalar_prefetch=4,
        grid=(n_blocks,),
        in_specs=[
            pl.BlockSpec((blk, lanes), xmap),
            pl.BlockSpec(memory_space=pl.ANY),
            pl.BlockSpec((1, 1, f2), emap),
            pl.BlockSpec(memory_space=pl.ANY),
            pl.BlockSpec((1, 1, d), emap),
            pl.BlockSpec((DEINT, DEINT), lambda i, be, fi, nx, nu: (0, 0)),
        ],
        out_specs=pl.BlockSpec((blk, lanes), lambda i, be, fi, nx, nu: (i, 0)),
        scratch_shapes=[pltpu.VMEM((d, f2), F32), pltpu.VMEM((f, d), F32),
                        pltpu.VMEM((d, f2), BF16), pltpu.VMEM((f, d), BF16),
                        pltpu.SemaphoreType.DMA((2,))],
    )
    return pl.pallas_call(
        _expert_kernel,
        grid_spec=grid_spec,
        out_shape=jax.ShapeDtypeStruct((n_blocks * blk, lanes), F32),
        compiler_params=_cparams(1),
        name="experts",
    )(blk_e, first, nxt, n_used, xs, w1, b1p, w2, b2, perm)


def _combine_kernel(tab_ref, nxt_ref, lpos_ref, gw_ref, x1_ref, gfin_ref, ys_ref, out_ref,
                    buf0_s, buf1_s, acc_s, sem):
    j = pl.program_id(0)
    tm = x1_ref.shape[0] // 2
    n_s = x1_ref.shape[1] // LANES
    per_tile = TOP_K * tm
    tab_w = RUN_TAB

    def gather(tab, tb, buf, slot):
        _for_each_run_piece(
            tab, tb, tm, n_s,
            lambda s, b, r: _row_copy(ys_ref.at[pl.ds(s, r)], buf.at[pl.ds(b, r)],
                                      sem.at[slot]).start())

    def reduce(buf, slot, t0):
        _row_copy(ys_ref.at[pl.ds(0, per_tile * n_s)], buf, sem.at[slot]).wait()
        base = (t0 // tm) * per_tile

        def tokens(g, carry):
            for u in range(COMBINE_UNROLL):
                t = g * COMBINE_UNROLL + u
                acc = None
                for k in range(TOP_K):
                    row = pl.multiple_of(lpos_ref[base + k * tm + t], n_s)
                    term = gw_ref[base + k * tm + t] * buf[pl.ds(row, n_s), :]
                    acc = term if acc is None else acc + term
                acc_s[pl.ds(pl.multiple_of(t * n_s, n_s), n_s), :] = acc
            return carry

        lax.fori_loop(0, tm // COMBINE_UNROLL, tokens, 0)
        y = jnp.concatenate([acc_s[pl.ds(s, tm, stride=n_s), :] for s in range(n_s)], axis=1)
        acc = x1_ref[t0:t0 + tm, :] + y
        ms = jnp.mean(acc * acc, axis=-1, keepdims=True)
        out_ref[t0:t0 + tm, :] = acc * lax.rsqrt(ms + RMS_EPS) * gfin_ref[...]

    @pl.when(j == 0)
    def _():
        gather(tab_ref, 0, buf0_s, 0)

    gather(tab_ref, tab_w, buf1_s, 1)
    reduce(buf0_s, 0, 0)

    @pl.when(j + 1 < pl.num_programs(0))
    def _():
        gather(nxt_ref, 0, buf0_s, 0)

    reduce(buf1_s, 1, tm)


def _combine(tab, lpos_tiles, gw_tiles, x1, g_final, ys):
    m, d = x1.shape
    _, lanes = ys.shape
    s = d // lanes
    tm = COMBINE_TILE
    n_pairs = m // (2 * tm)
    tab_w = RUN_TAB
    smem = pltpu.SMEM
    return pl.pallas_call(
        _combine_kernel,
        grid=(n_pairs,),
        in_specs=[
            pl.BlockSpec((2 * tab_w,), lambda j: (j,), memory_space=smem),
            pl.BlockSpec((tab_w,), lambda j: (jnp.minimum(2 * j + 2, 2 * n_pairs - 1),),
                         memory_space=smem),
            pl.BlockSpec((2 * TOP_K * tm,), lambda j: (j,), memory_space=smem),
            pl.BlockSpec((2 * TOP_K * tm,), lambda j: (j,), memory_space=smem),
            pl.BlockSpec((2 * tm, d), lambda j: (j, 0)),
            pl.BlockSpec((1, d), lambda j: (0, 0)),
            pl.BlockSpec(memory_space=pl.ANY),
        ],
        out_specs=pl.BlockSpec((2 * tm, d), lambda j: (j, 0)),
        out_shape=jax.ShapeDtypeStruct((m, d), F32),
        scratch_shapes=[pltpu.VMEM((TOP_K * tm * s, lanes), F32),
                        pltpu.VMEM((TOP_K * tm * s, lanes), F32),
                        pltpu.VMEM((tm * s, lanes), F32),
                        pltpu.SemaphoreType.DMA((2,))],
        compiler_params=_cparams(1),
        name="combine",
    )(tab, tab, lpos_tiles, gw_tiles, x1, g_final, ys)


def kernel(x, norm_mix_g, w_in, b_gate, pool_w, pool_scale, ssm_lambda_re, ssm_lambda_im, ssm_log_dt, ssm_b_re, ssm_b_im, ssm_c_re, ssm_c_im, ssm_d, ssm_glu_w, ssm_glu_b, w_pool_up, w_ssm_up, w_out, norm_ffn_g, router_w, router_b, moe_w1, moe_b1, moe_w2, moe_b2, norm_final_g):
    bsz, seq, d = x.shape
    depth = w_in.shape[0]
    pw = pool_w.shape[1] * pool_w.shape[2]
    n_groups = ssm_lambda_re.shape[1]
    sw = n_groups * SSM_GROUP_DIM
    m = bsz * seq
    assert depth == 1
    assert bsz == SUBLANES and seq % ROW_TILE == 0 and seq % SSM_TILE == 0 and d % LANES == 0
    assert pool_w.shape[1] == len(POOL_WINDOWS) and n_groups % SSM_PACK == 0
    assert all(w & (w - 1) == 0 and w <= POOL_HALO for w in POOL_WINDOWS)
    assert moe_w1.shape[3] % DEINT == 0

    x2 = x.reshape(m, d)
    for l in range(depth):
        wi = w_in[l].astype(BF16)
        u_pool, u_ssm, gates = _inproj(
            x2, norm_mix_g[l][None], wi[:, :pw], wi[:, pw:pw + sw], wi[:, pw + sw:], b_gate[l][None])
        mats = _ssm_matrices(ssm_lambda_re[l], ssm_lambda_im[l], ssm_log_dt[l], ssm_b_re[l],
                             ssm_b_im[l], ssm_c_re[l], ssm_c_im[l], ssm_d[l])
        y_ssm = _ssm(u_ssm.reshape(bsz, seq, sw), mats).reshape(m, sw)
        x1, h3, idx_t, gw_t, rank_t, cnt = _mix(
            x2, u_pool, y_ssm, gates, pool_w[l].astype(BF16), pool_scale[l][None],
            ssm_glu_w[l].astype(BF16), ssm_glu_b[l][None], w_pool_up[l].astype(BF16),
            w_ssm_up[l].astype(BF16), w_out[l].astype(BF16), norm_ffn_g[l][None],
            router_w[l].T, router_b[l][:, None], seq)

        n_assign = m * TOP_K
        n_blocks = (n_assign + N_EXPERTS * (MOE_ROWS - 1) + MOE_ROWS - 1) // MOE_ROWS
        n_slots = n_blocks * MOE_ROWS
        counts = cnt[:, 0].astype(jnp.int32)
        padded = (counts + MOE_ROWS - 1) // MOE_ROWS * MOE_ROWS
        pad_end = jnp.cumsum(padded)
        pad_start = pad_end - padded
        n_used = (pad_end[-1] // MOE_ROWS).astype(jnp.int32)[None]
        blk_row = jnp.arange(n_blocks, dtype=jnp.int32) * MOE_ROWS
        blk_e = jnp.minimum(jnp.sum(pad_end[None, :] <= blk_row[:, None], axis=1),
                            N_EXPERTS - 1).astype(jnp.int32)
        blk_ids = jnp.arange(n_blocks, dtype=jnp.int32)
        first = jnp.concatenate([jnp.ones((1,), jnp.int32),
                                 (blk_e[1:] != blk_e[:-1]).astype(jnp.int32)])
        first = jnp.where(blk_ids < n_used[0], first, 0)
        ex = jnp.arange(N_EXPERTS, dtype=jnp.int32)
        later = (ex[None, :] > ex[:, None]) & (counts[None, :] > 0)
        next_e = jnp.min(jnp.where(later, ex[None, :], N_EXPERTS), axis=1)
        next_e = jnp.where(next_e < N_EXPERTS, next_e, -1).astype(jnp.int32)
        nxt = jnp.sum(jnp.where(blk_e[:, None] == ex[None, :], next_e[None, :], 0), axis=1)
        e_ids = jnp.arange(N_EXPERTS, dtype=jnp.int32)[:, None, None]
        zpos = jnp.concatenate([pad_start + counts, n_used]).astype(jnp.int32)
        ct = COMBINE_TILE
        n_ct = m // ct
        cnt_te = jnp.sum((idx_t.reshape(TOP_K, n_ct, ct)[None] == e_ids[..., None]).astype(jnp.int32),
                         axis=(1, 3))
        carry_te = jnp.cumsum(cnt_te, axis=1) - cnt_te
        offs_te = jnp.cumsum(cnt_te, axis=0) - cnt_te
        tab = jnp.stack([cnt_te, offs_te, pad_start[:, None] + carry_te, jnp.zeros_like(cnt_te)],
                        axis=0)
        tab = tab.transpose(2, 0, 1).reshape(-1).astype(jnp.int32)
        delta = jnp.repeat(offs_te - carry_te, ct, axis=1)
        lpos = rank_t + jnp.sum(jnp.where(idx_t[None] == e_ids, delta[:, None, :], 0), axis=0)
        tile_major = lambda a: a.reshape(TOP_K, n_ct, ct).transpose(1, 0, 2).reshape(-1)
        lpos_tiles = tile_major((lpos * (d // LANES)).astype(jnp.int32))

        xs = _dispatch(tab, lpos_tiles, zpos, h3, n_slots, d // LANES)
        f2 = moe_w1.shape[3]
        b1p = (moe_b1[l].reshape(N_EXPERTS, f2 // DEINT, LANES, 2)
               .transpose(0, 1, 3, 2).reshape(N_EXPERTS, 1, f2))
        ys = _experts(blk_e, first, nxt, n_used, xs, moe_w1[l], b1p, moe_w2[l], moe_b2[l][:, None, :],
                      n_blocks)
        x2 = _combine(tab, lpos_tiles, tile_major(gw_t), x1, norm_final_g[None], ys)
    return x2.reshape(bsz, seq, d)
```

```python
import functools

import jax
import jax.numpy as jnp
from jax import lax
from jax.experimental import pallas as pl
from jax.experimental.pallas import tpu as pltpu

F32 = jnp.float32
BF16 = jnp.bfloat16

RMS_EPS = 1e-6
POOL_WINDOWS = (2, 4, 8, 16)
POOL_HALO = 16
SSM_GROUP_DIM = 16
SSM_STATE = 64
N_EXPERTS = 32
TOP_K = 4
SWIGLU_ALPHA = 1.702
SWIGLU_LIMIT = 7.0

LANES = 128
SUBLANES = 8
SSM_CHUNK = SUBLANES
SSM_PACK = LANES // SSM_GROUP_DIM
SSM_TILE = 512
DEINT = 2 * LANES
ROW_TILE = 512
INPROJ_TILE = 1024
COMBINE_TILE = 512
COMBINE_UNROLL = 8
RUN_TAB = 4 * N_EXPERTS
MOE_ROWS = 512
VMEM_LIMIT = 56 * 1024 * 1024


def _cparams(n_axes):
    return pltpu.CompilerParams(
        dimension_semantics=("arbitrary",) * n_axes, vmem_limit_bytes=VMEM_LIMIT)


def _inproj_kernel(x_ref, g_ref, wp_ref, ws_ref, wg_ref, bg_ref, up_ref, us_ref, gate_ref):
    x = x_ref[...]
    ms = jnp.mean(x * x, axis=-1, keepdims=True)
    h = (x * lax.rsqrt(ms + RMS_EPS) * g_ref[...]).astype(BF16)
    up_ref[...] = jnp.dot(h, wp_ref[...], preferred_element_type=F32)
    us_ref[...] = jnp.dot(h, ws_ref[...], preferred_element_type=F32)
    gl = jnp.dot(h, wg_ref[...], preferred_element_type=F32) + bg_ref[...]
    gate_ref[...] = jax.nn.sigmoid(gl).astype(BF16)


def _inproj(x2, g, wp, ws, wg, bg):
    m, d = x2.shape
    pw, sw, gw = wp.shape[1], ws.shape[1], wg.shape[1]
    tm = INPROJ_TILE
    const = lambda i: (0, 0)
    row = lambda i: (i, 0)
    return pl.pallas_call(
        _inproj_kernel,
        grid=(m // tm,),
        in_specs=[
            pl.BlockSpec((tm, d), row),
            pl.BlockSpec((1, d), const),
            pl.BlockSpec((d, pw), const),
            pl.BlockSpec((d, sw), const),
            pl.BlockSpec((d, gw), const),
            pl.BlockSpec((1, gw), const),
        ],
        out_specs=[
            pl.BlockSpec((tm, pw), row),
            pl.BlockSpec((tm, sw), row),
            pl.BlockSpec((tm, gw), row),
        ],
        out_shape=[
            jax.ShapeDtypeStruct((m, pw), F32),
            jax.ShapeDtypeStruct((m, sw), F32),
            jax.ShapeDtypeStruct((m, gw), BF16),
        ],
        compiler_params=_cparams(1),
        name="inproj",
    )(x2, g, wp, ws, wg, bg)


def _ssm_matrices(lam_re, lam_im, log_dt, b_re, b_im, c_re, c_im, d_skip):
    hi = lax.Precision.HIGHEST
    L, P, N, GP = SSM_CHUNK, SSM_GROUP_DIM, SSM_STATE, SSM_PACK
    G = lam_re.shape[0]
    K = G // GP
    lr, li = lam_re.astype(F32), lam_im.astype(F32)
    dt = jnp.exp(log_dt.astype(F32))[:, None]
    mag = jnp.exp(lr * dt)
    lb_re, lb_im = mag * jnp.cos(li * dt), mag * jnp.sin(li * dt)
    den = lr * lr + li * li
    xr, xi = lb_re - 1.0, lb_im
    f_re = (xr * lr + xi * li) / den
    f_im = (xi * lr - xr * li) / den
    br, bi = b_re.astype(F32), b_im.astype(F32)
    bb_re = f_re[..., None] * br - f_im[..., None] * bi
    bb_im = f_re[..., None] * bi + f_im[..., None] * br
    n_c, n_s = GP * P, GP * N
    tau = jnp.arange(L + 1, dtype=F32)[None, :, None]
    pmag = jnp.exp((lr * dt).reshape(K, 1, n_s) * tau)
    pang = (li * dt).reshape(K, 1, n_s) * tau
    pw_re, pw_im = pmag * jnp.cos(pang), pmag * jnp.sin(pang)
    same_group = (jnp.arange(n_c)[:, None] // P) == (jnp.arange(n_s)[None, :] // N)

    def block_diag(m, mask):
        rows = m.transpose(0, 2, 1).reshape(K, GP * m.shape[2], m.shape[1])
        return jnp.where(mask, jnp.tile(rows, (1, 1, GP)), 0.0)

    bbp_re, bbp_im = block_diag(bb_re, same_group), block_diag(bb_im, same_group)
    cp_re = block_diag(c_re.astype(F32), same_group.T)
    cp_im = block_diag(c_im.astype(F32), same_group.T)
    al_re, al_im = pw_re[:, :L, None, :], pw_im[:, :L, None, :]
    wl_re = al_re * bbp_re[:, None] - al_im * bbp_im[:, None]
    wl_im = al_re * bbp_im[:, None] + al_im * bbp_re[:, None]
    kern = (jnp.einsum('klcs,ksd->klcd', wl_re, cp_re, precision=hi)
            - jnp.einsum('klcs,ksd->klcd', wl_im, cp_im, precision=hi))
    kern = jnp.concatenate([jnp.zeros_like(kern), kern], axis=1)
    toep = jnp.stack([kern[:, L - j:2 * L - j].transpose(0, 2, 1, 3).reshape(K, n_c, L * n_c)
                      for j in range(L)], axis=1).reshape(K, L * n_c, L * n_c)
    w = jnp.concatenate([wl_re[:, ::-1].reshape(K, L * n_c, n_s),
                         wl_im[:, ::-1].reshape(K, L * n_c, n_s)], axis=2)
    p1_re, p1_im = pw_re[:, 1:, :, None], pw_im[:, 1:, :, None]
    v_re = cp_re[:, None] * p1_re - cp_im[:, None] * p1_im
    v_im = -(cp_re[:, None] * p1_im + cp_im[:, None] * p1_re)
    v = (jnp.concatenate([v_re, v_im], axis=2).transpose(0, 2, 1, 3)
         .reshape(K, 2 * n_s, L * n_c))
    al = jnp.stack([pw_re[:, L], pw_im[:, L]], axis=1)
    dvec = jnp.tile(d_skip.astype(F32).reshape(K, 1, n_c), (1, L, 1)).reshape(K, 1, L * n_c)
    return toep.astype(BF16), w.astype(BF16), v.astype(BF16), al, dvec


def _ssm_kernel(u_ref, toep_ref, w_ref, v_ref, al_ref, d_ref, y_ref, lhs_s, e_s, s_s, carry_s):
    bsz, tt, lanes = u_ref.shape
    L = SSM_CHUNK
    n_chunks = tt // L
    ns = al_ref.shape[2]

    @pl.when(pl.program_id(1) == 0)
    def _():
        carry_s[...] = jnp.zeros_like(carry_s)

    for b in range(bsz):
        for t in range(L):
            lhs_s[t, pl.ds(b, n_chunks, stride=bsz), :] = u_ref[b, pl.ds(t, n_chunks, stride=L), :]
    lhs = jnp.concatenate([lhs_s[t] for t in range(L)], axis=1)
    lb = lhs.astype(BF16)
    e_s[...] = jnp.dot(lb, w_ref[0], preferred_element_type=F32)
    ar = jnp.broadcast_to(al_ref[0, 0:1, :], (bsz, ns))
    ai = jnp.broadcast_to(al_ref[0, 1:2, :], (bsz, ns))
    sr = carry_s[0]
    si = carry_s[1]
    for c in range(n_chunks):
        rows = slice(c * bsz, (c + 1) * bsz)
        s_s[rows, :ns] = sr
        s_s[rows, ns:] = si
        er = e_s[rows, :ns]
        ei = e_s[rows, ns:]
        sr, si = ar * sr - ai * si + er, ar * si + ai * sr + ei
    carry_s[0] = sr
    carry_s[1] = si
    y = (jnp.dot(lb, toep_ref[0], preferred_element_type=F32)
         + jnp.dot(s_s[...].astype(BF16), v_ref[0], preferred_element_type=F32)
         + d_ref[0] * lhs)
    yg = jax.nn.gelu(y)
    for t in range(L):
        lhs_s[t] = yg[:, t * lanes:(t + 1) * lanes]
    for b in range(bsz):
        for t in range(L):
            y_ref[b, pl.ds(t, n_chunks, stride=L), :] = lhs_s[t, pl.ds(b, n_chunks, stride=bsz), :]


def _ssm(u3, mats):
    toep, w, v, al, dvec = mats
    bsz, seq, sw = u3.shape
    k = toep.shape[0]
    lanes = sw // k
    tt = SSM_TILE
    rows = bsz * tt // SSM_CHUNK
    cl = SSM_CHUNK * lanes
    ns = al.shape[2]
    pack = lambda p, t: (p, 0, 0)
    tile = lambda p, t: (0, t, p)
    return pl.pallas_call(
        _ssm_kernel,
        grid=(k, seq // tt),
        in_specs=[
            pl.BlockSpec((bsz, tt, lanes), tile),
            pl.BlockSpec((1, cl, cl), pack),
            pl.BlockSpec((1, cl, 2 * ns), pack),
            pl.BlockSpec((1, 2 * ns, cl), pack),
            pl.BlockSpec((1, 2, ns), pack),
            pl.BlockSpec((1, 1, cl), pack),
        ],
        out_specs=pl.BlockSpec((bsz, tt, lanes), tile),
        out_shape=jax.ShapeDtypeStruct((bsz, seq, sw), F32),
        scratch_shapes=[
            pltpu.VMEM((SSM_CHUNK, rows, lanes), F32),
            pltpu.VMEM((rows, 2 * ns), F32),
            pltpu.VMEM((rows, 2 * ns), F32),
            pltpu.VMEM((2, bsz, ns), F32),
        ],
        compiler_params=_cparams(2),
        name="ssm",
    )(u3, toep, w, v, al, dvec)


def _mix_kernel(tiles_per_seq, x_ref, up_ref, halo_ref, ys_ref, gate_ref, poolw_ref, pscale_ref,
                gluw_ref, glub_ref, wpu_ref, wsu_ref, wout_ref, gffn_ref, rwh_ref, rwl_ref, rb_ref,
                tri_ref,
                x1_ref, h3_ref, idx_ref, gw_ref, rank_ref, cnt_ref, ext_s, carry_s):
    i = pl.program_id(0)
    j = i % tiles_per_seq
    tm = x_ref.shape[0]
    d_model = x_ref.shape[1]
    gdim = poolw_ref.shape[1]

    @pl.when(i == 0)
    def _():
        carry_s[...] = jnp.zeros_like(carry_s)

    ext_s[0:POOL_HALO, :] = jnp.where(j == 0, 0.0, halo_ref[...])
    ext_s[POOL_HALO:, :] = up_ref[...]
    pos = (j * tm + 1 + lax.broadcasted_iota(jnp.int32, (tm, 1), 0)).astype(F32)
    parts = []
    for g, w in enumerate(POOL_WINDOWS):
        cols = slice(g * gdim, (g + 1) * gdim)
        s = ext_s[:, cols]
        k = 1
        while k < w:
            s = s + pltpu.roll(s, k, 0)
            k *= 2
        s = s[POOL_HALO:POOL_HALO + tm]
        cur = ext_s[POOL_HALO:POOL_HALO + tm, cols]
        dlt = (s / jnp.minimum(pos, float(w)) - cur).astype(BF16)
        parts.append(jnp.dot(dlt, poolw_ref[g], preferred_element_type=F32))
    yp = jnp.concatenate(parts, axis=1) * pscale_ref[...]
    y_pool = jnp.dot(yp.astype(BF16), wpu_ref[...], preferred_element_type=F32)

    yg = ys_ref[...].astype(BF16)
    lin = jnp.dot(yg, gluw_ref[...], preferred_element_type=F32) + glub_ref[...]
    glu = yg * jax.nn.sigmoid(lin.astype(BF16))
    y_ssm = jnp.dot(glu, wsu_ref[...], preferred_element_type=F32)

    z = (gate_ref[:, :d_model] * y_pool.astype(BF16) + gate_ref[:, d_model:] * y_ssm.astype(BF16))
    x1 = x_ref[...] + jnp.dot(z, wout_ref[...], preferred_element_type=F32)
    x1_ref[...] = x1

    ms = jnp.mean(x1 * x1, axis=-1, keepdims=True)
    h2 = x1 * lax.rsqrt(ms + RMS_EPS) * gffn_ref[...]
    n_s = d_model // LANES
    for s in range(n_s):
        h3_ref[pl.ds(s, tm, stride=n_s), :] = h2[:, s * LANES:(s + 1) * LANES]

    h_hi = h2.astype(BF16)
    h_lo = (h2 - h_hi.astype(F32)).astype(BF16)
    nt = (((1,), (1,)), ((), ()))
    logits = (lax.dot_general(rwh_ref[...], h_hi, nt, preferred_element_type=F32)
              + lax.dot_general(rwh_ref[...], h_lo, nt, preferred_element_type=F32)
              + lax.dot_general(rwl_ref[...], h_hi, nt, preferred_element_type=F32)
              + rb_ref[...])
    n_e = logits.shape[0]
    iota_e = lax.broadcasted_iota(jnp.int32, (n_e, tm), 0)
    l = logits
    tops, hots = [], []
    for k in range(TOP_K):
        m = jnp.max(l, axis=0, keepdims=True)
        idx = jnp.min(jnp.where(l == m, iota_e, n_e), axis=0, keepdims=True)
        hot = iota_e == idx
        l = jnp.where(hot, -jnp.inf, l)
        tops.append(m)
        hots.append(hot)
        idx_ref[k:k + 1, :] = idx
    exps = [jnp.exp(m - tops[0]) for m in tops]
    den = exps[0] + exps[1] + exps[2] + exps[3]
    for k in range(TOP_K):
        gw_ref[k:k + 1, :] = exps[k] / den

    multi = sum(h.astype(F32) for h in hots)
    cum = jnp.dot(multi.astype(BF16), tri_ref[...], preferred_element_type=F32) + carry_s[:, 0:1]
    for k in range(TOP_K):
        rk = jnp.sum(jnp.where(hots[k], cum, 0.0), axis=0, keepdims=True)
        rank_ref[k:k + 1, :] = rk.astype(jnp.int32)
    carry_s[...] = carry_s[...] + jnp.sum(multi, axis=1, keepdims=True)
    cnt_ref[...] = carry_s[...]


def _mix(x2, u_pool, y_ssm, gates, pool_w, pool_scale, glu_w, glu_b, w_pool_up, w_ssm_up, w_out,
         g_ffn, rw_t, rb, seq_len):
    m, d = x2.shape
    pw = u_pool.shape[1]
    sw = y_ssm.shape[1]
    tm = ROW_TILE
    n_e = rw_t.shape[0]
    rw_hi = rw_t.astype(BF16)
    rw_lo = (rw_t - rw_hi.astype(F32)).astype(BF16)
    tiles_per_seq = seq_len // tm
    tri = (jnp.arange(tm)[:, None] < jnp.arange(tm)[None, :]).astype(BF16)
    row = lambda i: (i, 0)
    const2 = lambda i: (0, 0)
    const3 = lambda i: (0, 0, 0)
    col = lambda i: (0, i)
    halo = lambda i: (jnp.maximum(i * (tm // POOL_HALO) - 1, 0), 0)
    return pl.pallas_call(
        functools.partial(_mix_kernel, tiles_per_seq),
        grid=(m // tm,),
        in_specs=[
            pl.BlockSpec((tm, d), row),
            pl.BlockSpec((tm, pw), row),
            pl.BlockSpec((POOL_HALO, pw), halo),
            pl.BlockSpec((tm, sw), row),
            pl.BlockSpec((tm, 2 * d), row),
            pl.BlockSpec(pool_w.shape, const3),
            pl.BlockSpec((1, pw), const2),
            pl.BlockSpec((sw, sw), const2),
            pl.BlockSpec((1, sw), const2),
            pl.BlockSpec((pw, d), const2),
            pl.BlockSpec((sw, d), const2),
            pl.BlockSpec((d, d), const2),
            pl.BlockSpec((1, d), const2),
            pl.BlockSpec((n_e, d), const2),
            pl.BlockSpec((n_e, d), const2),
            pl.BlockSpec((n_e, 1), const2),
            pl.BlockSpec((tm, tm), const2),
        ],
        out_specs=[
            pl.BlockSpec((tm, d), row),
            pl.BlockSpec((tm * (d // LANES), LANES), row),
            pl.BlockSpec((TOP_K, tm), col),
            pl.BlockSpec((TOP_K, tm), col),
            pl.BlockSpec((TOP_K, tm), col),
            pl.BlockSpec((n_e, LANES), const2),
        ],
        out_shape=[
            jax.ShapeDtypeStruct((m, d), F32),
            jax.ShapeDtypeStruct((m * (d // LANES), LANES), F32),
            jax.ShapeDtypeStruct((TOP_K, m), jnp.int32),
            jax.ShapeDtypeStruct((TOP_K, m), F32),
            jax.ShapeDtypeStruct((TOP_K, m), jnp.int32),
            jax.ShapeDtypeStruct((n_e, LANES), F32),
        ],
        scratch_shapes=[
            pltpu.VMEM((POOL_HALO + tm, pw), F32),
            pltpu.VMEM((n_e, LANES), F32),
        ],
        compiler_params=_cparams(1),
        name="mix_route",
    )(x2, u_pool, u_pool, y_ssm, gates, pool_w, pool_scale, glu_w, glu_b, w_pool_up, w_ssm_up,
      w_out, g_ffn, rw_hi, rw_lo, rb, tri)


def _row_copy(src, dst, sem):
    return pltpu.make_async_copy(src, dst, sem)


def _for_each_run_piece(tab, tb, max_rows, n_s, fn):
    def per_expert(e, carry):
        n = tab[tb + e]
        off = tab[tb + N_EXPERTS + e]
        slot = tab[tb + 2 * N_EXPERTS + e]
        def piece(b):
            size = 1 << b

            @pl.when(((n >> b) & 1) == 1)
            def _():
                done = n & (size - 1)
                fn(pl.multiple_of((slot + done) * n_s, n_s), pl.multiple_of((off + done) * n_s, n_s),
                   size * n_s)

        n_bits = max_rows.bit_length()
        low_bits = min(n_bits, (2 * TOP_K * max_rows // N_EXPERTS - 1).bit_length())
        for b in range(low_bits):
            piece(b)

        @pl.when((n >> low_bits) != 0)
        def _():
            for b in range(low_bits, n_bits):
                piece(b)
        return carry

    lax.fori_loop(0, N_EXPERTS, per_expert, 0)


def _dispatch_kernel(n_s, tab_ref, lpos_ref, zpos_ref, h3_ref, xs_ref, st0_s, st1_s, sem):
    j = pl.program_id(0)
    tm = h3_ref.shape[0] // (2 * n_s)
    per_tile = TOP_K * tm
    zrows = MOE_ROWS * n_s

    @pl.when(j == 0)
    def _():
        st0_s[0:zrows, :] = jnp.zeros((zrows, st0_s.shape[1]), F32)

        def fill(start):
            cp = _row_copy(st0_s.at[pl.ds(0, zrows)],
                           xs_ref.at[pl.ds(pl.multiple_of(start * n_s, n_s), zrows)], sem.at[0])
            cp.start()
            cp.wait()

        for e in range(N_EXPERTS):
            fill(zpos_ref[e])

        def tail(t, carry):
            fill(t * MOE_ROWS)
            return carry

        lax.fori_loop(zpos_ref[N_EXPERTS], xs_ref.shape[0] // zrows, tail, 0)

    def drain(stage, slot):
        _row_copy(stage, xs_ref.at[pl.ds(0, per_tile * n_s)], sem.at[slot]).wait()

    def scatter(stage, slot):
        @pl.when(j > 0)
        def _():
            drain(stage, slot)

        base = slot * per_tile

        def tokens(g, carry):
            for u in range(COMBINE_UNROLL):
                t = g * COMBINE_UNROLL + u
                rows = h3_ref[pl.ds(pl.multiple_of((slot * tm + t) * n_s, n_s), n_s), :]
                for k in range(TOP_K):
                    row = pl.multiple_of(lpos_ref[base + k * tm + t], n_s)
                    stage[pl.ds(row, n_s), :] = rows
            return carry

        lax.fori_loop(0, tm // COMBINE_UNROLL, tokens, 0)
        _for_each_run_piece(
            tab_ref, slot * RUN_TAB, tm, n_s,
            lambda s, b, r: _row_copy(stage.at[pl.ds(b, r)], xs_ref.at[pl.ds(s, r)],
                                      sem.at[slot]).start())

    scatter(st0_s, 0)
    scatter(st1_s, 1)

    @pl.when(j + 1 == pl.num_programs(0))
    def _():
        drain(st0_s, 0)
        drain(st1_s, 1)


def _dispatch(tab, lpos_tiles, zpos, h3, n_slots, n_s):
    rows, lanes = h3.shape
    tm = COMBINE_TILE
    smem = pltpu.SMEM
    return pl.pallas_call(
        functools.partial(_dispatch_kernel, n_s),
        grid=(rows // (2 * tm * n_s),),
        in_specs=[
            pl.BlockSpec((2 * RUN_TAB,), lambda j: (j,), memory_space=smem),
            pl.BlockSpec((2 * TOP_K * tm,), lambda j: (j,), memory_space=smem),
            pl.BlockSpec(memory_space=smem),
            pl.BlockSpec((2 * tm * n_s, lanes), lambda j: (j, 0)),
        ],
        out_specs=pl.BlockSpec(memory_space=pl.ANY),
        out_shape=jax.ShapeDtypeStruct(((n_slots + MOE_ROWS) * n_s, lanes), F32),
        scratch_shapes=[pltpu.VMEM((TOP_K * tm * n_s, lanes), F32),
                        pltpu.VMEM((TOP_K * tm * n_s, lanes), F32),
                        pltpu.SemaphoreType.DMA((2,))],
        compiler_params=_cparams(1),
        name="dispatch",
    )(tab, lpos_tiles, zpos, h3)


def _expert_kernel(blk_e_ref, first_ref, next_ref, nused_ref, xs_ref, w1_hbm, b1_ref, w2_hbm, b2_ref,
                   perm_ref, ys_ref, w1f_s, w2f_s, w1p_s, w2b_s, sem):
    i = pl.program_id(0)
    n_s = w1f_s.shape[0] // LANES
    rows = xs_ref.shape[0] // n_s
    f2 = w1f_s.shape[1]

    def fetch(e):
        return (_row_copy(w1_hbm.at[e], w1f_s, sem.at[0]), _row_copy(w2_hbm.at[e], w2f_s, sem.at[1]))

    @pl.when(i == 0)
    def _():
        for cp in fetch(blk_e_ref[0]):
            cp.start()

    @pl.when(first_ref[i] == 1)
    def _():
        for cp in fetch(blk_e_ref[i]):
            cp.wait()
        for c in range(f2 // DEINT):
            cols = slice(c * DEINT, (c + 1) * DEINT)
            strip = w1f_s[:, cols].astype(BF16)
            w1p_s[:, cols] = jnp.dot(strip, perm_ref[...], preferred_element_type=F32).astype(BF16)
        w2b_s[...] = w2f_s[...].astype(BF16)

        @pl.when(next_ref[i] >= 0)
        def _():
            for cp in fetch(next_ref[i]):
                cp.start()

    @pl.when(i < nused_ref[0])
    def _():
        x = jnp.concatenate([xs_ref[pl.ds(s, rows, stride=n_s), :] for s in range(n_s)],
                            axis=1).astype(BF16)
        h = jnp.dot(x, w1p_s[...], preferred_element_type=F32) + b1_ref[0]
        acts = []
        for c in range(f2 // DEINT):
            xg = jnp.minimum(h[:, c * DEINT:c * DEINT + LANES], SWIGLU_LIMIT)
            xl = jnp.clip(h[:, c * DEINT + LANES:(c + 1) * DEINT], -SWIGLU_LIMIT, SWIGLU_LIMIT)
            acts.append((xg * jax.nn.sigmoid(SWIGLU_ALPHA * xg) * (xl + 1.0)).astype(BF16))
        act = jnp.concatenate(acts, axis=1)
        y = jnp.dot(act, w2b_s[...], preferred_element_type=F32) + b2_ref[0]
        for s in range(n_s):
            ys_ref[pl.ds(s, rows, stride=n_s), :] = y[:, s * LANES:(s + 1) * LANES]

    @pl.when(i >= nused_ref[0])
    def _():
        ys_ref[...] = jnp.zeros_like(ys_ref)


def _experts(blk_e, first, nxt, n_used, xs, w1, b1p, w2, b2, n_blocks):
    _, lanes = xs.shape
    _, d, f2 = w1.shape
    blk = MOE_ROWS * (d // lanes)
    f = w2.shape[1]
    half = jnp.arange(DEINT) // 2 + (jnp.arange(DEINT) % 2) * LANES
    perm = (half[:, None] == jnp.arange(DEINT)[None, :]).astype(BF16)
    xmap = lambda i, be, fi, nx, nu: (jnp.minimum(i, nu[0] - 1), 0)
    emap = lambda i, be, fi, nx, nu: (be[i], 0, 0)
    grid_spec = pltpu.PrefetchScalarGridSpec(
        num_scalar_prefetch=4,
        grid=(n_blocks,),
        in_specs=[
            pl.BlockSpec((blk, lanes), xmap),
            pl.BlockSpec(memory_space=pl.ANY),
            pl.BlockSpec((1, 1, f2), emap),
            pl.BlockSpec(memory_space=pl.ANY),
            pl.BlockSpec((1, 1, d), emap),
            pl.BlockSpec((DEINT, DEINT), lambda i, be, fi, nx, nu: (0, 0)),
        ],
        out_specs=pl.BlockSpec((blk, lanes), lambda i, be, fi, nx, nu: (i, 0)),
        scratch_shapes=[pltpu.VMEM((d, f2), F32), pltpu.VMEM((f, d), F32),
                        pltpu.VMEM((d, f2), BF16), pltpu.VMEM((f, d), BF16),
                        pltpu.SemaphoreType.DMA((2,))],
    )
    return pl.pallas_call(
        _expert_kernel,
        grid_spec=grid_spec,
        out_shape=jax.ShapeDtypeStruct((n_blocks * blk, lanes), F32),
        compiler_params=_cparams(1),
        name="experts",
    )(blk_e, first, nxt, n_used, xs, w1, b1p, w2, b2, perm)


def _combine_kernel(tab_ref, nxt_ref, lpos_ref, gw_ref, x1_ref, gfin_ref, ys_ref, out_ref,
                    buf0_s, buf1_s, acc_s, sem):
    j = pl.program_id(0)
    tm = x1_ref.shape[0] // 2
    n_s = x1_ref.shape[1] // LANES
    per_tile = TOP_K * tm
    tab_w = RUN_TAB

    def gather(tab, tb, buf, slot):
        _for_each_run_piece(
            tab, tb, tm, n_s,
            lambda s, b, r: _row_copy(ys_ref.at[pl.ds(s, r)], buf.at[pl.ds(b, r)],
                                      sem.at[slot]).start())

    def reduce(buf, slot, t0):
        _row_copy(ys_ref.at[pl.ds(0, per_tile * n_s)], buf, sem.at[slot]).wait()
        base = (t0 // tm) * per_tile

        def tokens(g, carry):
            for u in range(COMBINE_UNROLL):
                t = g * COMBINE_UNROLL + u
                acc = None
                for k in range(TOP_K):
                    row = pl.multiple_of(lpos_ref[base + k * tm + t], n_s)
                    term = gw_ref[base + k * tm + t] * buf[pl.ds(row, n_s), :]
                    acc = term if acc is None else acc + term
                acc_s[pl.ds(pl.multiple_of(t * n_s, n_s), n_s), :] = acc
            return carry

        lax.fori_loop(0, tm // COMBINE_UNROLL, tokens, 0)
        y = jnp.concatenate([acc_s[pl.ds(s, tm, stride=n_s), :] for s in range(n_s)], axis=1)
        acc = x1_ref[t0:t0 + tm, :] + y
        ms = jnp.mean(acc * acc, axis=-1, keepdims=True)
        out_ref[t0:t0 + tm, :] = acc * lax.rsqrt(ms + RMS_EPS) * gfin_ref[...]

    @pl.when(j == 0)
    def _():
        gather(tab_ref, 0, buf0_s, 0)

    gather(tab_ref, tab_w, buf1_s, 1)
    reduce(buf0_s, 0, 0)

    @pl.when(j + 1 < pl.num_programs(0))
    def _():
        gather(nxt_ref, 0, buf0_s, 0)

    reduce(buf1_s, 1, tm)


def _combine(tab, lpos_tiles, gw_tiles, x1, g_final, ys):
    m, d = x1.shape
    _, lanes = ys.shape
    s = d // lanes
    tm = COMBINE_TILE
    n_pairs = m // (2 * tm)
    tab_w = RUN_TAB
    smem = pltpu.SMEM
    return pl.pallas_call(
        _combine_kernel,
        grid=(n_pairs,),
        in_specs=[
            pl.BlockSpec((2 * tab_w,), lambda j: (j,), memory_space=smem),
            pl.BlockSpec((tab_w,), lambda j: (jnp.minimum(2 * j + 2, 2 * n_pairs - 1),),
                         memory_space=smem),
            pl.BlockSpec((2 * TOP_K * tm,), lambda j: (j,), memory_space=smem),
            pl.BlockSpec((2 * TOP_K * tm,), lambda j: (j,), memory_space=smem),
            pl.BlockSpec((2 * tm, d), lambda j: (j, 0)),
            pl.BlockSpec((1, d), lambda j: (0, 0)),
            pl.BlockSpec(memory_space=pl.ANY),
        ],
        out_specs=pl.BlockSpec((2 * tm, d), lambda j: (j, 0)),
        out_shape=jax.ShapeDtypeStruct((m, d), F32),
        scratch_shapes=[pltpu.VMEM((TOP_K * tm * s, lanes), F32),
                        pltpu.VMEM((TOP_K * tm * s, lanes), F32),
                        pltpu.VMEM((tm * s, lanes), F32),
                        pltpu.SemaphoreType.DMA((2,))],
        compiler_params=_cparams(1),
        name="combine",
    )(tab, tab, lpos_tiles, gw_tiles, x1, g_final, ys)


def kernel(x, norm_mix_g, w_in, b_gate, pool_w, pool_scale, ssm_lambda_re, ssm_lambda_im, ssm_log_dt, ssm_b_re, ssm_b_im, ssm_c_re, ssm_c_im, ssm_d, ssm_glu_w, ssm_glu_b, w_pool_up, w_ssm_up, w_out, norm_ffn_g, router_w, router_b, moe_w1, moe_b1, moe_w2, moe_b2, norm_final_g):
    bsz, seq, d = x.shape
    depth = w_in.shape[0]
    pw = pool_w.shape[1] * pool_w.shape[2]
    n_groups = ssm_lambda_re.shape[1]
    sw = n_groups * SSM_GROUP_DIM
    m = bsz * seq
    assert depth == 1
    assert bsz == SUBLANES and seq % ROW_TILE == 0 and seq % SSM_TILE == 0 and d % LANES == 0
    assert pool_w.shape[1] == len(POOL_WINDOWS) and n_groups % SSM_PACK == 0
    assert all(w & (w - 1) == 0 and w <= POOL_HALO for w in POOL_WINDOWS)
    assert moe_w1.shape[3] % DEINT == 0 and moe_w1.shape[1] == N_EXPERTS
    assert m % INPROJ_TILE == 0 and m % (2 * COMBINE_TILE) == 0
    assert COMBINE_TILE % COMBINE_UNROLL == 0

    x2 = x.reshape(m, d)
    for l in range(depth):
        wi = w_in[l].astype(BF16)
        u_pool, u_ssm, gates = _inproj(
            x2, norm_mix_g[l][None], wi[:, :pw], wi[:, pw:pw + sw], wi[:, pw + sw:], b_gate[l][None])
        mats = _ssm_matrices(ssm_lambda_re[l], ssm_lambda_im[l], ssm_log_dt[l], ssm_b_re[l],
                             ssm_b_im[l], ssm_c_re[l], ssm_c_im[l], ssm_d[l])
        y_ssm = _ssm(u_ssm.reshape(bsz, seq, sw), mats).reshape(m, sw)
        x1, h3, idx_t, gw_t, rank_t, cnt = _mix(
            x2, u_pool, y_ssm, gates, pool_w[l].astype(BF16), pool_scale[l][None],
            ssm_glu_w[l].astype(BF16), ssm_glu_b[l][None], w_pool_up[l].astype(BF16),
            w_ssm_up[l].astype(BF16), w_out[l].astype(BF16), norm_ffn_g[l][None],
            router_w[l].T, router_b[l][:, None], seq)

        n_assign = m * TOP_K
        n_blocks = (n_assign + N_EXPERTS * (MOE_ROWS - 1) + MOE_ROWS - 1) // MOE_ROWS
        n_slots = n_blocks * MOE_ROWS
        counts = cnt[:, 0].astype(jnp.int32)
        padded = (counts + MOE_ROWS - 1) // MOE_ROWS * MOE_ROWS
        pad_end = jnp.cumsum(padded)
        pad_start = pad_end - padded
        n_used = (pad_end[-1] // MOE_ROWS).astype(jnp.int32)[None]
        blk_row = jnp.arange(n_blocks, dtype=jnp.int32) * MOE_ROWS
        blk_e = jnp.minimum(jnp.sum(pad_end[None, :] <= blk_row[:, None], axis=1),
                            N_EXPERTS - 1).astype(jnp.int32)
        blk_ids = jnp.arange(n_blocks, dtype=jnp.int32)
        first = jnp.concatenate([jnp.ones((1,), jnp.int32),
                                 (blk_e[1:] != blk_e[:-1]).astype(jnp.int32)])
        first = jnp.where(blk_ids < n_used[0], first, 0)
        ex = jnp.arange(N_EXPERTS, dtype=jnp.int32)
        later = (ex[None, :] > ex[:, None]) & (counts[None, :] > 0)
        next_e = jnp.min(jnp.where(later, ex[None, :], N_EXPERTS), axis=1)
        next_e = jnp.where(next_e < N_EXPERTS, next_e, -1).astype(jnp.int32)
        nxt = jnp.sum(jnp.where(blk_e[:, None] == ex[None, :], next_e[None, :], 0), axis=1)
        e_ids = jnp.arange(N_EXPERTS, dtype=jnp.int32)[:, None, None]
        zpos = jnp.concatenate([pad_start + counts, n_used]).astype(jnp.int32)
        ct = COMBINE_TILE
        n_ct = m // ct
        cnt_te = jnp.sum((idx_t.reshape(TOP_K, n_ct, ct)[None] == e_ids[..., None]).astype(jnp.int32),
                         axis=(1, 3))
        carry_te = jnp.cumsum(cnt_te, axis=1) - cnt_te
        offs_te = jnp.cumsum(cnt_te, axis=0) - cnt_te
        tab = jnp.stack([cnt_te, offs_te, pad_start[:, None] + carry_te, jnp.zeros_like(cnt_te)],
                        axis=0)
        tab = tab.transpose(2, 0, 1).reshape(-1).astype(jnp.int32)
        delta = jnp.repeat(offs_te - carry_te, ct, axis=1)
        lpos = rank_t + jnp.sum(jnp.where(idx_t[None] == e_ids, delta[:, None, :], 0), axis=0)
        tile_major = lambda a: a.reshape(TOP_K, n_ct, ct).transpose(1, 0, 2).reshape(-1)
        lpos_tiles = tile_major((lpos * (d // LANES)).astype(jnp.int32))

        xs = _dispatch(tab, lpos_tiles, zpos, h3, n_slots, d // LANES)
        f2 = moe_w1.shape[3]
        b1p = (moe_b1[l].reshape(N_EXPERTS, f2 // DEINT, LANES, 2)
               .transpose(0, 1, 3, 2).reshape(N_EXPERTS, 1, f2))
        ys = _experts(blk_e, first, nxt, n_used, xs, moe_w1[l], b1p, moe_w2[l], moe_b2[l][:, None, :],
                      n_blocks)
        x2 = _combine(tab, lpos_tiles, tile_major(gw_t), x1, norm_final_g[None], ys)
    return x2.reshape(bsz, seq, d)
```

```python
import functools

import jax
import jax.numpy as jnp
from jax import lax
from jax.experimental import pallas as pl
from jax.experimental.pallas import tpu as pltpu

F32 = jnp.float32
BF16 = jnp.bfloat16

RMS_EPS = 1e-6
POOL_WINDOWS = (2, 4, 8, 16)
POOL_HALO = 16
SSM_GROUP_DIM = 16
SSM_STATE = 64
N_EXPERTS = 32
TOP_K = 4
SWIGLU_ALPHA = 1.702
SWIGLU_LIMIT = 7.0

LANES = 128
SUBLANES = 8
SSM_CHUNK = SUBLANES
SSM_PACK = LANES // SSM_GROUP_DIM
SSM_TILE = 512
DEINT = 2 * LANES
ROW_TILE = 1024
INPROJ_TILE = 1024
COMBINE_TILE = 512
COMBINE_UNROLL = 8
RUN_TAB = 4 * N_EXPERTS
MOE_ROWS = 512
VMEM_LIMIT = 60 * 1024 * 1024


def _cparams(n_axes):
    return pltpu.CompilerParams(
        dimension_semantics=("arbitrary",) * n_axes, vmem_limit_bytes=VMEM_LIMIT)


def _inproj_kernel(x_ref, g_ref, wp_ref, ws_ref, wg_ref, bg_ref, up_ref, us_ref, gate_ref):
    x = x_ref[...]
    ms = jnp.mean(x * x, axis=-1, keepdims=True)
    h = (x * lax.rsqrt(ms + RMS_EPS) * g_ref[...]).astype(BF16)
    up_ref[...] = jnp.dot(h, wp_ref[...], preferred_element_type=F32)
    us_ref[...] = jnp.dot(h, ws_ref[...], preferred_element_type=F32)
    gl = jnp.dot(h, wg_ref[...], preferred_element_type=F32) + bg_ref[...]
    gate_ref[...] = jax.nn.sigmoid(gl).astype(BF16)


def _inproj(x2, g, wp, ws, wg, bg):
    m, d = x2.shape
    pw, sw, gw = wp.shape[1], ws.shape[1], wg.shape[1]
    tm = INPROJ_TILE
    const = lambda i: (0, 0)
    row = lambda i: (i, 0)
    return pl.pallas_call(
        _inproj_kernel,
        grid=(m // tm,),
        in_specs=[
            pl.BlockSpec((tm, d), row),
            pl.BlockSpec((1, d), const),
            pl.BlockSpec((d, pw), const),
            pl.BlockSpec((d, sw), const),
            pl.BlockSpec((d, gw), const),
            pl.BlockSpec((1, gw), const),
        ],
        out_specs=[
            pl.BlockSpec((tm, pw), row),
            pl.BlockSpec((tm, sw), row),
            pl.BlockSpec((tm, gw), row),
        ],
        out_shape=[
            jax.ShapeDtypeStruct((m, pw), F32),
            jax.ShapeDtypeStruct((m, sw), F32),
            jax.ShapeDtypeStruct((m, gw), BF16),
        ],
        compiler_params=_cparams(1),
        name="inproj",
    )(x2, g, wp, ws, wg, bg)


def _ssm_matrices(lam_re, lam_im, log_dt, b_re, b_im, c_re, c_im, d_skip):
    hi = lax.Precision.HIGHEST
    L, P, N, GP = SSM_CHUNK, SSM_GROUP_DIM, SSM_STATE, SSM_PACK
    G = lam_re.shape[0]
    K = G // GP
    lr, li = lam_re.astype(F32), lam_im.astype(F32)
    dt = jnp.exp(log_dt.astype(F32))[:, None]
    mag = jnp.exp(lr * dt)
    lb_re, lb_im = mag * jnp.cos(li * dt), mag * jnp.sin(li * dt)
    den = lr * lr + li * li
    xr, xi = lb_re - 1.0, lb_im
    f_re = (xr * lr + xi * li) / den
    f_im = (xi * lr - xr * li) / den
    br, bi = b_re.astype(F32), b_im.astype(F32)
    bb_re = f_re[..., None] * br - f_im[..., None] * bi
    bb_im = f_re[..., None] * bi + f_im[..., None] * br
    n_c, n_s = GP * P, GP * N
    tau = jnp.arange(L + 1, dtype=F32)[None, :, None]
    pmag = jnp.exp((lr * dt).reshape(K, 1, n_s) * tau)
    pang = (li * dt).reshape(K, 1, n_s) * tau
    pw_re, pw_im = pmag * jnp.cos(pang), pmag * jnp.sin(pang)
    same_group = (jnp.arange(n_c)[:, None] // P) == (jnp.arange(n_s)[None, :] // N)

    def block_diag(m, mask):
        rows = m.transpose(0, 2, 1).reshape(K, GP * m.shape[2], m.shape[1])
        return jnp.where(mask, jnp.tile(rows, (1, 1, GP)), 0.0)

    bbp_re, bbp_im = block_diag(bb_re, same_group), block_diag(bb_im, same_group)
    cp_re = block_diag(c_re.astype(F32), same_group.T)
    cp_im = block_diag(c_im.astype(F32), same_group.T)
    al_re, al_im = pw_re[:, :L, None, :], pw_im[:, :L, None, :]
    wl_re = al_re * bbp_re[:, None] - al_im * bbp_im[:, None]
    wl_im = al_re * bbp_im[:, None] + al_im * bbp_re[:, None]
    kern = (jnp.einsum('klcs,ksd->klcd', wl_re, cp_re, precision=hi)
            - jnp.einsum('klcs,ksd->klcd', wl_im, cp_im, precision=hi))
    kern = jnp.concatenate([jnp.zeros_like(kern), kern], axis=1)
    toep = jnp.stack([kern[:, L - j:2 * L - j].transpose(0, 2, 1, 3).reshape(K, n_c, L * n_c)
                      for j in range(L)], axis=1).reshape(K, L * n_c, L * n_c)
    w = jnp.concatenate([wl_re[:, ::-1].reshape(K, L * n_c, n_s),
                         wl_im[:, ::-1].reshape(K, L * n_c, n_s)], axis=2)
    p1_re, p1_im = pw_re[:, 1:, :, None], pw_im[:, 1:, :, None]
    v_re = cp_re[:, None] * p1_re - cp_im[:, None] * p1_im
    v_im = -(cp_re[:, None] * p1_im + cp_im[:, None] * p1_re)
    v = (jnp.concatenate([v_re, v_im], axis=2).transpose(0, 2, 1, 3)
         .reshape(K, 2 * n_s, L * n_c))
    al = jnp.stack([pw_re[:, L], pw_im[:, L]], axis=1)
    dvec = jnp.tile(d_skip.astype(F32).reshape(K, 1, n_c), (1, L, 1)).reshape(K, 1, L * n_c)
    return toep.astype(BF16), w.astype(BF16), v.astype(BF16), al, dvec


def _ssm_kernel(u_ref, toep_ref, w_ref, v_ref, al_ref, d_ref, y_ref, lhs_s, e_s, s_s, carry_s):
    bsz, tt, lanes = u_ref.shape
    L = SSM_CHUNK
    n_chunks = tt // L
    ns = al_ref.shape[2]

    @pl.when(pl.program_id(1) == 0)
    def _():
        carry_s[...] = jnp.zeros_like(carry_s)

    for b in range(bsz):
        for t in range(L):
            lhs_s[t, pl.ds(b, n_chunks, stride=bsz), :] = u_ref[b, pl.ds(t, n_chunks, stride=L), :]
    lhs = jnp.concatenate([lhs_s[t] for t in range(L)], axis=1)
    lb = lhs.astype(BF16)
    e_s[...] = jnp.dot(lb, w_ref[0], preferred_element_type=F32)
    ar = jnp.broadcast_to(al_ref[0, 0:1, :], (bsz, ns))
    ai = jnp.broadcast_to(al_ref[0, 1:2, :], (bsz, ns))
    sr = carry_s[0]
    si = carry_s[1]
    for c in range(n_chunks):
        rows = slice(c * bsz, (c + 1) * bsz)
        s_s[rows, :ns] = sr
        s_s[rows, ns:] = si
        er = e_s[rows, :ns]
        ei = e_s[rows, ns:]
        sr, si = ar * sr - ai * si + er, ar * si + ai * sr + ei
    carry_s[0] = sr
    carry_s[1] = si
    y = (jnp.dot(lb, toep_ref[0], preferred_element_type=F32)
         + jnp.dot(s_s[...].astype(BF16), v_ref[0], preferred_element_type=F32)
         + d_ref[0] * lhs)
    yg = jax.nn.gelu(y)
    for t in range(L):
        lhs_s[t] = yg[:, t * lanes:(t + 1) * lanes]
    for b in range(bsz):
        for t in range(L):
            y_ref[b, pl.ds(t, n_chunks, stride=L), :] = lhs_s[t, pl.ds(b, n_chunks, stride=bsz), :]


def _ssm(u3, mats):
    toep, w, v, al, dvec = mats
    bsz, seq, sw = u3.shape
    k = toep.shape[0]
    lanes = sw // k
    tt = SSM_TILE
    rows = bsz * tt // SSM_CHUNK
    cl = SSM_CHUNK * lanes
    ns = al.shape[2]
    pack = lambda p, t: (p, 0, 0)
    tile = lambda p, t: (0, t, p)
    return pl.pallas_call(
        _ssm_kernel,
        grid=(k, seq // tt),
        in_specs=[
            pl.BlockSpec((bsz, tt, lanes), tile),
            pl.BlockSpec((1, cl, cl), pack),
            pl.BlockSpec((1, cl, 2 * ns), pack),
            pl.BlockSpec((1, 2 * ns, cl), pack),
            pl.BlockSpec((1, 2, ns), pack),
            pl.BlockSpec((1, 1, cl), pack),
        ],
        out_specs=pl.BlockSpec((bsz, tt, lanes), tile),
        out_shape=jax.ShapeDtypeStruct((bsz, seq, sw), F32),
        scratch_shapes=[
            pltpu.VMEM((SSM_CHUNK, rows, lanes), F32),
            pltpu.VMEM((rows, 2 * ns), F32),
            pltpu.VMEM((rows, 2 * ns), F32),
            pltpu.VMEM((2, bsz, ns), F32),
        ],
        compiler_params=_cparams(2),
        name="ssm",
    )(u3, toep, w, v, al, dvec)


def _mix_kernel(tiles_per_seq, x_ref, up_ref, halo_ref, ys_ref, gate_ref, poolw_ref, pscale_ref,
                gluw_ref, glub_ref, wpu_ref, wsu_ref, wout_ref, gffn_ref, rwh_ref, rwl_ref, rb_ref,
                tri_ref,
                x1_ref, h3_ref, idx_ref, gw_ref, rank_ref, cnt_ref, ext_s, carry_s):
    i = pl.program_id(0)
    j = i % tiles_per_seq
    tm = x_ref.shape[0]
    d_model = x_ref.shape[1]
    gdim = poolw_ref.shape[1]

    @pl.when(i == 0)
    def _():
        carry_s[...] = jnp.zeros_like(carry_s)

    ext_s[0:POOL_HALO, :] = jnp.where(j == 0, 0.0, halo_ref[...])
    ext_s[POOL_HALO:, :] = up_ref[...]
    pos = (j * tm + 1 + lax.broadcasted_iota(jnp.int32, (tm, 1), 0)).astype(F32)
    parts = []
    for g, w in enumerate(POOL_WINDOWS):
        cols = slice(g * gdim, (g + 1) * gdim)
        s = ext_s[:, cols]
        k = 1
        while k < w:
            s = s + pltpu.roll(s, k, 0)
            k *= 2
        s = s[POOL_HALO:POOL_HALO + tm]
        cur = ext_s[POOL_HALO:POOL_HALO + tm, cols]
        dlt = (s / jnp.minimum(pos, float(w)) - cur).astype(BF16)
        parts.append(jnp.dot(dlt, poolw_ref[g], preferred_element_type=F32))
    yp = jnp.concatenate(parts, axis=1) * pscale_ref[...]
    y_pool = jnp.dot(yp.astype(BF16), wpu_ref[...], preferred_element_type=F32)

    yg = ys_ref[...].astype(BF16)
    lin = jnp.dot(yg, gluw_ref[...], preferred_element_type=F32) + glub_ref[...]
    glu = yg * jax.nn.sigmoid(lin.astype(BF16))
    y_ssm = jnp.dot(glu, wsu_ref[...], preferred_element_type=F32)

    z = (gate_ref[:, :d_model] * y_pool.astype(BF16) + gate_ref[:, d_model:] * y_ssm.astype(BF16))
    x1 = x_ref[...] + jnp.dot(z, wout_ref[...], preferred_element_type=F32)
    x1_ref[...] = x1

    ms = jnp.mean(x1 * x1, axis=-1, keepdims=True)
    h2 = x1 * lax.rsqrt(ms + RMS_EPS) * gffn_ref[...]
    n_s = d_model // LANES
    for s in range(n_s):
        h3_ref[pl.ds(s, tm, stride=n_s), :] = h2[:, s * LANES:(s + 1) * LANES]

    h_hi = h2.astype(BF16)
    h_lo = (h2 - h_hi.astype(F32)).astype(BF16)
    nt = (((1,), (1,)), ((), ()))
    logits = (lax.dot_general(rwh_ref[...], h_hi, nt, preferred_element_type=F32)
              + lax.dot_general(rwh_ref[...], h_lo, nt, preferred_element_type=F32)
              + lax.dot_general(rwl_ref[...], h_hi, nt, preferred_element_type=F32)
              + rb_ref[...])
    n_e = logits.shape[0]
    iota_e = lax.broadcasted_iota(jnp.int32, (n_e, tm), 0)
    l = logits
    tops, hots = [], []
    for k in range(TOP_K):
        m = jnp.max(l, axis=0, keepdims=True)
        idx = jnp.min(jnp.where(l == m, iota_e, n_e), axis=0, keepdims=True)
        hot = iota_e == idx
        l = jnp.where(hot, -jnp.inf, l)
        tops.append(m)
        hots.append(hot)
        idx_ref[k:k + 1, :] = idx
    exps = [jnp.exp(m - tops[0]) for m in tops]
    den = exps[0] + exps[1] + exps[2] + exps[3]
    for k in range(TOP_K):
        gw_ref[k:k + 1, :] = exps[k] / den

    multi = sum(h.astype(F32) for h in hots)
    cum = jnp.dot(multi.astype(BF16), tri_ref[...], preferred_element_type=F32) + carry_s[:, 0:1]
    for k in range(TOP_K):
        rk = jnp.sum(jnp.where(hots[k], cum, 0.0), axis=0, keepdims=True)
        rank_ref[k:k + 1, :] = rk.astype(jnp.int32)
    carry_s[...] = carry_s[...] + jnp.sum(multi, axis=1, keepdims=True)
    cnt_ref[...] = carry_s[...]


def _mix(x2, u_pool, y_ssm, gates, pool_w, pool_scale, glu_w, glu_b, w_pool_up, w_ssm_up, w_out,
         g_ffn, rw_t, rb, seq_len):
    m, d = x2.shape
    pw = u_pool.shape[1]
    sw = y_ssm.shape[1]
    tm = ROW_TILE
    n_e = rw_t.shape[0]
    rw_hi = rw_t.astype(BF16)
    rw_lo = (rw_t - rw_hi.astype(F32)).astype(BF16)
    tiles_per_seq = seq_len // tm
    tri = (jnp.arange(tm)[:, None] < jnp.arange(tm)[None, :]).astype(BF16)
    row = lambda i: (i, 0)
    const2 = lambda i: (0, 0)
    const3 = lambda i: (0, 0, 0)
    col = lambda i: (0, i)
    halo = lambda i: (jnp.maximum(i * (tm // POOL_HALO) - 1, 0), 0)
    return pl.pallas_call(
        functools.partial(_mix_kernel, tiles_per_seq),
        grid=(m // tm,),
        in_specs=[
            pl.BlockSpec((tm, d), row),
            pl.BlockSpec((tm, pw), row),
            pl.BlockSpec((POOL_HALO, pw), halo),
            pl.BlockSpec((tm, sw), row),
            pl.BlockSpec((tm, 2 * d), row),
            pl.BlockSpec(pool_w.shape, const3),
            pl.BlockSpec((1, pw), const2),
            pl.BlockSpec((sw, sw), const2),
            pl.BlockSpec((1, sw), const2),
            pl.BlockSpec((pw, d), const2),
            pl.BlockSpec((sw, d), const2),
            pl.BlockSpec((d, d), const2),
            pl.BlockSpec((1, d), const2),
            pl.BlockSpec((n_e, d), const2),
            pl.BlockSpec((n_e, d), const2),
            pl.BlockSpec((n_e, 1), const2),
            pl.BlockSpec((tm, tm), const2),
        ],
        out_specs=[
            pl.BlockSpec((tm, d), row),
            pl.BlockSpec((tm * (d // LANES), LANES), row),
            pl.BlockSpec((TOP_K, tm), col),
            pl.BlockSpec((TOP_K, tm), col),
            pl.BlockSpec((TOP_K, tm), col),
            pl.BlockSpec((n_e, LANES), const2),
        ],
        out_shape=[
            jax.ShapeDtypeStruct((m, d), F32),
            jax.ShapeDtypeStruct((m * (d // LANES), LANES), F32),
            jax.ShapeDtypeStruct((TOP_K, m), jnp.int32),
            jax.ShapeDtypeStruct((TOP_K, m), F32),
            jax.ShapeDtypeStruct((TOP_K, m), jnp.int32),
            jax.ShapeDtypeStruct((n_e, LANES), F32),
        ],
        scratch_shapes=[
            pltpu.VMEM((POOL_HALO + tm, pw), F32),
            pltpu.VMEM((n_e, LANES), F32),
        ],
        compiler_params=_cparams(1),
        name="mix_route",
    )(x2, u_pool, u_pool, y_ssm, gates, pool_w, pool_scale, glu_w, glu_b, w_pool_up, w_ssm_up,
      w_out, g_ffn, rw_hi, rw_lo, rb, tri)


def _row_copy(src, dst, sem):
    return pltpu.make_async_copy(src, dst, sem)


def _for_each_run_piece(tab, tb, max_rows, n_s, fn):
    def per_expert(e, carry):
        n = tab[tb + e]
        off = tab[tb + N_EXPERTS + e]
        slot = tab[tb + 2 * N_EXPERTS + e]
        def piece(b):
            size = 1 << b

            @pl.when(((n >> b) & 1) == 1)
            def _():
                done = n & (size - 1)
                fn(pl.multiple_of((slot + done) * n_s, n_s), pl.multiple_of((off + done) * n_s, n_s),
                   size * n_s)

        n_bits = max_rows.bit_length()
        low_bits = min(n_bits, (2 * TOP_K * max_rows // N_EXPERTS - 1).bit_length())
        for b in range(low_bits):
            piece(b)

        @pl.when((n >> low_bits) != 0)
        def _():
            for b in range(low_bits, n_bits):
                piece(b)
        return carry

    lax.fori_loop(0, N_EXPERTS, per_expert, 0)


def _dispatch_kernel(n_s, tab_ref, lpos_ref, zpos_ref, h3_ref, xs_ref, st0_s, st1_s, sem):
    j = pl.program_id(0)
    tm = h3_ref.shape[0] // (2 * n_s)
    per_tile = TOP_K * tm
    zrows = MOE_ROWS * n_s

    @pl.when(j == 0)
    def _():
        st0_s[0:zrows, :] = jnp.zeros((zrows, st0_s.shape[1]), F32)

        def fill(start):
            cp = _row_copy(st0_s.at[pl.ds(0, zrows)],
                           xs_ref.at[pl.ds(pl.multiple_of(start * n_s, n_s), zrows)], sem.at[0])
            cp.start()
            cp.wait()

        for e in range(N_EXPERTS):
            fill(zpos_ref[e])

        def tail(t, carry):
            fill(t * MOE_ROWS)
            return carry

        lax.fori_loop(zpos_ref[N_EXPERTS], xs_ref.shape[0] // zrows, tail, 0)

    def drain(stage, slot):
        _row_copy(stage, xs_ref.at[pl.ds(0, per_tile * n_s)], sem.at[slot]).wait()

    def scatter(stage, slot):
        @pl.when(j > 0)
        def _():
            drain(stage, slot)

        base = slot * per_tile

        def tokens(g, carry):
            for u in range(COMBINE_UNROLL):
                t = g * COMBINE_UNROLL + u
                rows = h3_ref[pl.ds(pl.multiple_of((slot * tm + t) * n_s, n_s), n_s), :]
                for k in range(TOP_K):
                    row = pl.multiple_of(lpos_ref[base + k * tm + t], n_s)
                    stage[pl.ds(row, n_s), :] = rows
            return carry

        lax.fori_loop(0, tm // COMBINE_UNROLL, tokens, 0)
        _for_each_run_piece(
            tab_ref, slot * RUN_TAB, tm, n_s,
            lambda s, b, r: _row_copy(stage.at[pl.ds(b, r)], xs_ref.at[pl.ds(s, r)],
                                      sem.at[slot]).start())

    scatter(st0_s, 0)
    scatter(st1_s, 1)

    @pl.when(j + 1 == pl.num_programs(0))
    def _():
        drain(st0_s, 0)
        drain(st1_s, 1)


def _dispatch(tab, lpos_tiles, zpos, h3, n_slots, n_s):
    rows, lanes = h3.shape
    tm = COMBINE_TILE
    smem = pltpu.SMEM
    return pl.pallas_call(
        functools.partial(_dispatch_kernel, n_s),
        grid=(rows // (2 * tm * n_s),),
        in_specs=[
            pl.BlockSpec((2 * RUN_TAB,), lambda j: (j,), memory_space=smem),
            pl.BlockSpec((2 * TOP_K * tm,), lambda j: (j,), memory_space=smem),
            pl.BlockSpec(memory_space=smem),
            pl.BlockSpec((2 * tm * n_s, lanes), lambda j: (j, 0)),
        ],
        out_specs=pl.BlockSpec(memory_space=pl.ANY),
        out_shape=jax.ShapeDtypeStruct(((n_slots + MOE_ROWS) * n_s, lanes), F32),
        scratch_shapes=[pltpu.VMEM((TOP_K * tm * n_s, lanes), F32),
                        pltpu.VMEM((TOP_K * tm * n_s, lanes), F32),
                        pltpu.SemaphoreType.DMA((2,))],
        compiler_params=_cparams(1),
        name="dispatch",
    )(tab, lpos_tiles, zpos, h3)


def _expert_kernel(blk_e_ref, first_ref, next_ref, nused_ref, xs_ref, w1_hbm, b1_ref, w2_hbm, b2_ref,
                   perm_ref, ys_ref, w1f_s, w2f_s, w1p_s, w2b_s, sem):
    i = pl.program_id(0)
    n_s = w1f_s.shape[0] // LANES
    rows = xs_ref.shape[0] // n_s
    f2 = w1f_s.shape[1]

    def fetch(e):
        return (_row_copy(w1_hbm.at[e], w1f_s, sem.at[0]), _row_copy(w2_hbm.at[e], w2f_s, sem.at[1]))

    @pl.when(i == 0)
    def _():
        for cp in fetch(blk_e_ref[0]):
            cp.start()

    @pl.when(first_ref[i] == 1)
    def _():
        for cp in fetch(blk_e_ref[i]):
            cp.wait()
        for c in range(f2 // DEINT):
            cols = slice(c * DEINT, (c + 1) * DEINT)
            strip = w1f_s[:, cols].astype(BF16)
            w1p_s[:, cols] = jnp.dot(strip, perm_ref[...], preferred_element_type=F32).astype(BF16)
        w2b_s[...] = w2f_s[...].astype(BF16)

        @pl.when(next_ref[i] >= 0)
        def _():
            for cp in fetch(next_ref[i]):
                cp.start()

    @pl.when(i < nused_ref[0])
    def _():
        x = jnp.concatenate([xs_ref[pl.ds(s, rows, stride=n_s), :] for s in range(n_s)],
                            axis=1).astype(BF16)
        h = jnp.dot(x, w1p_s[...], preferred_element_type=F32) + b1_ref[0]
        acts = []
        for c in range(f2 // DEINT):
            xg = jnp.minimum(h[:, c * DEINT:c * DEINT + LANES], SWIGLU_LIMIT)
            xl = jnp.clip(h[:, c * DEINT + LANES:(c + 1) * DEINT], -SWIGLU_LIMIT, SWIGLU_LIMIT)
            acts.append((xg * jax.nn.sigmoid(SWIGLU_ALPHA * xg) * (xl + 1.0)).astype(BF16))
        act = jnp.concatenate(acts, axis=1)
        y = jnp.dot(act, w2b_s[...], preferred_element_type=F32) + b2_ref[0]
        for s in range(n_s):
            ys_ref[pl.ds(s, rows, stride=n_s), :] = y[:, s * LANES:(s + 1) * LANES]

    @pl.when(i >= nused_ref[0])
    def _():
        ys_ref[...] = jnp.zeros_like(ys_ref)


def _experts(blk_e, first, nxt, n_used, xs, w1, b1p, w2, b2, n_blocks):
    _, lanes = xs.shape
    _, d, f2 = w1.shape
    blk = MOE_ROWS * (d // lanes)
    f = w2.shape[1]
    half = jnp.arange(DEINT) // 2 + (jnp.arange(DEINT) % 2) * LANES
    perm = (half[:, None] == jnp.arange(DEINT)[None, :]).astype(BF16)
    xmap = lambda i, be, fi, nx, nu: (jnp.minimum(i, nu[0] - 1), 0)
    emap = lambda i, be, fi, nx, nu: (be[i], 0, 0)
    grid_spec = pltpu.PrefetchScalarGridSpec(
        num_scalar_prefetch=4,
        grid=(n_blocks,),
        in_specs=[
            pl.BlockSpec((blk, lanes), xmap),
            pl.BlockSpec(memory_space=pl.ANY),
            pl.BlockSpec((1, 1, f2), emap),
            pl.BlockSpec(memory_space=pl.ANY),
            pl.BlockSpec((1, 1, d), emap),
            pl.BlockSpec((DEINT, DEINT), lambda i, be, fi, nx, nu: (0, 0)),
        ],
        out_specs=pl.BlockSpec((blk, lanes), lambda i, be, fi, nx, nu: (i, 0)),
        scratch_shapes=[pltpu.VMEM((d, f2), F32), pltpu.VMEM((f, d), F32),
                        pltpu.VMEM((d, f2), BF16), pltpu.VMEM((f, d), BF16),
                        pltpu.SemaphoreType.DMA((2,))],
    )
    return pl.pallas_call(
        _expert_kernel,
        grid_spec=grid_spec,
        out_shape=jax.ShapeDtypeStruct((n_blocks * blk, lanes), F32),
        compiler_params=_cparams(1),
        name="experts",
    )(blk_e, first, nxt, n_used, xs, w1, b1p, w2, b2, perm)


def _combine_kernel(tab_ref, nxt_ref, lpos_ref, gw_ref, x1_ref, gfin_ref, ys_ref, out_ref,
                    buf0_s, buf1_s, acc_s, sem):
    j = pl.program_id(0)
    tm = x1_ref.shape[0] // 2
    n_s = x1_ref.shape[1] // LANES
    per_tile = TOP_K * tm
    tab_w = RUN_TAB

    def gather(tab, tb, buf, slot):
        _for_each_run_piece(
            tab, tb, tm, n_s,
            lambda s, b, r: _row_copy(ys_ref.at[pl.ds(s, r)], buf.at[pl.ds(b, r)],
                                      sem.at[slot]).start())

    def reduce(buf, slot, t0):
        _row_copy(ys_ref.at[pl.ds(0, per_tile * n_s)], buf, sem.at[slot]).wait()
        base = (t0 // tm) * per_tile

        def tokens(g, carry):
            for u in range(COMBINE_UNROLL):
                t = g * COMBINE_UNROLL + u
                acc = None
                for k in range(TOP_K):
                    row = pl.multiple_of(lpos_ref[base + k * tm + t], n_s)
                    term = gw_ref[base + k * tm + t] * buf[pl.ds(row, n_s), :]
                    acc = term if acc is None else acc + term
                acc_s[pl.ds(pl.multiple_of(t * n_s, n_s), n_s), :] = acc
            return carry

        lax.fori_loop(0, tm // COMBINE_UNROLL, tokens, 0)
        y = jnp.concatenate([acc_s[pl.ds(s, tm, stride=n_s), :] for s in range(n_s)], axis=1)
        acc = x1_ref[t0:t0 + tm, :] + y
        ms = jnp.mean(acc * acc, axis=-1, keepdims=True)
        out_ref[t0:t0 + tm, :] = acc * lax.rsqrt(ms + RMS_EPS) * gfin_ref[...]

    @pl.when(j == 0)
    def _():
        gather(tab_ref, 0, buf0_s, 0)

    gather(tab_ref, tab_w, buf1_s, 1)
    reduce(buf0_s, 0, 0)

    @pl.when(j + 1 < pl.num_programs(0))
    def _():
        gather(nxt_ref, 0, buf0_s, 0)

    reduce(buf1_s, 1, tm)


def _combine(tab, lpos_tiles, gw_tiles, x1, g_final, ys):
    m, d = x1.shape
    _, lanes = ys.shape
    s = d // lanes
    tm = COMBINE_TILE
    n_pairs = m // (2 * tm)
    tab_w = RUN_TAB
    smem = pltpu.SMEM
    return pl.pallas_call(
        _combine_kernel,
        grid=(n_pairs,),
        in_specs=[
            pl.BlockSpec((2 * tab_w,), lambda j: (j,), memory_space=smem),
            pl.BlockSpec((tab_w,), lambda j: (jnp.minimum(2 * j + 2, 2 * n_pairs - 1),),
                         memory_space=smem),
            pl.BlockSpec((2 * TOP_K * tm,), lambda j: (j,), memory_space=smem),
            pl.BlockSpec((2 * TOP_K * tm,), lambda j: (j,), memory_space=smem),
            pl.BlockSpec((2 * tm, d), lambda j: (j, 0)),
            pl.BlockSpec((1, d), lambda j: (0, 0)),
            pl.BlockSpec(memory_space=pl.ANY),
        ],
        out_specs=pl.BlockSpec((2 * tm, d), lambda j: (j, 0)),
        out_shape=jax.ShapeDtypeStruct((m, d), F32),
        scratch_shapes=[pltpu.VMEM((TOP_K * tm * s, lanes), F32),
                        pltpu.VMEM((TOP_K * tm * s, lanes), F32),
                        pltpu.VMEM((tm * s, lanes), F32),
                        pltpu.SemaphoreType.DMA((2,))],
        compiler_params=_cparams(1),
        name="combine",
    )(tab, tab, lpos_tiles, gw_tiles, x1, g_final, ys)


def kernel(x, norm_mix_g, w_in, b_gate, pool_w, pool_scale, ssm_lambda_re, ssm_lambda_im, ssm_log_dt, ssm_b_re, ssm_b_im, ssm_c_re, ssm_c_im, ssm_d, ssm_glu_w, ssm_glu_b, w_pool_up, w_ssm_up, w_out, norm_ffn_g, router_w, router_b, moe_w1, moe_b1, moe_w2, moe_b2, norm_final_g):
    bsz, seq, d = x.shape
    depth = w_in.shape[0]
    pw = pool_w.shape[1] * pool_w.shape[2]
    n_groups = ssm_lambda_re.shape[1]
    sw = n_groups * SSM_GROUP_DIM
    m = bsz * seq
    assert depth == 1
    assert bsz == SUBLANES and seq % ROW_TILE == 0 and seq % SSM_TILE == 0 and d % LANES == 0
    assert pool_w.shape[1] == len(POOL_WINDOWS) and n_groups % SSM_PACK == 0
    assert all(w & (w - 1) == 0 and w <= POOL_HALO for w in POOL_WINDOWS)
    assert moe_w1.shape[3] % DEINT == 0 and moe_w1.shape[1] == N_EXPERTS
    assert m % INPROJ_TILE == 0 and m % (2 * COMBINE_TILE) == 0
    assert COMBINE_TILE % COMBINE_UNROLL == 0

    x2 = x.reshape(m, d)
    for l in range(depth):
        wi = w_in[l].astype(BF16)
        u_pool, u_ssm, gates = _inproj(
            x2, norm_mix_g[l][None], wi[:, :pw], wi[:, pw:pw + sw], wi[:, pw + sw:], b_gate[l][None])
        mats = _ssm_matrices(ssm_lambda_re[l], ssm_lambda_im[l], ssm_log_dt[l], ssm_b_re[l],
                             ssm_b_im[l], ssm_c_re[l], ssm_c_im[l], ssm_d[l])
        y_ssm = _ssm(u_ssm.reshape(bsz, seq, sw), mats).reshape(m, sw)
        x1, h3, idx_t, gw_t, rank_t, cnt = _mix(
            x2, u_pool, y_ssm, gates, pool_w[l].astype(BF16), pool_scale[l][None],
            ssm_glu_w[l].astype(BF16), ssm_glu_b[l][None], w_pool_up[l].astype(BF16),
            w_ssm_up[l].astype(BF16), w_out[l].astype(BF16), norm_ffn_g[l][None],
            router_w[l].T, router_b[l][:, None], seq)

        n_assign = m * TOP_K
        n_blocks = (n_assign + N_EXPERTS * (MOE_ROWS - 1) + MOE_ROWS - 1) // MOE_ROWS
        n_slots = n_blocks * MOE_ROWS
        counts = cnt[:, 0].astype(jnp.int32)
        padded = (counts + MOE_ROWS - 1) // MOE_ROWS * MOE_ROWS
        pad_end = jnp.cumsum(padded)
        pad_start = pad_end - padded
        n_used = (pad_end[-1] // MOE_ROWS).astype(jnp.int32)[None]
        blk_row = jnp.arange(n_blocks, dtype=jnp.int32) * MOE_ROWS
        blk_e = jnp.minimum(jnp.sum(pad_end[None, :] <= blk_row[:, None], axis=1),
                            N_EXPERTS - 1).astype(jnp.int32)
        blk_ids = jnp.arange(n_blocks, dtype=jnp.int32)
        first = jnp.concatenate([jnp.ones((1,), jnp.int32),
                                 (blk_e[1:] != blk_e[:-1]).astype(jnp.int32)])
        first = jnp.where(blk_ids < n_used[0], first, 0)
        ex = jnp.arange(N_EXPERTS, dtype=jnp.int32)
        later = (ex[None, :] > ex[:, None]) & (counts[None, :] > 0)
        next_e = jnp.min(jnp.where(later, ex[None, :], N_EXPERTS), axis=1)
        next_e = jnp.where(next_e < N_EXPERTS, next_e, -1).astype(jnp.int32)
        nxt = jnp.sum(jnp.where(blk_e[:, None] == ex[None, :], next_e[None, :], 0), axis=1)
        e_ids = jnp.arange(N_EXPERTS, dtype=jnp.int32)[:, None, None]
        zpos = jnp.concatenate([pad_start + counts, n_used]).astype(jnp.int32)
        ct = COMBINE_TILE
        n_ct = m // ct
        cnt_te = jnp.sum((idx_t.reshape(TOP_K, n_ct, ct)[None] == e_ids[..., None]).astype(jnp.int32),
                         axis=(1, 3))
        carry_te = jnp.cumsum(cnt_te, axis=1) - cnt_te
        offs_te = jnp.cumsum(cnt_te, axis=0) - cnt_te
        tab = jnp.stack([cnt_te, offs_te, pad_start[:, None] + carry_te, jnp.zeros_like(cnt_te)],
                        axis=0)
        tab = tab.transpose(2, 0, 1).reshape(-1).astype(jnp.int32)
        delta = jnp.repeat(offs_te - carry_te, ct, axis=1)
        lpos = rank_t + jnp.sum(jnp.where(idx_t[None] == e_ids, delta[:, None, :], 0), axis=0)
        tile_major = lambda a: a.reshape(TOP_K, n_ct, ct).transpose(1, 0, 2).reshape(-1)
        lpos_tiles = tile_major((lpos * (d // LANES)).astype(jnp.int32))

        xs = _dispatch(tab, lpos_tiles, zpos, h3, n_slots, d // LANES)
        f2 = moe_w1.shape[3]
        b1p = (moe_b1[l].reshape(N_EXPERTS, f2 // DEINT, LANES, 2)
               .transpose(0, 1, 3, 2).reshape(N_EXPERTS, 1, f2))
        ys = _experts(blk_e, first, nxt, n_used, xs, moe_w1[l], b1p, moe_w2[l], moe_b2[l][:, None, :],
                      n_blocks)
        x2 = _combine(tab, lpos_tiles, tile_major(gw_t), x1, norm_final_g[None], ys)
    return x2.reshape(bsz, seq, d)
```
